```python
import math
import jax, jax.numpy as jnp
from jax import lax
import numpy as np

D_MODEL = 1024
BATCH = 16
SEQ = 256
DEPTH = 2
DEC_BATCH = 8
DEC_SEQ = 4096
PAST_LEN = 256

GRID_W = 64
EPS = 1e-6
GATE_FLOOR = 1e-30
GDN_HEADS = 4
GDN_DK = 64
GDN_DV = 64
GDN_CONV = 5
GDN_CHUNK = 64
GDN_QK = GDN_HEADS * GDN_DK
GDN_W = GDN_HEADS * GDN_DV
GDN_CONV_CH = 2 * GDN_QK + GDN_W
HG_HEADS = 4
HG_DK = 64
HG_DV = 64
HG_CHUNK = 64
HG_QK = HG_HEADS * HG_DK
HG_W = HG_HEADS * HG_DV
MLA_HEADS = 4
MLA_Q_RANK = 384
MLA_KV_RANK = 256
MLA_NOPE = 128
MLA_ROPE = 64
MLA_DV = 128
MLA_QK = MLA_NOPE + MLA_ROPE
MLA_W = MLA_HEADS * MLA_DV
ROPE_BASE = 10000.0
Q_BLOCK = 128
MIX_W = GDN_W + HG_W + MLA_W
IN_SIZES = (GDN_QK, GDN_QK, GDN_W, GDN_W, 2 * GDN_HEADS, 2 * GDN_HEADS,
            HG_QK, HG_W, 2 * HG_QK, HG_W,
            MLA_Q_RANK, MLA_KV_RANK, MLA_ROPE)
IN_DIM = sum(IN_SIZES)
D_FF = -(-8 * D_MODEL // (3 * 256)) * 256

kernel_name = 'hybrid_gdn_hgrn2_mla_diffusion_step'

F32 = jnp.float32


def _split(x, sizes):
    out, off = [], 0
    for s in sizes:
        out.append(x[..., off:off + s])
        off += s
    return out


def rmsnorm(x, w):
    xf = x.astype(F32)
    y = xf * lax.rsqrt(jnp.mean(xf * xf, axis=-1, keepdims=True) + EPS)
    return (y * w.astype(F32)).astype(x.dtype)


def l2norm(x):
    return x * lax.rsqrt(jnp.sum(x * x, axis=-1, keepdims=True) + EPS)


def centred_dwconv(x, w):
    k = w.shape[-1]
    return lax.conv_general_dilated(
        x, w.T[:, None, :].astype(x.dtype), window_strides=(1,),
        padding=[(k // 2, k // 2)], dimension_numbers=('NWC', 'WIO', 'NWC'),
        feature_group_count=x.shape[-1])


def axial_rope(n):
    rows = n // GRID_W
    row = jnp.repeat(jnp.arange(rows, dtype=F32), GRID_W)
    col = jnp.tile(jnp.arange(GRID_W, dtype=F32), rows)
    nf = MLA_ROPE // 4
    inv = ROPE_BASE ** (-jnp.arange(nf, dtype=F32) / nf)
    ang = jnp.stack([row[:, None] * inv, col[:, None] * inv], axis=1)
    return jnp.cos(ang), jnp.sin(ang)


def apply_rope(x, cos, sin):
    xs = x.astype(F32).reshape(x.shape[:-1] + (2, 2, MLA_ROPE // 4))
    x1, x2 = xs[..., 0, :], xs[..., 1, :]
    out = jnp.stack([x1 * cos - x2 * sin, x2 * cos + x1 * sin], axis=-2)
    return out.reshape(x.shape).astype(x.dtype)


def _heads_first(t, n_chunks, chunk):
    b, _, h = t.shape[:3]
    t = t.reshape((b, n_chunks, chunk, h) + t.shape[3:])
    return t.transpose((0, 3, 1, 2) + tuple(range(4, t.ndim)))


def gdn_scan(q, k, v, g, beta, s0):
    bsz, t_len, h, dk = q.shape
    dv = v.shape[-1]
    c = GDN_CHUNK
    n = t_len // c
    qb = _heads_first(q * dk ** -0.5, n, c)
    kb = _heads_first(k, n, c)
    vb = _heads_first(v, n, c)
    gc = jnp.cumsum(_heads_first(g, n, c), axis=-1)
    bt = _heads_first(beta, n, c)[..., None]
    idx = jnp.arange(c)
    incl = idx[:, None] >= idx[None, :]
    strict = idx[:, None] > idx[None, :]
    diff = gc[..., :, None] - gc[..., None, :]
    decay = jnp.where(incl, jnp.exp(jnp.where(incl, diff, 0.0)), 0.0)
    k_beta = kb * bt
    m = jnp.where(strict, jnp.einsum('bhnik,bhnjk->bhnij', k_beta, kb) * decay, 0.0)
    a_mat = m + jnp.eye(c, dtype=m.dtype)
    rhs = jnp.concatenate([vb * bt, k_beta * jnp.exp(gc)[..., None]], axis=-1)
    sol = lax.linalg.triangular_solve(a_mat, rhs, left_side=True, lower=True, unit_diagonal=True)
    u, w = sol[..., :dv], sol[..., dv:]
    att = jnp.where(incl, jnp.einsum('bhnik,bhnjk->bhnij', qb, kb) * decay, 0.0)
    q_dec = qb * jnp.exp(gc)[..., None]
    k_tail = kb * jnp.exp(gc[..., -1:] - gc)[..., None]
    g_last = jnp.exp(gc[..., -1])

    def step(s, xs):
        att_i, u_i, w_i, qd_i, kt_i, gl_i = xs
        v_new = u_i - jnp.einsum('bhck,bhkv->bhcv', w_i, s)
        o = jnp.einsum('bhck,bhkv->bhcv', qd_i, s) + jnp.einsum('bhij,bhjv->bhiv', att_i, v_new)
        s = s * gl_i[..., None, None] + jnp.einsum('bhck,bhcv->bhkv', kt_i, v_new)
        return s, o

    xs = tuple(jnp.moveaxis(a, 2, 0) for a in (att, u, w, q_dec, k_tail, g_last))
    s_fin, o = lax.scan(step, s0.astype(F32), xs)
    return o.transpose(1, 0, 3, 2, 4).reshape(bsz, t_len, h, dv), s_fin


def hgrn_scan(q, k, v, logf, s0):
    bsz, t_len, h, _ = q.shape
    dv = v.shape[-1]
    c = HG_CHUNK
    n = t_len // c
    idx = jnp.arange(c)
    causal = (idx[:, None] >= idx[None, :])[:, :, None]

    def step(s, xs):
        qc, kc, vc, lf = xs
        bc = jnp.cumsum(lf, axis=2)
        diff = bc[:, :, :, None, :] - bc[:, :, None, :, :]
        dec = jnp.where(causal, jnp.exp(jnp.where(causal, diff, 0.0)), 0.0)
        att = jnp.einsum('bhtk,bhtsk,bhsk->bhts', qc, dec, kc)
        o = jnp.einsum('bhts,bhsv->bhtv', att, vc) + jnp.einsum('bhtk,bhkv->bhtv', qc * jnp.exp(bc), s)
        bl = bc[:, :, -1, :]
        s = s * jnp.exp(bl)[..., None] + jnp.einsum('bhsk,bhsv->bhkv', kc * jnp.exp(bl[:, :, None, :] - bc), vc)
        return s, o

    xs = tuple(jnp.moveaxis(_heads_first(a, n, c), 2, 0) for a in (q, k, v, logf))
    s_fin, o = lax.scan(step, s0.astype(F32), xs)
    return o.transpose(1, 0, 3, 2, 4).reshape(bsz, t_len, h, dv), s_fin


def _flip(t):
    return jnp.flip(t, axis=1)


def gdn_mixer(q, k, v, z, a, b, conv_w, a_log, dt_bias, norm_w, s_fwd, s_bwd):
    bsz, t_len, _ = q.shape
    qkv = jax.nn.silu(centred_dwconv(jnp.concatenate([q, k, v], axis=-1), conv_w)).astype(F32)
    q, k, v = _split(qkv, (GDN_QK, GDN_QK, GDN_W))
    q = l2norm(q.reshape(bsz, t_len, GDN_HEADS, GDN_DK))
    k = l2norm(k.reshape(bsz, t_len, GDN_HEADS, GDN_DK))
    v = v.reshape(bsz, t_len, GDN_HEADS, GDN_DV)
    a = a.astype(F32).reshape(bsz, t_len, 2, GDN_HEADS)
    b = b.astype(F32).reshape(bsz, t_len, 2, GDN_HEADS)
    g = -jnp.exp(a_log.astype(F32)) * jax.nn.softplus(a + dt_bias.astype(F32))
    beta = jax.nn.sigmoid(b)
    o_f, sf = gdn_scan(q, k, v, g[:, :, 0], beta[:, :, 0], s_fwd)
    o_b, sb = gdn_scan(_flip(q), _flip(k), _flip(v), _flip(g[:, :, 1]), _flip(beta[:, :, 1]), s_bwd)
    o = o_f + _flip(o_b)
    o = rmsnorm(o, norm_w) * jax.nn.silu(z.astype(F32).reshape(bsz, t_len, GDN_HEADS, GDN_DV))
    return o.reshape(bsz, t_len, GDN_W), sf, sb


def hgrn_mixer(q, i, f, g, lb, norm_w, s_fwd, s_bwd):
    bsz, t_len, _ = q.shape
    q = q.astype(F32).reshape(bsz, t_len, HG_HEADS, HG_DK)
    v = i.astype(F32).reshape(bsz, t_len, HG_HEADS, HG_DV)
    f = f.astype(F32).reshape(bsz, t_len, 2, HG_HEADS, HG_DK)
    lb = lb.astype(F32).reshape(HG_HEADS, HG_DK)
    gate = lb + (1.0 - lb) * jax.nn.sigmoid(f)
    logf = jnp.log(jnp.maximum(gate, GATE_FLOOR))
    k = (1.0 - lb) * jax.nn.sigmoid(-f)
    o_f, sf = hgrn_scan(q, k[:, :, 0], v, logf[:, :, 0], s_fwd)
    o_b, sb = hgrn_scan(_flip(q), _flip(k[:, :, 1]), _flip(v), _flip(logf[:, :, 1]), s_bwd)
    o = o_f + _flip(o_b)
    o = rmsnorm(o, norm_w) * jax.nn.sigmoid(g.astype(F32).reshape(bsz, t_len, HG_HEADS, HG_DV))
    return o.reshape(bsz, t_len, HG_W), sf, sb


def mla_keys(c_kv, k_rope, w_ukv):
    bsz, s_len, _ = c_kv.shape
    kv = (c_kv @ w_ukv).reshape(bsz, s_len, MLA_HEADS, MLA_NOPE + MLA_DV)
    k = jnp.concatenate([kv[..., :MLA_NOPE],
                         jnp.broadcast_to(k_rope[:, :, None, :], (bsz, s_len, MLA_HEADS, MLA_ROPE))], axis=-1)
    return k, kv[..., MLA_NOPE:]


def blocked_attention(q, k, v):
    bsz, t_len, h, dq = q.shape
    nb = t_len // Q_BLOCK
    scale = dq ** -0.5
    qb = q.reshape(bsz, nb, Q_BLOCK, h, dq).transpose(1, 0, 2, 3, 4)

    def one_block(qi):
        s = jnp.einsum('bqhd,bshd->bhqs', qi, k, preferred_element_type=F32) * scale
        p = jax.nn.softmax(s, axis=-1)
        return jnp.einsum('bhqs,bshd->bqhd', p.astype(v.dtype), v)

    o = lax.map(one_block, qb)
    return o.transpose(1, 0, 2, 3, 4).reshape(bsz, t_len, h, v.shape[-1])


def trunk_layer(x, cond, p, gdn_s0, hgrn_s0, rope, ctx_ckv, ctx_kr):
    bsz, t_len, _ = x.shape
    mod = (jax.nn.silu(cond) @ p['w_ada'] + p['b_ada'])[:, None, :]
    sh1, sc1, gt1, sh2, sc2, gt2 = jnp.split(mod, 6, axis=-1)
    h = rmsnorm(x, p['g_pre_mix']) * (1.0 + sc1) + sh1
    (gq, gk, gv, gz, ga, gb, hq, hi, hf, hg, mcq, mckv, mkr) = _split(h @ p['w_in'], IN_SIZES)
    o_gdn, gsf, gsb = gdn_mixer(gq, gk, gv, gz, ga, gb, p['gdn_conv_w'], p['gdn_a_log'],
                                p['gdn_dt_bias'], p['gdn_norm_w'], gdn_s0[:, 0], gdn_s0[:, 1])
    o_hg, hsf, hsb = hgrn_mixer(hq, hi, hf, hg, p['hgrn_lb'], p['hgrn_norm_w'],
                                hgrn_s0[:, 0], hgrn_s0[:, 1])
    q = (rmsnorm(mcq, p['mla_q_norm_w']) @ p['mla_w_uq']).reshape(bsz, t_len, MLA_HEADS, MLA_QK)
    c_kv = rmsnorm(mckv, p['mla_kv_norm_w'])
    if rope is None:
        k, v = mla_keys(c_kv, mkr, p['mla_w_ukv'])
    else:
        cos, sin = rope
        q = jnp.concatenate([q[..., :MLA_NOPE],
                             apply_rope(q[..., MLA_NOPE:], cos[:, None], sin[:, None])], axis=-1)
        k_lat, v_lat = mla_keys(c_kv, apply_rope(mkr, cos, sin), p['mla_w_ukv'])
        k_ctx, v_ctx = mla_keys(ctx_ckv, ctx_kr, p['mla_w_ukv'])
        k = jnp.concatenate([k_lat, k_ctx], axis=1)
        v = jnp.concatenate([v_lat, v_ctx], axis=1)
    o_mla = blocked_attention(q, k, v).reshape(bsz, t_len, MLA_W)
    mix = jnp.concatenate([o_gdn.astype(x.dtype), o_hg.astype(x.dtype), o_mla.astype(x.dtype)], axis=-1)
    x = x + gt1 * rmsnorm(mix @ p['w_out'], p['g_post_mix'])
    h = rmsnorm(x, p['g_pre_ffn']) * (1.0 + sc2) + sh2
    ff_a, ff_b = jnp.split(h @ p['w_ffn_in'], 2, axis=-1)
    x = x + gt2 * rmsnorm((jax.nn.silu(ff_a) * ff_b) @ p['w_ffn_out'], p['g_post_ffn'])
    return x, jnp.stack([gsf, gsb], axis=1), jnp.stack([hsf, hsb], axis=1), c_kv, mkr


def setup_inputs(seed: int = 0) -> dict:
    key = jax.random.key(seed)
    ks = jax.random.split(key, 32)

    def nrm(k, shape, s):
        return jax.random.normal(k, shape, F32) * s

    def gain(k, shape):
        return 1.0 + 0.05 * jax.random.normal(k, shape, F32)

    dt = jnp.exp(jax.random.uniform(ks[18], (DEPTH, 2, GDN_HEADS), F32, math.log(1e-3), math.log(1e-1)))
    return {
        'x_prompt': nrm(ks[0], (BATCH, SEQ, D_MODEL), 1.0),
        'x_sample': nrm(ks[1], (DEC_BATCH, DEC_SEQ, D_MODEL), 1.0),
        'cache_mla_ckv': nrm(ks[2], (DEC_BATCH, DEPTH, PAST_LEN, MLA_KV_RANK), 1.0),
        'cache_mla_krope': nrm(ks[3], (DEC_BATCH, DEPTH, PAST_LEN, MLA_ROPE), 1.0),
        'state_gdn': nrm(ks[4], (DEC_BATCH, DEPTH, 2, GDN_HEADS, GDN_DK, GDN_DV), 0.3),
        'state_hgrn': nrm(ks[5], (DEC_BATCH, DEPTH, 2, HG_HEADS, HG_DK, HG_DV), 0.5),
        'c': nrm(ks[6], (DEC_BATCH, D_MODEL), 1.0),
        'c_ctx': nrm(ks[7], (D_MODEL,), 1.0),
        'w_ada': nrm(ks[8], (DEPTH, D_MODEL, 6 * D_MODEL), 0.5 * D_MODEL ** -0.5),
        'b_ada': nrm(ks[9], (DEPTH, 6 * D_MODEL), 0.02),
        'g_pre_mix': gain(ks[10], (DEPTH, D_MODEL)),
        'g_post_mix': gain(ks[11], (DEPTH, D_MODEL)),
        'g_pre_ffn': gain(ks[12], (DEPTH, D_MODEL)),
        'g_post_ffn': gain(ks[13], (DEPTH, D_MODEL)),
        'w_in': nrm(ks[14], (DEPTH, D_MODEL, IN_DIM), D_MODEL ** -0.5),
        'w_out': nrm(ks[15], (DEPTH, MIX_W, D_MODEL), MIX_W ** -0.5),
        'gdn_conv_w': nrm(ks[16], (DEPTH, GDN_CONV_CH, GDN_CONV), GDN_CONV ** -0.5),
        'gdn_a_log': jnp.log(jax.random.uniform(ks[17], (DEPTH, 2, GDN_HEADS), F32, 1.0, 16.0)),
        'gdn_dt_bias': dt + jnp.log(-jnp.expm1(-dt)),
        'gdn_norm_w': gain(ks[19], (DEPTH, GDN_DV)),
        'hgrn_lb': nrm(ks[20], (DEPTH, HG_QK), 1.0),
        'hgrn_norm_w': gain(ks[21], (DEPTH, HG_DV)),
        'mla_q_norm_w': gain(ks[22], (DEPTH, MLA_Q_RANK)),
        'mla_w_uq': nrm(ks[23], (DEPTH, MLA_Q_RANK, MLA_HEADS * MLA_QK), MLA_Q_RANK ** -0.5),
        'mla_kv_norm_w': gain(ks[24], (DEPTH, MLA_KV_RANK)),
        'mla_w_ukv': nrm(ks[25], (DEPTH, MLA_KV_RANK, MLA_HEADS * (MLA_NOPE + MLA_DV)), MLA_KV_RANK ** -0.5),
        'w_ffn_in': nrm(ks[26], (DEPTH, D_MODEL, 2 * D_FF), D_MODEL ** -0.5),
        'w_ffn_out': nrm(ks[27], (DEPTH, D_FF, D_MODEL), D_FF ** -0.5),
    }


def reference(x_prompt, x_sample, cache_mla_ckv, cache_mla_krope, state_gdn, state_hgrn, c, c_ctx,
              w_ada, b_ada, g_pre_mix, g_post_mix, g_pre_ffn, g_post_ffn, w_in, w_out,
              gdn_conv_w, gdn_a_log, gdn_dt_bias, gdn_norm_w, hgrn_lb, hgrn_norm_w,
              mla_q_norm_w, mla_w_uq, mla_kv_norm_w, mla_w_ukv, w_ffn_in, w_ffn_out):
    gamma = jax.nn.softmax(hgrn_lb.astype(F32), axis=0)
    lower_bounds = jnp.cumsum(gamma, axis=0) - gamma[0:1]

    def layer_params(l):
        return {'w_ada': w_ada[l], 'b_ada': b_ada[l], 'g_pre_mix': g_pre_mix[l],
                'g_post_mix': g_post_mix[l], 'g_pre_ffn': g_pre_ffn[l], 'g_post_ffn': g_post_ffn[l],
                'w_in': w_in[l], 'w_out': w_out[l], 'gdn_conv_w': gdn_conv_w[l],
                'gdn_a_log': gdn_a_log[l], 'gdn_dt_bias': gdn_dt_bias[l], 'gdn_norm_w': gdn_norm_w[l],
                'hgrn_lb': lower_bounds[l], 'hgrn_norm_w': hgrn_norm_w[l],
                'mla_q_norm_w': mla_q_norm_w[l], 'mla_w_uq': mla_w_uq[l],
                'mla_kv_norm_w': mla_kv_norm_w[l], 'mla_w_ukv': mla_w_ukv[l],
                'w_ffn_in': w_ffn_in[l], 'w_ffn_out': w_ffn_out[l]}

    bp = x_prompt.shape[0]
    zero_gdn = jnp.zeros((bp, 2, GDN_HEADS, GDN_DK, GDN_DV), F32)
    zero_hg = jnp.zeros((bp, 2, HG_HEADS, HG_DK, HG_DV), F32)
    xp = x_prompt
    ckv_list, kr_list, gdn_list, hg_list = [], [], [], []
    for l in range(DEPTH):
        xp, s_g, s_h, ckv_l, kr_l = trunk_layer(xp, c_ctx[None, :], layer_params(l),
                                                zero_gdn, zero_hg, None, None, None)
        ckv_list.append(ckv_l)
        kr_list.append(kr_l)
        gdn_list.append(s_g)
        hg_list.append(s_h)
    new_mla_ckv = jnp.stack(ckv_list, axis=1).astype(x_prompt.dtype)
    new_mla_krope = jnp.stack(kr_list, axis=1).astype(x_prompt.dtype)
    new_state_gdn = jnp.stack(gdn_list, axis=1).astype(x_prompt.dtype)
    new_state_hgrn = jnp.stack(hg_list, axis=1).astype(x_prompt.dtype)

    rope = axial_rope(x_sample.shape[1])
    xs = x_sample
    for l in range(DEPTH):
        xs, _, _, _, _ = trunk_layer(xs, c, layer_params(l), state_gdn[:, l], state_hgrn[:, l],
                                     rope, cache_mla_ckv[:, l], cache_mla_krope[:, l])

    return (xp, xs, new_mla_ckv, new_mla_krope, new_state_gdn, new_state_hgrn)
```

```python
import functools
import math

import numpy as np
import jax
import jax.numpy as jnp
from jax import lax
from jax.experimental import pallas as pl
from jax.experimental.pallas import tpu as pltpu

F32 = jnp.float32
BF16 = jnp.bfloat16

D_MODEL = 1024
DEPTH = 2
GRID_W = 64
EPS = 1e-6
GATE_FLOOR = 1e-30
HEADS = 4
HD = 64
LIN_W = HEADS * HD
GDN_CHUNK = 64
HG_CHUNK = 16
SCAN_BLOCK = 128
MLA_Q_RANK = 384
MLA_KV_RANK = 256
MLA_NOPE = 128
MLA_ROPE = 64
MLA_DV = 128
MLA_QK = MLA_NOPE + MLA_ROPE
MLA_HEAD_PAD = 256
ROPE_BASE = 10000.0
D_FF = -(-8 * D_MODEL // (3 * 256)) * 256
FF_SPLIT = 2
TOKEN_TILE = 512
ATTN_Q_TILE = 256
MOD_ROWS = 16
VMEM_LIMIT = 56 * 1024 * 1024

_OFF = {}
_o = 0
for _n, _s in (("gq", 256), ("gk", 256), ("gv", 256), ("gz", 256), ("ga", 8), ("gb", 8),
               ("hq", 256), ("hi", 256), ("hf", 512), ("hg", 256),
               ("mcq", MLA_Q_RANK), ("mckv", MLA_KV_RANK), ("mkr", MLA_ROPE)):
    _OFF[_n] = _o
    _o += _s
IN_DIM = _o

G_W = 1280
G_CONV = 1024
H_W = 1280
A_W = 896
A_KR = MLA_Q_RANK + MLA_KV_RANK
A_AB = A_KR + 2 * MLA_ROPE


def _rope_swap_idx():
    j = np.arange(MLA_ROPE)
    return np.where((j % 32) < 16, j + 16, j - 16)


def _in_perm():
    ar = np.arange
    vk = np.concatenate([np.concatenate([_OFF["gv"] + 64 * h + ar(64), _OFF["gk"] + 64 * h + ar(64)])
                         for h in range(HEADS)])
    g = np.concatenate([_OFF["gq"] + ar(256), _OFF["gk"] + ar(256), vk, _OFF["gz"] + ar(256)])
    hh = np.concatenate([_OFF["hq"] + ar(256), _OFF["hi"] + ar(256), _OFF["hf"] + ar(512), _OFF["hg"] + ar(256)])
    a = np.concatenate([_OFF["mcq"] + ar(MLA_Q_RANK), _OFF["mckv"] + ar(MLA_KV_RANK), _OFF["mkr"] + ar(MLA_ROPE),
                        _OFF["mkr"] + _rope_swap_idx(), _OFF["ga"] + ar(8), _OFF["gb"] + ar(8)])
    conv = np.concatenate([ar(256), 256 + ar(256),
                           np.concatenate([np.concatenate([512 + 64 * h + ar(64), 256 + 64 * h + ar(64)])
                                           for h in range(HEADS)])])
    return g, hh, a, conv


_PERM_G, _PERM_H, _PERM_A, _PERM_CONV = _in_perm()


def _uq_perm():
    sw = _rope_swap_idx()
    cols = []
    for h in range(HEADS):
        base = h * MLA_QK
        cols += [base + np.arange(MLA_NOPE), base + MLA_NOPE + np.arange(MLA_ROPE), base + MLA_NOPE + sw]
    return np.concatenate(cols)


def _ukv_perm():
    per = MLA_NOPE + MLA_DV
    kn = np.concatenate([h * per + np.arange(MLA_NOPE) for h in range(HEADS)])
    vv = np.concatenate([h * per + MLA_NOPE + np.arange(MLA_DV) for h in range(HEADS)])
    return np.concatenate([kn, vv])


_PERM_UQ = _uq_perm()
_PERM_UKV = _ukv_perm()


def _seg_ones(n):
    i = np.arange(n) // HD
    return jnp.asarray((i[:, None] == i[None, :]).astype(np.float32), dtype=BF16)


def _rms(x, w):
    return x * lax.rsqrt(jnp.mean(x * x, axis=-1, keepdims=True) + EPS) * w


def _dot(a, b):
    return jnp.dot(a, b, preferred_element_type=F32)


def _dot_t(a, b):
    return lax.dot_general(a, b, (((1,), (1,)), ((), ())), preferred_element_type=F32)


def _split3(x):
    hi = x.astype(BF16)
    r1 = x - hi.astype(F32)
    mid = r1.astype(BF16)
    lo = (r1 - mid.astype(F32)).astype(BF16)
    return hi, mid, lo


def _split2(x):
    hi = x.astype(BF16)
    return hi, (x - hi.astype(F32)).astype(BF16)


def _seg_sum(x, seg):
    hi, lo = _split2(x)
    return _dot(hi, seg) + _dot(lo, seg)


def _mask_dot(mask01, x):
    hi, mid, lo = _split3(x)
    return _dot(mask01, hi) + _dot(mask01, mid) + _dot(mask01, lo)


def _dot_mask_t(x, mask01):
    hi, mid, lo = _split3(x)
    return _dot_t(hi, mask01) + _dot_t(mid, mask01) + _dot_t(lo, mask01)


def _softplus(x):
    return jnp.maximum(x, 0.0) + jnp.log1p(jnp.exp(-jnp.abs(x)))


def _sigmoid(x):
    return jax.nn.sigmoid(x)


def _silu(x):
    return x * jax.nn.sigmoid(x)


def _ada_kernel(c_ref, w_ref, b_ref, o_ref):
    s = _silu(c_ref[...]).astype(BF16)
    o_ref[0] = _dot(s, w_ref[0].astype(BF16)) + b_ref[0]


def _ada(cond, w_ada, b_ada):
    n = w_ada.shape[-1]
    tn = 1536
    return pl.pallas_call(
        _ada_kernel,
        grid=(DEPTH, n // tn),
        in_specs=[pl.BlockSpec((MOD_ROWS, D_MODEL), lambda l, j: (0, 0)),
                  pl.BlockSpec((1, D_MODEL, tn), lambda l, j: (l, 0, j)),
                  pl.BlockSpec((1, 1, tn), lambda l, j: (l, 0, j))],
        out_specs=pl.BlockSpec((1, MOD_ROWS, tn), lambda l, j: (l, 0, j)),
        out_shape=jax.ShapeDtypeStruct((DEPTH, MOD_ROWS, n), F32),
        compiler_params=pltpu.CompilerParams(dimension_semantics=("arbitrary", "arbitrary"),
                                             vmem_limit_bytes=VMEM_LIMIT),
        name="ada",
    )(cond, w_ada, b_ada.reshape(DEPTH, 1, n))


def _premix_kernel(x_ref, g_ref, sc_ref, sh_ref, w_ref, og_ref, oh_ref, oa_ref):
    h = _rms(x_ref[...], g_ref[...]) * (1.0 + sc_ref[0]) + sh_ref[0]
    hb = h.astype(BF16)
    og_ref[...] = _dot(hb, w_ref[:, 0:G_W])
    oh_ref[...] = _dot(hb, w_ref[:, G_W:G_W + H_W])
    oa_ref[...] = _dot(hb, w_ref[:, G_W + H_W:G_W + H_W + A_W])


def _premix(x, g, sc, sh, w, tm):
    m = x.shape[0]
    tiles = m // tm
    per = tiles // sc.shape[0]
    row = lambda i: (i // per, 0, 0)
    wt = G_W + H_W + A_W
    return pl.pallas_call(
        _premix_kernel,
        grid=(tiles,),
        in_specs=[pl.BlockSpec((tm, D_MODEL), lambda i: (i, 0)),
                  pl.BlockSpec((1, D_MODEL), lambda i: (0, 0)),
                  pl.BlockSpec((1, 1, D_MODEL), row),
                  pl.BlockSpec((1, 1, D_MODEL), row),
                  pl.BlockSpec((D_MODEL, wt), lambda i: (0, 0))],
        out_specs=[pl.BlockSpec((tm, G_W), lambda i: (i, 0)),
                   pl.BlockSpec((tm, H_W), lambda i: (i, 0)),
                   pl.BlockSpec((tm, A_W), lambda i: (i, 0))],
        out_shape=[jax.ShapeDtypeStruct((m, G_W), F32),
                   jax.ShapeDtypeStruct((m, H_W), F32),
                   jax.ShapeDtypeStruct((m, A_W), F32)],
        compiler_params=pltpu.CompilerParams(dimension_semantics=("arbitrary",), vmem_limit_bytes=VMEM_LIMIT),
        name="premix",
    )(x, g, sc, sh, w)


def _keys_from_latent(ckv, kr128, ck, sk, wukv_ref, k_out, v_out):
    kv = _dot(ckv.astype(BF16), wukv_ref[...])
    krot = (kr128 * ck + pltpu.roll(kr128, 64, 1) * sk).astype(BF16)
    for h in range(HEADS):
        k_out[:, MLA_HEAD_PAD * h:MLA_HEAD_PAD * h + MLA_NOPE] = kv[:, MLA_NOPE * h:MLA_NOPE * (h + 1)].astype(BF16)
        k_out[:, MLA_HEAD_PAD * h + MLA_NOPE:MLA_HEAD_PAD * (h + 1)] = krot
    v_out[...] = kv[:, HEADS * MLA_NOPE:].astype(BF16)


def _mla_prep_kernel(a_ref, qnw_ref, kvnw_ref, wuq_ref, wukv_ref, qa_ref, qb_ref, ck_ref, sk_ref,
                     q_out, k_out, v_out, ckv_out):
    qn = _rms(a_ref[:, 0:MLA_Q_RANK], qnw_ref[...])
    y = _dot(qn.astype(BF16), wuq_ref[...])
    z = pltpu.roll(y, HEADS * MLA_HEAD_PAD - MLA_ROPE, 1)
    qa = qa_ref[...]
    qb = qb_ref[...]
    for h in range(HEADS):
        sl = slice(MLA_HEAD_PAD * h, MLA_HEAD_PAD * (h + 1))
        q_out[:, sl] = (y[:, sl] * qa + z[:, sl] * qb).astype(BF16)
    ckv = _rms(a_ref[:, MLA_Q_RANK:A_KR], kvnw_ref[...])
    ckv_out[...] = ckv
    _keys_from_latent(ckv, a_ref[:, A_KR:A_AB], ck_ref[...], sk_ref[...], wukv_ref, k_out, v_out)


def _mla_prep(oa, qnw, kvnw, wuq, wukv, qa, qb, ck, sk, tm, t_len):
    m = oa.shape[0]
    tiles = m // tm
    tps = t_len // tm
    pos = lambda i: (i % tps, 0)
    full = lambda i: (0, 0)
    kw = HEADS * MLA_HEAD_PAD
    return pl.pallas_call(
        _mla_prep_kernel,
        grid=(tiles,),
        in_specs=[pl.BlockSpec((tm, A_W), lambda i: (i, 0)),
                  pl.BlockSpec((1, MLA_Q_RANK), full),
                  pl.BlockSpec((1, MLA_KV_RANK), full),
                  pl.BlockSpec((MLA_Q_RANK, kw), full),
                  pl.BlockSpec((MLA_KV_RANK, HEADS * (MLA_NOPE + MLA_DV)), full),
                  pl.BlockSpec((tm, MLA_HEAD_PAD), pos),
                  pl.BlockSpec((tm, MLA_HEAD_PAD), pos),
                  pl.BlockSpec((tm, 128), pos),
                  pl.BlockSpec((tm, 128), pos)],
        out_specs=[pl.BlockSpec((tm, kw), lambda i: (i, 0)),
                   pl.BlockSpec((tm, kw), lambda i: (i, 0)),
                   pl.BlockSpec((tm, HEADS * MLA_DV), lambda i: (i, 0)),
                   pl.BlockSpec((tm, MLA_KV_RANK), lambda i: (i, 0))],
        out_shape=[jax.ShapeDtypeStruct((m, kw), BF16),
                   jax.ShapeDtypeStruct((m, kw), BF16),
                   jax.ShapeDtypeStruct((m, HEADS * MLA_DV), BF16),
                   jax.ShapeDtypeStruct((m, MLA_KV_RANK), F32)],
        compiler_params=pltpu.CompilerParams(dimension_semantics=("arbitrary",), vmem_limit_bytes=VMEM_LIMIT),
        name="mla_prep",
    )(oa, qnw, kvnw, wuq, wukv, qa, qb, ck, sk)


def _kv_up_kernel(ckv_ref, kr_ref, ck_ref, sk_ref, wukv_ref, k_out, v_out):
    _keys_from_latent(ckv_ref[...], kr_ref[...], ck_ref[...], sk_ref[...], wukv_ref, k_out, v_out)


def _kv_up(ckv, kr128, ck, sk, wukv, tm):
    m = ckv.shape[0]
    kw = HEADS * MLA_HEAD_PAD
    return pl.pallas_call(
        _kv_up_kernel,
        grid=(m // tm,),
        in_specs=[pl.BlockSpec((tm, MLA_KV_RANK), lambda i: (i, 0)),
                  pl.BlockSpec((tm, 128), lambda i: (i, 0)),
                  pl.BlockSpec((tm, 128), lambda i: (0, 0)),
                  pl.BlockSpec((tm, 128), lambda i: (0, 0)),
                  pl.BlockSpec((MLA_KV_RANK, HEADS * (MLA_NOPE + MLA_DV)), lambda i: (0, 0))],
        out_specs=[pl.BlockSpec((tm, kw), lambda i: (i, 0)),
                   pl.BlockSpec((tm, HEADS * MLA_DV), lambda i: (i, 0))],
        out_shape=[jax.ShapeDtypeStruct((m, kw), BF16),
                   jax.ShapeDtypeStruct((m, HEADS * MLA_DV), BF16)],
        compiler_params=pltpu.CompilerParams(dimension_semantics=("arbitrary",), vmem_limit_bytes=VMEM_LIMIT),
        name="kv_up",
    )(ckv, kr128, ck, sk, wukv)


def _attn_kernel(q_ref, k_ref, v_ref, o_ref):
    for h in range(HEADS):
        sl = slice(MLA_HEAD_PAD * h, MLA_HEAD_PAD * (h + 1))
        s = _dot_t(q_ref[0, :, sl], k_ref[0, :, sl])
        p = jnp.exp(s - jnp.max(s, axis=-1, keepdims=True))
        l = jnp.sum(p, axis=-1, keepdims=True)
        o = _dot(p.astype(BF16), v_ref[0, :, MLA_DV * h:MLA_DV * (h + 1)])
        o_ref[0, :, MLA_DV * h:MLA_DV * (h + 1)] = o / l


def _attn(q, k, v, tq):
    b, t, kw = q.shape
    s = k.shape[1]
    return pl.pallas_call(
        _attn_kernel,
        grid=(b, t // tq),
        in_specs=[pl.BlockSpec((1, tq, kw), lambda bi, i: (bi, i, 0)),
                  pl.BlockSpec((1, s, kw), lambda bi, i: (bi, 0, 0)),
                  pl.BlockSpec((1, s, HEADS * MLA_DV), lambda bi, i: (bi, 0, 0))],
        out_specs=pl.BlockSpec((1, tq, HEADS * MLA_DV), lambda bi, i: (bi, i, 0)),
        out_shape=jax.ShapeDtypeStruct((b, t, HEADS * MLA_DV), F32),
        compiler_params=pltpu.CompilerParams(dimension_semantics=("arbitrary", "arbitrary"),
                                             vmem_limit_bytes=VMEM_LIMIT),
        name="attn",
    )(q, k, v)


def _gdn_prep_kernel(tps, cur_ref, prev_ref, next_ref, w_ref, seg_ref, q_out, k_out, vk_out):
    i = pl.program_id(0)
    tm = cur_ref.shape[0]
    has_prev = ((i % tps) != 0).astype(F32)
    has_next = ((i % tps) != (tps - 1)).astype(F32)
    xc = jnp.concatenate([prev_ref[...] * has_prev, cur_ref[...], next_ref[...] * has_next], axis=0)
    n = tm + 16
    y = xc[8:8 + tm] * w_ref[2:3, :]
    for j in (0, 1, 3, 4):
        d = j - 2
        y = y + pltpu.roll(xc, (n - d) % n, 0)[8:8 + tm] * w_ref[j:j + 1, :]
    y = _silu(y)
    seg = seg_ref[...]
    qk = y[:, 0:2 * LIN_W]
    qk = qk * lax.rsqrt(_seg_sum(qk * qk, seg) + EPS)
    q_out[...] = qk[:, 0:LIN_W] * (HD ** -0.5)
    k_out[...] = qk[:, LIN_W:2 * LIN_W]
    vk = y[:, 2 * LIN_W:]
    is_k = (lax.broadcasted_iota(jnp.int32, vk.shape, 1) & HD) != 0
    vk_out[...] = jnp.where(is_k, vk * lax.rsqrt(_seg_sum(vk * vk, seg) + EPS), vk)


def _gdn_prep(og, convw, seg, tm, t_len):
    m = og.shape[0]
    tiles = m // tm
    tps = t_len // tm
    r8 = tm // 8
    last8 = m // 8 - 1
    return pl.pallas_call(
        functools.partial(_gdn_prep_kernel, tps),
        grid=(tiles,),
        in_specs=[pl.BlockSpec((tm, G_CONV), lambda i: (i, 0)),
                  pl.BlockSpec((8, G_CONV), lambda i: (jnp.maximum(i * r8 - 1, 0), 0)),
                  pl.BlockSpec((8, G_CONV), lambda i: (jnp.minimum((i + 1) * r8, last8), 0)),
                  pl.BlockSpec((8, G_CONV), lambda i: (0, 0)),
                  pl.BlockSpec((2 * LIN_W, 2 * LIN_W), lambda i: (0, 0))],
        out_specs=[pl.BlockSpec((tm, LIN_W), lambda i: (i, 0)),
                   pl.BlockSpec((tm, LIN_W), lambda i: (i, 0)),
                   pl.BlockSpec((tm, 2 * LIN_W), lambda i: (i, 0))],
        out_shape=[jax.ShapeDtypeStruct((m, LIN_W), F32),
                   jax.ShapeDtypeStruct((m, LIN_W), F32),
                   jax.ShapeDtypeStruct((m, 2 * LIN_W), F32)],
        compiler_params=pltpu.CompilerParams(dimension_semantics=("arbitrary",), vmem_limit_bytes=VMEM_LIMIT),
        name="gdn_prep",
    )(og, og, og, convw, seg)


def _gdn_direction(d, q_ref, k_ref, vk_ref, kt_ref, ab_ref, abt_ref, gc_ref, gct_ref, s_ref, o_ref):
    c = GDN_CHUNK
    ri = lax.broadcasted_iota(jnp.int32, (c, c), 0)
    ci = lax.broadcasted_iota(jnp.int32, (c, c), 1)
    incl = (ri >= ci) if d == 0 else (ri <= ci)
    strict = (ri > ci) if d == 0 else (ri < ci)
    incl01 = incl.astype(F32).astype(BF16)
    diag_blk = jnp.right_shift(ri, 4) == jnp.right_shift(ci, 4)
    eye = (ri == ci).astype(F32)
    last = c - 1 if d == 0 else 0
    ab = ab_ref[0]
    g_cols = gc_ref[0:1, :] * _softplus(ab + gc_ref[1:2, :])
    beta = _sigmoid(ab)
    g_rows = gct_ref[0:16, :] * _softplus(abt_ref[0] + gct_ref[16:32, :])
    lane_is_v = lax.broadcasted_iota(jnp.int32, (c, 2 * HD), 1) < HD
    n_chunks = SCAN_BLOCK // c
    for cc in (range(n_chunks) if d == 0 else range(n_chunks - 1, -1, -1)):
        rows = slice(cc * c, (cc + 1) * c)
        gcs = _mask_dot(incl01, g_cols[rows])
        gcr = _dot_mask_t(g_rows[:, rows], incl01)
        for h in range(HEADS):
            j = d * HEADS + h
            hs = slice(HD * h, HD * (h + 1))
            gcol = gcs[:, j:j + 1]
            grow = gcr[j:j + 1, :]
            bcol = beta[rows, 8 + j:9 + j]
            gl = gcol[last:last + 1, :]
            q = q_ref[0, rows, hs]
            k = k_ref[0, rows, hs]
            vk = vk_ref[0, rows, 2 * HD * h:2 * HD * (h + 1)]
            kt = kt_ref[0, hs, rows]
            decay = jnp.where(incl, jnp.exp(jnp.where(incl, gcol - grow, 0.0)), 0.0)
            kb16 = k.astype(BF16)
            kk = _dot_t((k * bcol).astype(BF16), kb16)
            qk = _dot_t(q.astype(BF16), kb16)
            p = jnp.where(strict, kk * decay, 0.0)
            att = qk * decay
            eg = jnp.exp(gcol)
            x = vk * jnp.where(lane_is_v, bcol, bcol * eg)
            pd = jnp.where(diag_blk, p, 0.0)
            td = eye - pd
            a = pd.astype(BF16)
            for _ in range(3):
                a = _dot(a, a).astype(BF16)
                td = td + _dot(td.astype(BF16), a)
            tdb = td.astype(BF16)
            y = _dot(tdb, x.astype(BF16))
            nb = _dot(tdb, (p - pd).astype(BF16)).astype(BF16)
            x = y
            for _ in range(3):
                x = y - _dot(nb, x.astype(BF16))
            s = s_ref[0, j]
            sb = s.astype(BF16)
            v_new = x[:, 0:HD] - _dot(x[:, HD:2 * HD].astype(BF16), sb)
            vb = v_new.astype(BF16)
            o_ref[0, rows, hs] = _dot((q * eg).astype(BF16), sb) + _dot(att.astype(BF16), vb)
            k_tail = kt * jnp.exp(gl - grow)
            s_ref[0, j] = s * jnp.exp(gl) + _dot(k_tail.astype(BF16), vb)


def _gdn_scan_kernel(qf, kf, vkf, ktf, abf, abtf, qb, kb, vkb, ktb, abb, abtb, gc_ref, gct_ref, s0_ref,
                     of_ref, ob_ref, s_ref):
    @pl.when(pl.program_id(1) == 0)
    def _():
        s_ref[...] = s0_ref[...]
    _gdn_direction(0, qf, kf, vkf, ktf, abf, abtf, gc_ref, gct_ref, s_ref, of_ref)
    _gdn_direction(1, qb, kb, vkb, ktb, abb, abtb, gc_ref, gct_ref, s_ref, ob_ref)


def _gdn_scan(q, k, vk, kt, ab, abt, gconst, gconst_t, s0):
    b, t, _ = q.shape
    nb = t // SCAN_BLOCK
    fwd = lambda bi, i: (bi, i, 0)
    bwd = lambda bi, i: (bi, nb - 1 - i, 0)
    fwd_t = lambda bi, i: (bi, 0, i)
    bwd_t = lambda bi, i: (bi, 0, nb - 1 - i)

    def specs(im, im_t):
        return [pl.BlockSpec((1, SCAN_BLOCK, LIN_W), im),
                pl.BlockSpec((1, SCAN_BLOCK, LIN_W), im),
                pl.BlockSpec((1, SCAN_BLOCK, 2 * LIN_W), im),
                pl.BlockSpec((1, LIN_W, SCAN_BLOCK), im_t),
                pl.BlockSpec((1, SCAN_BLOCK, 128), im),
                pl.BlockSpec((1, 16, SCAN_BLOCK), im_t)]

    state = pl.BlockSpec((1, 2 * HEADS, HD, HD), lambda bi, i: (bi, 0, 0, 0))
    return pl.pallas_call(
        _gdn_scan_kernel,
        grid=(b, nb),
        in_specs=specs(fwd, fwd_t) + specs(bwd, bwd_t)
        + [pl.BlockSpec((8, 128), lambda bi, i: (0, 0)),
           pl.BlockSpec((32, SCAN_BLOCK), lambda bi, i: (0, 0)),
           state],
        out_specs=[pl.BlockSpec((1, SCAN_BLOCK, LIN_W), fwd),
                   pl.BlockSpec((1, SCAN_BLOCK, LIN_W), bwd),
                   state],
        out_shape=[jax.ShapeDtypeStruct((b, t, LIN_W), F32),
                   jax.ShapeDtypeStruct((b, t, LIN_W), F32),
                   jax.ShapeDtypeStruct((b, 2 * HEADS, HD, HD), F32)],
        compiler_params=pltpu.CompilerParams(dimension_semantics=("arbitrary", "arbitrary"),
                                             vmem_limit_bytes=VMEM_LIMIT),
        name="gdn_scan",
    )(q, k, vk, kt, ab, abt, q, k, vk, kt, ab, abt, gconst, gconst_t, s0)


def _hgrn_lower_bound(layer, lb_ref):
    raw = lb_ref[0:DEPTH, :]
    e = jnp.exp(raw - jnp.max(raw, axis=0, keepdims=True))
    gamma = e / jnp.sum(e, axis=0, keepdims=True)
    lb = jnp.zeros((1, LIN_W), F32)
    for i in range(1, layer + 1):
        lb = lb + gamma[i:i + 1, :]
    return lb


def _hgrn_direction(d, lb, q_ref, v_ref, f_ref, vt_ref, ones_ref, st_ref, o_ref):
    c = HG_CHUNK
    ri = lax.broadcasted_iota(jnp.int32, (c, c), 0)
    ci = lax.broadcasted_iota(jnp.int32, (c, c), 1)
    incl01 = ((ri >= ci) if d == 0 else (ri <= ci)).astype(F32).astype(BF16)
    row_id = lax.broadcasted_iota(jnp.int32, (c, LIN_W), 0)
    last = c - 1 if d == 0 else 0
    oml = 1.0 - lb
    ones = ones_ref[...]
    seg_mask = ones.astype(F32)
    n_chunks = SCAN_BLOCK // c
    for cc in (range(n_chunks) if d == 0 else range(n_chunks - 1, -1, -1)):
        rows = slice(cc * c, (cc + 1) * c)
        q = q_ref[0, rows, :]
        v = v_ref[0, rows, :]
        f = f_ref[0, rows, :]
        logf = jnp.log(jnp.maximum(lb + oml * _sigmoid(f), GATE_FLOOR))
        kk = oml * _sigmoid(-f)
        bc = _mask_dot(incl01, logf)
        bl = bc[last:last + 1, :]
        parts = []
        for s in range(c):
            e = jnp.exp(bc - bc[s:s + 1, :])
            keep = (row_id >= s) if d == 0 else (row_id <= s)
            parts.append(jnp.where(keep, q * e * kk[s:s + 1, :], 0.0).astype(BF16))
        r = _dot(jnp.concatenate(parts, axis=0), ones)
        o = _dot_t((q * jnp.exp(bc)).astype(BF16), st_ref[0, d].astype(BF16))
        for s in range(c):
            o = o + r[s * c:(s + 1) * c, :] * v[s:s + 1, :]
        o_ref[0, rows, :] = o
        k_tail = (kk * jnp.exp(bl - bc)).astype(BF16)
        upd = _dot(vt_ref[0, :, rows].astype(BF16), k_tail)
        st_ref[0, d] = st_ref[0, d] * jnp.exp(bl) + upd * seg_mask


def _hgrn_scan_kernel(layer, qf, vf, ff, vtf, qb, vb, fb, vtb, lb_ref, ones_ref, s0_ref, of_ref, ob_ref, st_ref):
    @pl.when(pl.program_id(1) == 0)
    def _():
        st_ref[...] = s0_ref[...]
    lb = _hgrn_lower_bound(layer, lb_ref)
    _hgrn_direction(0, lb, qf, vf, ff, vtf, ones_ref, st_ref, of_ref)
    _hgrn_direction(1, lb, qb, vb, fb, vtb, ones_ref, st_ref, ob_ref)


def _hgrn_scan(layer, oh, vt, lb, ones, s0):
    b, t, _ = oh.shape
    nb = t // SCAN_BLOCK

    def col(cb, rev):
        return (lambda bi, i: (bi, nb - 1 - i, cb)) if rev else (lambda bi, i: (bi, i, cb))

    blk = lambda im: pl.BlockSpec((1, SCAN_BLOCK, LIN_W), im)
    state = pl.BlockSpec((1, 2, LIN_W, LIN_W), lambda bi, i: (bi, 0, 0, 0))
    return pl.pallas_call(
        functools.partial(_hgrn_scan_kernel, layer),
        grid=(b, nb),
        in_specs=[blk(col(0, False)), blk(col(1, False)), blk(col(2, False)),
                  pl.BlockSpec((1, LIN_W, SCAN_BLOCK), lambda bi, i: (bi, 0, i)),
                  blk(col(0, True)), blk(col(1, True)), blk(col(3, True)),
                  pl.BlockSpec((1, LIN_W, SCAN_BLOCK), lambda bi, i: (bi, 0, nb - 1 - i)),
                  pl.BlockSpec((8, LIN_W), lambda bi, i: (0, 0)),
                  pl.BlockSpec((LIN_W, LIN_W), lambda bi, i: (0, 0)),
                  state],
        out_specs=[blk(col(0, False)), blk(col(0, True)), state],
        out_shape=[jax.ShapeDtypeStruct((b, t, LIN_W), F32),
                   jax.ShapeDtypeStruct((b, t, LIN_W), F32),
                   jax.ShapeDtypeStruct((b, 2, LIN_W, LIN_W), F32)],
        compiler_params=pltpu.CompilerParams(dimension_semantics=("arbitrary", "arbitrary"),
                                             vmem_limit_bytes=VMEM_LIMIT),
        name="hgrn_scan",
    )(oh, oh, oh, vt, oh, oh, oh, vt, lb, ones, s0)


def _post_kernel(x_ref, ogf_ref, ogb_ref, z_ref, gnw_ref, ohf_ref, ohb_ref, hg_ref, hnw_ref, om_ref, seg_ref,
                 wout_ref, gpm_ref, gt1_ref, gpf_ref, sc2_ref, sh2_ref, gt2_ref, gff_ref,
                 wa_ref, wb_ref, wo_ref, out_ref):
    seg = seg_ref[...]
    inv = 1.0 / HD
    og = ogf_ref[...] + ogb_ref[...]
    og = og * lax.rsqrt(_seg_sum(og * og, seg) * inv + EPS) * gnw_ref[...] * _silu(z_ref[...])
    oh = ohf_ref[...] + ohb_ref[...]
    oh = oh * lax.rsqrt(_seg_sum(oh * oh, seg) * inv + EPS) * hnw_ref[...] * _sigmoid(hg_ref[...])
    mix = jnp.concatenate([og.astype(BF16), oh.astype(BF16), om_ref[...].astype(BF16)], axis=-1)
    x1 = x_ref[...] + gt1_ref[0] * _rms(_dot(mix, wout_ref[...]), gpm_ref[...])
    hb = (_rms(x1, gpf_ref[...]) * (1.0 + sc2_ref[0]) + sh2_ref[0]).astype(BF16)
    fw = D_FF // FF_SPLIT
    y = None
    for part in range(FF_SPLIT):
        cs = slice(part * fw, (part + 1) * fw)
        act = (_silu(_dot(hb, wa_ref[:, cs])) * _dot(hb, wb_ref[:, cs])).astype(BF16)
        contrib = _dot(act, wo_ref[cs, :])
        y = contrib if y is None else y + contrib
    out_ref[...] = x1 + gt2_ref[0] * _rms(y, gff_ref[...])


def _post(x, ogf, ogb, og, gnw, ohf, ohb, oh, hnw, om, seg, wout, gpm, gt1, gpf, sc2, sh2, gt2, gff, wa, wb, wo, tm):
    m = x.shape[0]
    tiles = m // tm
    per = tiles // gt1.shape[0]
    row = lambda i: (i // per, 0, 0)
    tok = lambda w: pl.BlockSpec((tm, w), lambda i: (i, 0))
    vec = lambda w: pl.BlockSpec((1, w), lambda i: (0, 0))
    mod = pl.BlockSpec((1, 1, D_MODEL), row)
    once = lambda r, c: pl.BlockSpec((r, c), lambda i: (0, 0), pipeline_mode=pl.Buffered(1))
    return pl.pallas_call(
        _post_kernel,
        grid=(tiles,),
        in_specs=[tok(D_MODEL), tok(LIN_W), tok(LIN_W),
                  pl.BlockSpec((tm, LIN_W), lambda i: (i, G_W // LIN_W - 1)), vec(LIN_W),
                  tok(LIN_W), tok(LIN_W),
                  pl.BlockSpec((tm, LIN_W), lambda i: (i, H_W // LIN_W - 1)), vec(LIN_W),
                  tok(HEADS * MLA_DV), once(LIN_W, LIN_W),
                  once(D_MODEL, D_MODEL), vec(D_MODEL), mod, vec(D_MODEL), mod, mod, mod, vec(D_MODEL),
                  once(D_MODEL, D_FF), once(D_MODEL, D_FF), once(D_FF, D_MODEL)],
        out_specs=tok(D_MODEL),
        out_shape=jax.ShapeDtypeStruct((m, D_MODEL), F32),
        compiler_params=pltpu.CompilerParams(dimension_semantics=("arbitrary",), vmem_limit_bytes=VMEM_LIMIT),
        name="post",
    )(x, ogf, ogb, og, gnw, ohf, ohb, oh, hnw, om, seg, wout, gpm, gt1, gpf, sc2, sh2, gt2, gff, wa, wb, wo)


def _rope_tables(t_len, use_rope):
    scale = MLA_QK ** -0.5
    if use_rope:
        rows = t_len // GRID_W
        row = jnp.repeat(jnp.arange(rows, dtype=F32), GRID_W)
        col = jnp.tile(jnp.arange(GRID_W, dtype=F32), rows)
        nf = MLA_ROPE // 4
        inv = ROPE_BASE ** (-jnp.arange(nf, dtype=F32) / nf)
        ar, ac = row[:, None] * inv, col[:, None] * inv
        cos = jnp.concatenate([jnp.cos(ar), jnp.cos(ar), jnp.cos(ac), jnp.cos(ac)], axis=-1)
        sin = jnp.concatenate([-jnp.sin(ar), jnp.sin(ar), -jnp.sin(ac), jnp.sin(ac)], axis=-1)
    else:
        cos = jnp.ones((t_len, MLA_ROPE), F32)
        sin = jnp.zeros((t_len, MLA_ROPE), F32)
    z64 = jnp.zeros((t_len, MLA_ROPE), F32)
    qa = scale * jnp.concatenate([jnp.ones((t_len, MLA_NOPE), F32), cos, z64], axis=-1)
    qb = scale * jnp.concatenate([jnp.zeros((t_len, MLA_NOPE), F32), sin, z64], axis=-1)
    ck = jnp.concatenate([cos, z64], axis=-1)
    sk = jnp.concatenate([sin, z64], axis=-1)
    return qa, qb, ck, sk


def _layer_weights(l, w_in, w_out, gdn_conv_w, gdn_a_log, gdn_dt_bias, gdn_norm_w, lower_bounds, hgrn_norm_w,
                   mla_q_norm_w, mla_w_uq, mla_kv_norm_w, mla_w_ukv, w_ffn_in, w_ffn_out,
                   g_pre_mix, g_post_mix, g_pre_ffn, g_post_ffn):
    wi = w_in[l]
    w_cat = jnp.concatenate([wi[:, _PERM_G], wi[:, _PERM_H], wi[:, _PERM_A],
                             jnp.zeros((D_MODEL, A_W - _PERM_A.shape[0]), F32)], axis=1).astype(BF16)
    convw = jnp.concatenate([gdn_conv_w[l][_PERM_CONV].T, jnp.zeros((3, G_CONV), F32)], axis=0)
    neg_a = -jnp.exp(gdn_a_log[l].astype(F32)).reshape(1, 8)
    dt = gdn_dt_bias[l].astype(F32).reshape(1, 8)
    pad = lambda r: jnp.pad(r, ((0, 0), (0, 128 - r.shape[1])))
    gconst = jnp.concatenate([pad(neg_a), pad(dt), jnp.zeros((6, 128), F32)], axis=0)
    z8 = jnp.zeros((8, SCAN_BLOCK), F32)
    gconst_t = jnp.concatenate([jnp.broadcast_to(neg_a.T, (8, SCAN_BLOCK)), z8,
                                jnp.broadcast_to(dt.T, (8, SCAN_BLOCK)), z8], axis=0)
    vec = lambda a: a.reshape(1, -1).astype(F32)
    return dict(
        w_cat=w_cat, convw=convw, gconst=gconst, gconst_t=gconst_t,
        gnw=vec(jnp.tile(gdn_norm_w[l], HEADS)), hnw=vec(jnp.tile(hgrn_norm_w[l], HEADS)),
        layer=l, lb=lower_bounds,
        qnw=vec(mla_q_norm_w[l]), kvnw=vec(mla_kv_norm_w[l]),
        wuq=mla_w_uq[l][:, _PERM_UQ].astype(BF16), wukv=mla_w_ukv[l][:, _PERM_UKV].astype(BF16),
        wout=w_out[l].astype(BF16), wa=w_ffn_in[l][:, :D_FF].astype(BF16), wb=w_ffn_in[l][:, D_FF:].astype(BF16),
        wo=w_ffn_out[l].astype(BF16),
        g_pre_mix=vec(g_pre_mix[l]), g_post_mix=vec(g_post_mix[l]),
        g_pre_ffn=vec(g_pre_ffn[l]), g_post_ffn=vec(g_post_ffn[l]))


def _trunk_layer(x, mod, w, gdn_s0, hg_s0, tables, ctx_keys, seg512, seg256):
    b, t, _ = x.shape
    m = b * t
    tm = min(TOKEN_TILE, t)
    sh1, sc1, gt1, sh2, sc2, gt2 = mod
    xf = x.reshape(m, D_MODEL)
    og, oh, oa = _premix(xf, w["g_pre_mix"], sc1, sh1, w["w_cat"], tm)

    qa, qb, ck, sk = tables
    qcat, kcat, vcat, ckv = _mla_prep(oa, w["qnw"], w["kvnw"], w["wuq"], w["wukv"], qa, qb, ck, sk, tm, t)
    kw = HEADS * MLA_HEAD_PAD
    qcat = qcat.reshape(b, t, kw)
    kcat = kcat.reshape(b, t, kw)
    vcat = vcat.reshape(b, t, HEADS * MLA_DV)
    if ctx_keys is not None:
        kcat = jnp.concatenate([kcat, ctx_keys[0]], axis=1)
        vcat = jnp.concatenate([vcat, ctx_keys[1]], axis=1)
    o_mla = _attn(qcat, kcat, vcat, min(ATTN_Q_TILE, t)).reshape(m, HEADS * MLA_DV)

    gq, gk, gvk = _gdn_prep(og, w["convw"], seg512, tm, t)
    ab = oa[:, A_AB:A_AB + 128].reshape(b, t, 128)
    abt = jnp.swapaxes(ab[:, :, 0:16], 1, 2)
    gk3 = gk.reshape(b, t, LIN_W)
    ogf, ogb, gdn_state = _gdn_scan(gq.reshape(b, t, LIN_W), gk3, gvk.reshape(b, t, 2 * LIN_W),
                                    jnp.swapaxes(gk3, 1, 2), ab, abt, w["gconst"], w["gconst_t"], gdn_s0)

    oh3 = oh.reshape(b, t, H_W)
    vt = jnp.swapaxes(oh3[:, :, LIN_W:2 * LIN_W], 1, 2)
    ohf, ohb, hg_state = _hgrn_scan(w["layer"], oh3, vt, w["lb"], seg256, hg_s0)

    x_new = _post(xf, ogf.reshape(m, LIN_W), ogb.reshape(m, LIN_W), og, w["gnw"],
                  ohf.reshape(m, LIN_W), ohb.reshape(m, LIN_W), oh, w["hnw"], o_mla, seg256,
                  w["wout"], w["g_post_mix"], gt1, w["g_pre_ffn"], sc2, sh2, gt2, w["g_post_ffn"],
                  w["wa"], w["wb"], w["wo"], tm)
    mkr = oa[:, A_KR:A_KR + MLA_ROPE]
    return x_new.reshape(b, t, D_MODEL), gdn_state, hg_state, ckv.reshape(b, t, MLA_KV_RANK), mkr.reshape(b, t, MLA_ROPE)


def _hg_state_to_block(s):
    st = jnp.swapaxes(s, -1, -2)
    eye = jnp.eye(HEADS, dtype=s.dtype)
    big = st[:, :, :, :, None, :] * eye[None, None, :, None, :, None]
    return big.reshape(s.shape[0], 2, LIN_W, LIN_W)


def _hg_block_to_state(big):
    b = big.shape[0]
    r = big.reshape(b, 2, HEADS, HD, HEADS, HD)
    diag = jnp.stack([r[:, :, h, :, h, :] for h in range(HEADS)], axis=2)
    return jnp.swapaxes(diag, -1, -2)


def kernel(x_prompt, x_sample, cache_mla_ckv, cache_mla_krope, state_gdn, state_hgrn, c, c_ctx, w_ada, b_ada,
           g_pre_mix, g_post_mix, g_pre_ffn, g_post_ffn, w_in, w_out, gdn_conv_w, gdn_a_log, gdn_dt_bias,
           gdn_norm_w, hgrn_lb, hgrn_norm_w, mla_q_norm_w, mla_w_uq, mla_kv_norm_w, mla_w_ukv, w_ffn_in, w_ffn_out):
    bp, tp, _ = x_prompt.shape
    bd, td, _ = x_sample.shape
    past = cache_mla_ckv.shape[2]
    assert 1 + bd <= MOD_ROWS and tp % SCAN_BLOCK == 0 and td % SCAN_BLOCK == 0

    lower_bounds = jnp.pad(hgrn_lb.astype(F32), ((0, 8 - DEPTH), (0, 0)))

    cond = jnp.concatenate([c_ctx[None, :], c, jnp.zeros((MOD_ROWS - 1 - bd, D_MODEL), F32)], axis=0)
    mod_all = _ada(cond, w_ada, b_ada)

    seg512 = _seg_ones(2 * LIN_W)
    seg256 = _seg_ones(LIN_W)
    tab_ctx = _rope_tables(tp, False)
    tab_lat = _rope_tables(td, True)
    ones_k = jnp.concatenate([jnp.ones((past, MLA_ROPE), F32), jnp.zeros((past, MLA_ROPE), F32)], axis=-1)
    zeros_k = jnp.zeros((past, 128), F32)

    weights = [_layer_weights(l, w_in, w_out, gdn_conv_w, gdn_a_log, gdn_dt_bias, gdn_norm_w, lower_bounds,
                              hgrn_norm_w, mla_q_norm_w, mla_w_uq, mla_kv_norm_w, mla_w_ukv, w_ffn_in, w_ffn_out,
                              g_pre_mix, g_post_mix, g_pre_ffn, g_post_ffn) for l in range(DEPTH)]

    def mods(l, lo, n):
        rows = mod_all[l, lo:lo + n].reshape(n, 1, 6 * D_MODEL)
        return tuple(rows[:, :, i * D_MODEL:(i + 1) * D_MODEL] for i in range(6))

    xp = x_prompt
    zero_g = jnp.zeros((bp, 2 * HEADS, HD, HD), F32)
    zero_h = jnp.zeros((bp, 2, LIN_W, LIN_W), F32)
    ckv_l, kr_l, gs_l, hs_l = [], [], [], []
    for l in range(DEPTH):
        xp, gs, hs, ckv, mkr = _trunk_layer(xp, mods(l, 0, 1), weights[l], zero_g, zero_h, tab_ctx, None,
                                            seg512, seg256)
        ckv_l.append(ckv)
        kr_l.append(mkr)
        gs_l.append(gs.reshape(bp, 2, HEADS, HD, HD))
        hs_l.append(_hg_block_to_state(hs))

    xs = x_sample
    for l in range(DEPTH):
        kr128 = jnp.pad(cache_mla_krope[:, l].reshape(bd * past, MLA_ROPE), ((0, 0), (0, 128 - MLA_ROPE)))
        k_ctx, v_ctx = _kv_up(cache_mla_ckv[:, l].reshape(bd * past, MLA_KV_RANK), kr128, ones_k, zeros_k,
                              weights[l]["wukv"], past)
        ctx_keys = (k_ctx.reshape(bd, past, HEADS * MLA_HEAD_PAD), v_ctx.reshape(bd, past, HEADS * MLA_DV))
        xs, _, _, _, _ = _trunk_layer(xs, mods(l, 1, bd), weights[l],
                                      state_gdn[:, l].reshape(bd, 2 * HEADS, HD, HD),
                                      _hg_state_to_block(state_hgrn[:, l]), tab_lat, ctx_keys, seg512, seg256)

    return (xp, xs, jnp.stack(ckv_l, axis=1), jnp.stack(kr_l, axis=1),
            jnp.stack(gs_l, axis=1), jnp.stack(hs_l, axis=1))
```

```python
import functools
import math

import numpy as np
import jax
import jax.numpy as jnp
from jax import lax
from jax.experimental import pallas as pl
from jax.experimental.pallas import tpu as pltpu

F32 = jnp.float32
BF16 = jnp.bfloat16

D_MODEL = 1024
DEPTH = 2
GRID_W = 64
EPS = 1e-6
GATE_FLOOR = 1e-30
HEADS = 4
HD = 64
LIN_W = HEADS * HD
GDN_CHUNK = 64
HG_CHUNK = 16
SCAN_BLOCK = 128
MLA_Q_RANK = 384
MLA_KV_RANK = 256
MLA_NOPE = 128
MLA_ROPE = 64
MLA_DV = 128
MLA_QK = MLA_NOPE + MLA_ROPE
MLA_HEAD_PAD = 256
ROPE_BASE = 10000.0
D_FF = -(-8 * D_MODEL // (3 * 256)) * 256
FF_SPLIT = 2
TOKEN_TILE = 512
ATTN_Q_TILE = 256
MOD_ROWS = 16
VMEM_LIMIT = 56 * 1024 * 1024

_OFF = {}
_o = 0
for _n, _s in (("gq", 256), ("gk", 256), ("gv", 256), ("gz", 256), ("ga", 8), ("gb", 8),
               ("hq", 256), ("hi", 256), ("hf", 512), ("hg", 256),
               ("mcq", MLA_Q_RANK), ("mckv", MLA_KV_RANK), ("mkr", MLA_ROPE)):
    _OFF[_n] = _o
    _o += _s
IN_DIM = _o

G_W = 1280
G_CONV = 1024
H_W = 1280
A_W = 896
A_KR = MLA_Q_RANK + MLA_KV_RANK
A_AB = A_KR + 2 * MLA_ROPE


def _rope_swap_idx():
    j = np.arange(MLA_ROPE)
    return np.where((j % 32) < 16, j + 16, j - 16)


def _in_perm():
    ar = np.arange
    vk = np.concatenate([np.concatenate([_OFF["gv"] + 64 * h + ar(64), _OFF["gk"] + 64 * h + ar(64)])
                         for h in range(HEADS)])
    g = np.concatenate([_OFF["gq"] + ar(256), _OFF["gk"] + ar(256), vk, _OFF["gz"] + ar(256)])
    hh = np.concatenate([_OFF["hq"] + ar(256), _OFF["hi"] + ar(256), _OFF["hf"] + ar(512), _OFF["hg"] + ar(256)])
    a = np.concatenate([_OFF["mcq"] + ar(MLA_Q_RANK), _OFF["mckv"] + ar(MLA_KV_RANK), _OFF["mkr"] + ar(MLA_ROPE),
                        _OFF["mkr"] + _rope_swap_idx(), _OFF["ga"] + ar(8), _OFF["gb"] + ar(8)])
    conv = np.concatenate([ar(256), 256 + ar(256),
                           np.concatenate([np.concatenate([512 + 64 * h + ar(64), 256 + 64 * h + ar(64)])
                                           for h in range(HEADS)])])
    return g, hh, a, conv


_PERM_G, _PERM_H, _PERM_A, _PERM_CONV = _in_perm()


def _uq_perm():
    sw = _rope_swap_idx()
    cols = []
    for h in range(HEADS):
        base = h * MLA_QK
        cols += [base + np.arange(MLA_NOPE), base + MLA_NOPE + np.arange(MLA_ROPE), base + MLA_NOPE + sw]
    return np.concatenate(cols)


def _ukv_perm():
    per = MLA_NOPE + MLA_DV
    kn = np.concatenate([h * per + np.arange(MLA_NOPE) for h in range(HEADS)])
    vv = np.concatenate([h * per + MLA_NOPE + np.arange(MLA_DV) for h in range(HEADS)])
    return np.concatenate([kn, vv])


_PERM_UQ = _uq_perm()
_PERM_UKV = _ukv_perm()


def _seg_ones(n):
    i = np.arange(n) // HD
    return jnp.asarray((i[:, None] == i[None, :]).astype(np.float32), dtype=BF16)


def _rms(x, w):
    return x * lax.rsqrt(jnp.mean(x * x, axis=-1, keepdims=True) + EPS) * w


def _dot(a, b):
    return jnp.dot(a, b, preferred_element_type=F32)


def _dot_t(a, b):
    return lax.dot_general(a, b, (((1,), (1,)), ((), ())), preferred_element_type=F32)


def _split3(x):
    hi = x.astype(BF16)
    r1 = x - hi.astype(F32)
    mid = r1.astype(BF16)
    lo = (r1 - mid.astype(F32)).astype(BF16)
    return hi, mid, lo


def _split2(x):
    hi = x.astype(BF16)
    return hi, (x - hi.astype(F32)).astype(BF16)


def _seg_sum(x, seg):
    hi, lo = _split2(x)
    return _dot(hi, seg) + _dot(lo, seg)


def _mask_dot(mask01, x):
    hi, mid, lo = _split3(x)
    return _dot(mask01, hi) + _dot(mask01, mid) + _dot(mask01, lo)


def _dot_mask_t(x, mask01):
    hi, mid, lo = _split3(x)
    return _dot_t(hi, mask01) + _dot_t(mid, mask01) + _dot_t(lo, mask01)


def _softplus(x):
    return jnp.maximum(x, 0.0) + jnp.log1p(jnp.exp(-jnp.abs(x)))


def _sigmoid(x):
    return jax.nn.sigmoid(x)


def _silu(x):
    return x * jax.nn.sigmoid(x)


def _ada_kernel(c_ref, w_ref, b_ref, o_ref):
    s = _silu(c_ref[...]).astype(BF16)
    o_ref[0] = _dot(s, w_ref[0].astype(BF16)) + b_ref[0]


def _ada(cond, w_ada, b_ada):
    n = w_ada.shape[-1]
    tn = 1536
    return pl.pallas_call(
        _ada_kernel,
        grid=(DEPTH, n // tn),
        in_specs=[pl.BlockSpec((MOD_ROWS, D_MODEL), lambda l, j: (0, 0)),
                  pl.BlockSpec((1, D_MODEL, tn), lambda l, j: (l, 0, j)),
                  pl.BlockSpec((1, 1, tn), lambda l, j: (l, 0, j))],
        out_specs=pl.BlockSpec((1, MOD_ROWS, tn), lambda l, j: (l, 0, j)),
        out_shape=jax.ShapeDtypeStruct((DEPTH, MOD_ROWS, n), F32),
        compiler_params=pltpu.CompilerParams(dimension_semantics=("arbitrary", "arbitrary"),
                                             vmem_limit_bytes=VMEM_LIMIT),
        name="ada",
    )(cond, w_ada, b_ada.reshape(DEPTH, 1, n))


def _premix_kernel(x_ref, g_ref, sc_ref, sh_ref, w_ref, og_ref, oh_ref, oa_ref):
    h = _rms(x_ref[...], g_ref[...]) * (1.0 + sc_ref[0]) + sh_ref[0]
    hb = h.astype(BF16)
    og_ref[...] = _dot(hb, w_ref[:, 0:G_W])
    oh_ref[...] = _dot(hb, w_ref[:, G_W:G_W + H_W])
    oa_ref[...] = _dot(hb, w_ref[:, G_W + H_W:G_W + H_W + A_W])


def _premix(x, g, sc, sh, w, tm):
    m = x.shape[0]
    tiles = m // tm
    per = tiles // sc.shape[0]
    row = lambda i: (i // per, 0, 0)
    wt = G_W + H_W + A_W
    return pl.pallas_call(
        _premix_kernel,
        grid=(tiles,),
        in_specs=[pl.BlockSpec((tm, D_MODEL), lambda i: (i, 0)),
                  pl.BlockSpec((1, D_MODEL), lambda i: (0, 0)),
                  pl.BlockSpec((1, 1, D_MODEL), row),
                  pl.BlockSpec((1, 1, D_MODEL), row),
                  pl.BlockSpec((D_MODEL, wt), lambda i: (0, 0))],
        out_specs=[pl.BlockSpec((tm, G_W), lambda i: (i, 0)),
                   pl.BlockSpec((tm, H_W), lambda i: (i, 0)),
                   pl.BlockSpec((tm, A_W), lambda i: (i, 0))],
        out_shape=[jax.ShapeDtypeStruct((m, G_W), F32),
                   jax.ShapeDtypeStruct((m, H_W), F32),
                   jax.ShapeDtypeStruct((m, A_W), F32)],
        compiler_params=pltpu.CompilerParams(dimension_semantics=("arbitrary",), vmem_limit_bytes=VMEM_LIMIT),
        name="premix",
    )(x, g, sc, sh, w)


def _keys_from_latent(ckv, kr128, ck, sk, wukv_ref, k_out, v_out):
    kv = _dot(ckv.astype(BF16), wukv_ref[...])
    krot = (kr128 * ck + pltpu.roll(kr128, 64, 1) * sk).astype(BF16)
    for h in range(HEADS):
        k_out[:, MLA_HEAD_PAD * h:MLA_HEAD_PAD * h + MLA_NOPE] = kv[:, MLA_NOPE * h:MLA_NOPE * (h + 1)].astype(BF16)
        k_out[:, MLA_HEAD_PAD * h + MLA_NOPE:MLA_HEAD_PAD * (h + 1)] = krot
    v_out[...] = kv[:, HEADS * MLA_NOPE:].astype(BF16)


def _mla_prep_kernel(a_ref, qnw_ref, kvnw_ref, wuq_ref, wukv_ref, qa_ref, qb_ref, ck_ref, sk_ref,
                     q_out, k_out, v_out, ckv_out):
    qn = _rms(a_ref[:, 0:MLA_Q_RANK], qnw_ref[...])
    y = _dot(qn.astype(BF16), wuq_ref[...])
    z = pltpu.roll(y, HEADS * MLA_HEAD_PAD - MLA_ROPE, 1)
    qa = qa_ref[...]
    qb = qb_ref[...]
    for h in range(HEADS):
        sl = slice(MLA_HEAD_PAD * h, MLA_HEAD_PAD * (h + 1))
        q_out[:, sl] = (y[:, sl] * qa + z[:, sl] * qb).astype(BF16)
    ckv = _rms(a_ref[:, MLA_Q_RANK:A_KR], kvnw_ref[...])
    ckv_out[...] = ckv
    _keys_from_latent(ckv, a_ref[:, A_KR:A_AB], ck_ref[...], sk_ref[...], wukv_ref, k_out, v_out)


def _mla_prep(oa, qnw, kvnw, wuq, wukv, qa, qb, ck, sk, tm, t_len):
    m = oa.shape[0]
    tiles = m // tm
    tps = t_len // tm
    pos = lambda i: (i % tps, 0)
    full = lambda i: (0, 0)
    kw = HEADS * MLA_HEAD_PAD
    return pl.pallas_call(
        _mla_prep_kernel,
        grid=(tiles,),
        in_specs=[pl.BlockSpec((tm, A_W), lambda i: (i, 0)),
                  pl.BlockSpec((1, MLA_Q_RANK), full),
                  pl.BlockSpec((1, MLA_KV_RANK), full),
                  pl.BlockSpec((MLA_Q_RANK, kw), full),
                  pl.BlockSpec((MLA_KV_RANK, HEADS * (MLA_NOPE + MLA_DV)), full),
                  pl.BlockSpec((tm, MLA_HEAD_PAD), pos),
                  pl.BlockSpec((tm, MLA_HEAD_PAD), pos),
                  pl.BlockSpec((tm, 128), pos),
                  pl.BlockSpec((tm, 128), pos)],
        out_specs=[pl.BlockSpec((tm, kw), lambda i: (i, 0)),
                   pl.BlockSpec((tm, kw), lambda i: (i, 0)),
                   pl.BlockSpec((tm, HEADS * MLA_DV), lambda i: (i, 0)),
                   pl.BlockSpec((tm, MLA_KV_RANK), lambda i: (i, 0))],
        out_shape=[jax.ShapeDtypeStruct((m, kw), BF16),
                   jax.ShapeDtypeStruct((m, kw), BF16),
                   jax.ShapeDtypeStruct((m, HEADS * MLA_DV), BF16),
                   jax.ShapeDtypeStruct((m, MLA_KV_RANK), F32)],
        compiler_params=pltpu.CompilerParams(dimension_semantics=("arbitrary",), vmem_limit_bytes=VMEM_LIMIT),
        name="mla_prep",
    )(oa, qnw, kvnw, wuq, wukv, qa, qb, ck, sk)


def _kv_up_kernel(ckv_ref, kr_ref, ck_ref, sk_ref, wukv_ref, k_out, v_out):
    _keys_from_latent(ckv_ref[...], kr_ref[...], ck_ref[...], sk_ref[...], wukv_ref, k_out, v_out)


def _kv_up(ckv, kr128, ck, sk, wukv, tm):
    m = ckv.shape[0]
    kw = HEADS * MLA_HEAD_PAD
    return pl.pallas_call(
        _kv_up_kernel,
        grid=(m // tm,),
        in_specs=[pl.BlockSpec((tm, MLA_KV_RANK), lambda i: (i, 0)),
                  pl.BlockSpec((tm, 128), lambda i: (i, 0)),
                  pl.BlockSpec((tm, 128), lambda i: (0, 0)),
                  pl.BlockSpec((tm, 128), lambda i: (0, 0)),
                  pl.BlockSpec((MLA_KV_RANK, HEADS * (MLA_NOPE + MLA_DV)), lambda i: (0, 0))],
        out_specs=[pl.BlockSpec((tm, kw), lambda i: (i, 0)),
                   pl.BlockSpec((tm, HEADS * MLA_DV), lambda i: (i, 0))],
        out_shape=[jax.ShapeDtypeStruct((m, kw), BF16),
                   jax.ShapeDtypeStruct((m, HEADS * MLA_DV), BF16)],
        compiler_params=pltpu.CompilerParams(dimension_semantics=("arbitrary",), vmem_limit_bytes=VMEM_LIMIT),
        name="kv_up",
    )(ckv, kr128, ck, sk, wukv)


def _attn_kernel(q_ref, k_ref, v_ref, o_ref):
    for h in range(HEADS):
        sl = slice(MLA_HEAD_PAD * h, MLA_HEAD_PAD * (h + 1))
        s = _dot_t(q_ref[0, :, sl], k_ref[0, :, sl])
        p = jnp.exp(s - jnp.max(s, axis=-1, keepdims=True))
        l = jnp.sum(p, axis=-1, keepdims=True)
        o = _dot(p.astype(BF16), v_ref[0, :, MLA_DV * h:MLA_DV * (h + 1)])
        o_ref[0, :, MLA_DV * h:MLA_DV * (h + 1)] = o / l


def _attn(q, k, v, tq):
    b, t, kw = q.shape
    s = k.shape[1]
    return pl.pallas_call(
        _attn_kernel,
        grid=(b, t // tq),
        in_specs=[pl.BlockSpec((1, tq, kw), lambda bi, i: (bi, i, 0)),
                  pl.BlockSpec((1, s, kw), lambda bi, i: (bi, 0, 0)),
                  pl.BlockSpec((1, s, HEADS * MLA_DV), lambda bi, i: (bi, 0, 0))],
        out_specs=pl.BlockSpec((1, tq, HEADS * MLA_DV), lambda bi, i: (bi, i, 0)),
        out_shape=jax.ShapeDtypeStruct((b, t, HEADS * MLA_DV), F32),
        compiler_params=pltpu.CompilerParams(dimension_semantics=("arbitrary", "arbitrary"),
                                             vmem_limit_bytes=VMEM_LIMIT),
        name="attn",
    )(q, k, v)


def _gdn_prep_kernel(tps, cur_ref, prev_ref, next_ref, w_ref, seg_ref, q_out, k_out, vk_out):
    i = pl.program_id(0)
    tm = cur_ref.shape[0]
    has_prev = ((i % tps) != 0).astype(F32)
    has_next = ((i % tps) != (tps - 1)).astype(F32)
    xc = jnp.concatenate([prev_ref[...] * has_prev, cur_ref[...], next_ref[...] * has_next], axis=0)
    n = tm + 16
    y = xc[8:8 + tm] * w_ref[2:3, :]
    for j in (0, 1, 3, 4):
        d = j - 2
        y = y + pltpu.roll(xc, (n - d) % n, 0)[8:8 + tm] * w_ref[j:j + 1, :]
    y = _silu(y)
    seg = seg_ref[...]
    qk = y[:, 0:2 * LIN_W]
    qk = qk * lax.rsqrt(_seg_sum(qk * qk, seg) + EPS)
    q_out[...] = qk[:, 0:LIN_W] * (HD ** -0.5)
    k_out[...] = qk[:, LIN_W:2 * LIN_W]
    vk = y[:, 2 * LIN_W:]
    is_k = (lax.broadcasted_iota(jnp.int32, vk.shape, 1) & HD) != 0
    vk_out[...] = jnp.where(is_k, vk * lax.rsqrt(_seg_sum(vk * vk, seg) + EPS), vk)


def _gdn_prep(og, convw, seg, tm, t_len):
    m = og.shape[0]
    tiles = m // tm
    tps = t_len // tm
    r8 = tm // 8
    last8 = m // 8 - 1
    return pl.pallas_call(
        functools.partial(_gdn_prep_kernel, tps),
        grid=(tiles,),
        in_specs=[pl.BlockSpec((tm, G_CONV), lambda i: (i, 0)),
                  pl.BlockSpec((8, G_CONV), lambda i: (jnp.maximum(i * r8 - 1, 0), 0)),
                  pl.BlockSpec((8, G_CONV), lambda i: (jnp.minimum((i + 1) * r8, last8), 0)),
                  pl.BlockSpec((8, G_CONV), lambda i: (0, 0)),
                  pl.BlockSpec((2 * LIN_W, 2 * LIN_W), lambda i: (0, 0))],
        out_specs=[pl.BlockSpec((tm, LIN_W), lambda i: (i, 0)),
                   pl.BlockSpec((tm, LIN_W), lambda i: (i, 0)),
                   pl.BlockSpec((tm, 2 * LIN_W), lambda i: (i, 0))],
        out_shape=[jax.ShapeDtypeStruct((m, LIN_W), F32),
                   jax.ShapeDtypeStruct((m, LIN_W), F32),
                   jax.ShapeDtypeStruct((m, 2 * LIN_W), F32)],
        compiler_params=pltpu.CompilerParams(dimension_semantics=("arbitrary",), vmem_limit_bytes=VMEM_LIMIT),
        name="gdn_prep",
    )(og, og, og, convw, seg)


def _gdn_units(d, q_ref, k_ref, vk_ref, kt_ref, ab_ref, abt_ref, gc_ref, gct_ref, o_ref):
    c = GDN_CHUNK
    ri = lax.broadcasted_iota(jnp.int32, (c, c), 0)
    ci = lax.broadcasted_iota(jnp.int32, (c, c), 1)
    incl = (ri >= ci) if d == 0 else (ri <= ci)
    strict = (ri > ci) if d == 0 else (ri < ci)
    incl01 = incl.astype(F32).astype(BF16)
    last = c - 1 if d == 0 else 0
    ab = ab_ref[0]
    g_cols = gc_ref[0:1, :] * _softplus(ab + gc_ref[1:2, :])
    beta = _sigmoid(ab)
    g_rows = gct_ref[0:16, :] * _softplus(abt_ref[0] + gct_ref[16:32, :])
    lane_is_v = lax.broadcasted_iota(jnp.int32, (c, 2 * HD), 1) < HD
    n_chunks = SCAN_BLOCK // c
    units = []
    for rank, cc in enumerate(range(n_chunks) if d == 0 else range(n_chunks - 1, -1, -1)):
        rows = slice(cc * c, (cc + 1) * c)
        gcs = _mask_dot(incl01, g_cols[rows])
        gcr = _dot_mask_t(g_rows[:, rows], incl01)
        for h in range(HEADS):
            j = d * HEADS + h
            hs = slice(HD * h, HD * (h + 1))
            gcol = gcs[:, j:j + 1]
            grow = gcr[j:j + 1, :]
            bcol = beta[rows, 8 + j:9 + j]
            gl = gcol[last:last + 1, :]
            eg = jnp.exp(gcol)
            vk = vk_ref[0, rows, 2 * HD * h:2 * HD * (h + 1)]
            units.append(dict(
                rank=rank, j=j, rows=rows, hs=hs, o_ref=o_ref, strict=strict, gl=gl, bcol=bcol, eg=eg,
                q=q_ref[0, rows, hs], k=k_ref[0, rows, hs],
                k_tail=(kt_ref[0, hs, rows] * jnp.exp(gl - grow)).astype(BF16),
                decay=jnp.where(incl, jnp.exp(jnp.where(incl, gcol - grow, 0.0)), 0.0),
                x=vk * jnp.where(lane_is_v, bcol, bcol * eg)))
    return units


def _gdn_scan_kernel(qf, kf, vkf, ktf, abf, abtf, qb, kb, vkb, ktb, abb, abtb, gc_ref, gct_ref, s0_ref,
                     of_ref, ob_ref, s_ref):
    @pl.when(pl.program_id(1) == 0)
    def _():
        s_ref[...] = s0_ref[...]

    c = GDN_CHUNK
    ri = lax.broadcasted_iota(jnp.int32, (c, c), 0)
    ci = lax.broadcasted_iota(jnp.int32, (c, c), 1)
    diag_blk = jnp.right_shift(ri, 4) == jnp.right_shift(ci, 4)
    eye = (ri == ci).astype(F32)
    units = (_gdn_units(0, qf, kf, vkf, ktf, abf, abtf, gc_ref, gct_ref, of_ref)
             + _gdn_units(1, qb, kb, vkb, ktb, abb, abtb, gc_ref, gct_ref, ob_ref))
    for u in units:
        kb16 = u["k"].astype(BF16)
        u["kk"] = _dot_t((u["k"] * u["bcol"]).astype(BF16), kb16)
        u["qk"] = _dot_t(u["q"].astype(BF16), kb16)
    for u in units:
        p = jnp.where(u["strict"], u["kk"] * u["decay"], 0.0)
        pd = jnp.where(diag_blk, p, 0.0)
        u["att"] = (u["qk"] * u["decay"]).astype(BF16)
        u["poff"] = (p - pd).astype(BF16)
        u["td"] = eye - pd
        u["a"] = pd.astype(BF16)
    for _ in range(3):
        for u in units:
            u["a"] = _dot(u["a"], u["a"]).astype(BF16)
        for u in units:
            u["td"] = u["td"] + _dot(u["td"].astype(BF16), u["a"])
    for u in units:
        tdb = u["td"].astype(BF16)
        u["y"] = _dot(tdb, u["x"].astype(BF16))
        u["n"] = _dot(tdb, u["poff"])
    for u in units:
        u["n"] = u["n"].astype(BF16)
        u["x"] = u["y"]
    for _ in range(3):
        for u in units:
            u["x"] = u["y"] - _dot(u["n"], u["x"].astype(BF16))
    for rank in range(SCAN_BLOCK // c):
        ranked = [u for u in units if u["rank"] == rank]
        for u in ranked:
            u["s"] = s_ref[0, u["j"]]
            u["sb"] = u["s"].astype(BF16)
            u["vb"] = (u["x"][:, 0:HD] - _dot(u["x"][:, HD:2 * HD].astype(BF16), u["sb"])).astype(BF16)
        for u in ranked:
            u["o_ref"][0, u["rows"], u["hs"]] = (_dot((u["q"] * u["eg"]).astype(BF16), u["sb"])
                                                 + _dot(u["att"], u["vb"]))
            s_ref[0, u["j"]] = u["s"] * jnp.exp(u["gl"]) + _dot(u["k_tail"], u["vb"])


def _gdn_scan(q, k, vk, kt, ab, abt, gconst, gconst_t, s0):
    b, t, _ = q.shape
    nb = t // SCAN_BLOCK
    fwd = lambda bi, i: (bi, i, 0)
    bwd = lambda bi, i: (bi, nb - 1 - i, 0)
    fwd_t = lambda bi, i: (bi, 0, i)
    bwd_t = lambda bi, i: (bi, 0, nb - 1 - i)

    def specs(im, im_t):
        return [pl.BlockSpec((1, SCAN_BLOCK, LIN_W), im),
                pl.BlockSpec((1, SCAN_BLOCK, LIN_W), im),
                pl.BlockSpec((1, SCAN_BLOCK, 2 * LIN_W), im),
                pl.BlockSpec((1, LIN_W, SCAN_BLOCK), im_t),
                pl.BlockSpec((1, SCAN_BLOCK, 128), im),
                pl.BlockSpec((1, 16, SCAN_BLOCK), im_t)]

    state = pl.BlockSpec((1, 2 * HEADS, HD, HD), lambda bi, i: (bi, 0, 0, 0))
    return pl.pallas_call(
        _gdn_scan_kernel,
        grid=(b, nb),
        in_specs=specs(fwd, fwd_t) + specs(bwd, bwd_t)
        + [pl.BlockSpec((8, 128), lambda bi, i: (0, 0)),
           pl.BlockSpec((32, SCAN_BLOCK), lambda bi, i: (0, 0)),
           state],
        out_specs=[pl.BlockSpec((1, SCAN_BLOCK, LIN_W), fwd),
                   pl.BlockSpec((1, SCAN_BLOCK, LIN_W), bwd),
                   state],
        out_shape=[jax.ShapeDtypeStruct((b, t, LIN_W), F32),
                   jax.ShapeDtypeStruct((b, t, LIN_W), F32),
                   jax.ShapeDtypeStruct((b, 2 * HEADS, HD, HD), F32)],
        compiler_params=pltpu.CompilerParams(dimension_semantics=("arbitrary", "arbitrary"),
                                             vmem_limit_bytes=VMEM_LIMIT),
        name="gdn_scan",
    )(q, k, vk, kt, ab, abt, q, k, vk, kt, ab, abt, gconst, gconst_t, s0)


def _hgrn_lower_bound(layer, lb_ref):
    raw = lb_ref[0:DEPTH, :]
    e = jnp.exp(raw - jnp.max(raw, axis=0, keepdims=True))
    gamma = e / jnp.sum(e, axis=0, keepdims=True)
    lb = jnp.zeros((1, LIN_W), F32)
    for i in range(1, layer + 1):
        lb = lb + gamma[i:i + 1, :]
    return lb


def _hgrn_block(d, lb, q_ref, v_ref, f_ref):
    c = HG_CHUNK
    ri = lax.broadcasted_iota(jnp.int32, (SCAN_BLOCK, SCAN_BLOCK), 0)
    ci = lax.broadcasted_iota(jnp.int32, (SCAN_BLOCK, SCAN_BLOCK), 1)
    same_chunk = jnp.right_shift(ri, 4) == jnp.right_shift(ci, 4)
    incl01 = (same_chunk & ((ri >= ci) if d == 0 else (ri <= ci))).astype(F32).astype(BF16)
    oml = 1.0 - lb
    q = q_ref[0]
    f = f_ref[0]
    logf = jnp.log(jnp.maximum(lb + oml * _sigmoid(f), GATE_FLOOR))
    bc = _mask_dot(incl01, logf)
    return dict(d=d, q=q, v=v_ref[0], kk=oml * _sigmoid(-f), bc=bc, qe=(q * jnp.exp(bc)).astype(BF16),
                last=c - 1 if d == 0 else 0)


def _hgrn_chunk(blk, cc, vt_ref, ones, seg_mask, st, o_ref):
    c = HG_CHUNK
    half = c // 2
    d = blk["d"]
    rows = slice(cc * c, (cc + 1) * c)
    q, v, kk, bc = blk["q"][rows], blk["v"][rows], blk["kk"][rows], blk["bc"][rows]
    bl = bc[blk["last"]:blk["last"] + 1, :]
    row_id = lax.broadcasted_iota(jnp.int32, (c, LIN_W), 0)
    zero_half = jnp.zeros((half, LIN_W), F32)
    parts = []
    for s in range(c):
        live = slice(0, c)
        keep = (row_id[live] >= s) if d == 0 else (row_id[live] <= s)
        p = jnp.where(keep, q[live] * jnp.exp(bc[live] - bc[s:s + 1, :]) * kk[s:s + 1, :], 0.0)
        if live.stop - live.start == half:
            p = jnp.concatenate([zero_half, p] if live.start else [p, zero_half], axis=0)
        parts.append(p.astype(BF16))
    r = _dot(jnp.concatenate(parts, axis=0), ones)
    o = _dot_t(blk["qe"][rows], st.astype(BF16))
    for s in range(c):
        o = o + r[s * c:(s + 1) * c, :] * v[s:s + 1, :]
    o_ref[0, rows, :] = o
    k_tail = (kk * jnp.exp(bl - bc)).astype(BF16)
    upd = _dot(vt_ref[0, :, rows].astype(BF16), k_tail)
    return st * jnp.exp(bl) + upd * seg_mask


def _hgrn_scan_kernel(layer, qf, vf, ff, vtf, qb, vb, fb, vtb, lb_ref, ones_ref, s0_ref, of_ref, ob_ref, st_ref):
    @pl.when(pl.program_id(1) == 0)
    def _():
        st_ref[...] = s0_ref[...]
    lb = _hgrn_lower_bound(layer, lb_ref)
    ones = ones_ref[...]
    seg_mask = ones.astype(F32)
    fwd = _hgrn_block(0, lb, qf, vf, ff)
    bwd = _hgrn_block(1, lb, qb, vb, fb)
    st_f = st_ref[0, 0]
    st_b = st_ref[0, 1]
    n_chunks = SCAN_BLOCK // HG_CHUNK
    for step in range(n_chunks):
        st_f = _hgrn_chunk(fwd, step, vtf, ones, seg_mask, st_f, of_ref)
        st_b = _hgrn_chunk(bwd, n_chunks - 1 - step, vtb, ones, seg_mask, st_b, ob_ref)
    st_ref[0, 0] = st_f
    st_ref[0, 1] = st_b


def _hgrn_scan(layer, oh, vt, lb, ones, s0):
    b, t, _ = oh.shape
    nb = t // SCAN_BLOCK

    def col(cb, rev):
        return (lambda bi, i: (bi, nb - 1 - i, cb)) if rev else (lambda bi, i: (bi, i, cb))

    blk = lambda im: pl.BlockSpec((1, SCAN_BLOCK, LIN_W), im)
    state = pl.BlockSpec((1, 2, LIN_W, LIN_W), lambda bi, i: (bi, 0, 0, 0))
    return pl.pallas_call(
        functools.partial(_hgrn_scan_kernel, layer),
        grid=(b, nb),
        in_specs=[blk(col(0, False)), blk(col(1, False)), blk(col(2, False)),
                  pl.BlockSpec((1, LIN_W, SCAN_BLOCK), lambda bi, i: (bi, 0, i)),
                  blk(col(0, True)), blk(col(1, True)), blk(col(3, True)),
                  pl.BlockSpec((1, LIN_W, SCAN_BLOCK), lambda bi, i: (bi, 0, nb - 1 - i)),
                  pl.BlockSpec((8, LIN_W), lambda bi, i: (0, 0)),
                  pl.BlockSpec((LIN_W, LIN_W), lambda bi, i: (0, 0)),
                  state],
        out_specs=[blk(col(0, False)), blk(col(0, True)), state],
        out_shape=[jax.ShapeDtypeStruct((b, t, LIN_W), F32),
                   jax.ShapeDtypeStruct((b, t, LIN_W), F32),
                   jax.ShapeDtypeStruct((b, 2, LIN_W, LIN_W), F32)],
        compiler_params=pltpu.CompilerParams(dimension_semantics=("arbitrary", "arbitrary"),
                                             vmem_limit_bytes=VMEM_LIMIT),
        name="hgrn_scan",
    )(oh, oh, oh, vt, oh, oh, oh, vt, lb, ones, s0)


def _post_kernel(x_ref, ogf_ref, ogb_ref, z_ref, gnw_ref, ohf_ref, ohb_ref, hg_ref, hnw_ref, om_ref, seg_ref,
                 wout_ref, gpm_ref, gt1_ref, gpf_ref, sc2_ref, sh2_ref, gt2_ref, gff_ref,
                 wa_ref, wb_ref, wo_ref, out_ref):
    seg = seg_ref[...]
    inv = 1.0 / HD
    og = ogf_ref[...] + ogb_ref[...]
    og = og * lax.rsqrt(_seg_sum(og * og, seg) * inv + EPS) * gnw_ref[...] * _silu(z_ref[...])
    oh = ohf_ref[...] + ohb_ref[...]
    oh = oh * lax.rsqrt(_seg_sum(oh * oh, seg) * inv + EPS) * hnw_ref[...] * _sigmoid(hg_ref[...])
    mix = jnp.concatenate([og.astype(BF16), oh.astype(BF16), om_ref[...].astype(BF16)], axis=-1)
    x1 = x_ref[...] + gt1_ref[0] * _rms(_dot(mix, wout_ref[...]), gpm_ref[...])
    hb = (_rms(x1, gpf_ref[...]) * (1.0 + sc2_ref[0]) + sh2_ref[0]).astype(BF16)
    fw = D_FF // FF_SPLIT
    y = None
    for part in range(FF_SPLIT):
        cs = slice(part * fw, (part + 1) * fw)
        act = (_silu(_dot(hb, wa_ref[:, cs])) * _dot(hb, wb_ref[:, cs])).astype(BF16)
        contrib = _dot(act, wo_ref[cs, :])
        y = contrib if y is None else y + contrib
    out_ref[...] = x1 + gt2_ref[0] * _rms(y, gff_ref[...])


def _post(x, ogf, ogb, og, gnw, ohf, ohb, oh, hnw, om, seg, wout, gpm, gt1, gpf, sc2, sh2, gt2, gff, wa, wb, wo, tm):
    m = x.shape[0]
    tiles = m // tm
    per = tiles // gt1.shape[0]
    row = lambda i: (i // per, 0, 0)
    tok = lambda w: pl.BlockSpec((tm, w), lambda i: (i, 0))
    vec = lambda w: pl.BlockSpec((1, w), lambda i: (0, 0))
    mod = pl.BlockSpec((1, 1, D_MODEL), row)
    once = lambda r, c: pl.BlockSpec((r, c), lambda i: (0, 0), pipeline_mode=pl.Buffered(1))
    return pl.pallas_call(
        _post_kernel,
        grid=(tiles,),
        in_specs=[tok(D_MODEL), tok(LIN_W), tok(LIN_W),
                  pl.BlockSpec((tm, LIN_W), lambda i: (i, G_W // LIN_W - 1)), vec(LIN_W),
                  tok(LIN_W), tok(LIN_W),
                  pl.BlockSpec((tm, LIN_W), lambda i: (i, H_W // LIN_W - 1)), vec(LIN_W),
                  tok(HEADS * MLA_DV), once(LIN_W, LIN_W),
                  once(D_MODEL, D_MODEL), vec(D_MODEL), mod, vec(D_MODEL), mod, mod, mod, vec(D_MODEL),
                  once(D_MODEL, D_FF), once(D_MODEL, D_FF), once(D_FF, D_MODEL)],
        out_specs=tok(D_MODEL),
        out_shape=jax.ShapeDtypeStruct((m, D_MODEL), F32),
        compiler_params=pltpu.CompilerParams(dimension_semantics=("arbitrary",), vmem_limit_bytes=VMEM_LIMIT),
        name="post",
    )(x, ogf, ogb, og, gnw, ohf, ohb, oh, hnw, om, seg, wout, gpm, gt1, gpf, sc2, sh2, gt2, gff, wa, wb, wo)


def _rope_tables(t_len, use_rope):
    scale = MLA_QK ** -0.5
    if use_rope:
        rows = t_len // GRID_W
        row = jnp.repeat(jnp.arange(rows, dtype=F32), GRID_W)
        col = jnp.tile(jnp.arange(GRID_W, dtype=F32), rows)
        nf = MLA_ROPE // 4
        inv = ROPE_BASE ** (-jnp.arange(nf, dtype=F32) / nf)
        ar, ac = row[:, None] * inv, col[:, None] * inv
        cos = jnp.concatenate([jnp.cos(ar), jnp.cos(ar), jnp.cos(ac), jnp.cos(ac)], axis=-1)
        sin = jnp.concatenate([-jnp.sin(ar), jnp.sin(ar), -jnp.sin(ac), jnp.sin(ac)], axis=-1)
    else:
        cos = jnp.ones((t_len, MLA_ROPE), F32)
        sin = jnp.zeros((t_len, MLA_ROPE), F32)
    z64 = jnp.zeros((t_len, MLA_ROPE), F32)
    qa = scale * jnp.concatenate([jnp.ones((t_len, MLA_NOPE), F32), cos, z64], axis=-1)
    qb = scale * jnp.concatenate([jnp.zeros((t_len, MLA_NOPE), F32), sin, z64], axis=-1)
    ck = jnp.concatenate([cos, z64], axis=-1)
    sk = jnp.concatenate([sin, z64], axis=-1)
    return qa, qb, ck, sk


def _layer_weights(l, w_in, w_out, gdn_conv_w, gdn_a_log, gdn_dt_bias, gdn_norm_w, lower_bounds, hgrn_norm_w,
                   mla_q_norm_w, mla_w_uq, mla_kv_norm_w, mla_w_ukv, w_ffn_in, w_ffn_out,
                   g_pre_mix, g_post_mix, g_pre_ffn, g_post_ffn):
    wi = w_in[l]
    w_cat = jnp.concatenate([wi[:, _PERM_G], wi[:, _PERM_H], wi[:, _PERM_A],
                             jnp.zeros((D_MODEL, A_W - _PERM_A.shape[0]), F32)], axis=1).astype(BF16)
    convw = jnp.concatenate([gdn_conv_w[l][_PERM_CONV].T, jnp.zeros((3, G_CONV), F32)], axis=0)
    neg_a = -jnp.exp(gdn_a_log[l].astype(F32)).reshape(1, 8)
    dt = gdn_dt_bias[l].astype(F32).reshape(1, 8)
    pad = lambda r: jnp.pad(r, ((0, 0), (0, 128 - r.shape[1])))
    gconst = jnp.concatenate([pad(neg_a), pad(dt), jnp.zeros((6, 128), F32)], axis=0)
    z8 = jnp.zeros((8, SCAN_BLOCK), F32)
    gconst_t = jnp.concatenate([jnp.broadcast_to(neg_a.T, (8, SCAN_BLOCK)), z8,
                                jnp.broadcast_to(dt.T, (8, SCAN_BLOCK)), z8], axis=0)
    vec = lambda a: a.reshape(1, -1).astype(F32)
    return dict(
        w_cat=w_cat, convw=convw, gconst=gconst, gconst_t=gconst_t,
        gnw=vec(jnp.tile(gdn_norm_w[l], HEADS)), hnw=vec(jnp.tile(hgrn_norm_w[l], HEADS)),
        layer=l, lb=lower_bounds,
        qnw=vec(mla_q_norm_w[l]), kvnw=vec(mla_kv_norm_w[l]),
        wuq=mla_w_uq[l][:, _PERM_UQ].astype(BF16), wukv=mla_w_ukv[l][:, _PERM_UKV].astype(BF16),
        wout=w_out[l].astype(BF16), wa=w_ffn_in[l][:, :D_FF].astype(BF16), wb=w_ffn_in[l][:, D_FF:].astype(BF16),
        wo=w_ffn_out[l].astype(BF16),
        g_pre_mix=vec(g_pre_mix[l]), g_post_mix=vec(g_post_mix[l]),
        g_pre_ffn=vec(g_pre_ffn[l]), g_post_ffn=vec(g_post_ffn[l]))


def _trunk_layer(x, mod, w, gdn_s0, hg_s0, tables, ctx_keys, seg512, seg256):
    b, t, _ = x.shape
    m = b * t
    tm = min(TOKEN_TILE, t)
    sh1, sc1, gt1, sh2, sc2, gt2 = mod
    xf = x.reshape(m, D_MODEL)
    og, oh, oa = _premix(xf, w["g_pre_mix"], sc1, sh1, w["w_cat"], tm)

    qa, qb, ck, sk = tables
    qcat, kcat, vcat, ckv = _mla_prep(oa, w["qnw"], w["kvnw"], w["wuq"], w["wukv"], qa, qb, ck, sk, tm, t)
    kw = HEADS * MLA_HEAD_PAD
    qcat = qcat.reshape(b, t, kw)
    kcat = kcat.reshape(b, t, kw)
    vcat = vcat.reshape(b, t, HEADS * MLA_DV)
    if ctx_keys is not None:
        kcat = jnp.concatenate([kcat, ctx_keys[0]], axis=1)
        vcat = jnp.concatenate([vcat, ctx_keys[1]], axis=1)
    o_mla = _attn(qcat, kcat, vcat, min(ATTN_Q_TILE, t)).reshape(m, HEADS * MLA_DV)

    gq, gk, gvk = _gdn_prep(og, w["convw"], seg512, tm, t)
    ab = oa[:, A_AB:A_AB + 128].reshape(b, t, 128)
    abt = jnp.swapaxes(ab[:, :, 0:16], 1, 2)
    gk3 = gk.reshape(b, t, LIN_W)
    ogf, ogb, gdn_state = _gdn_scan(gq.reshape(b, t, LIN_W), gk3, gvk.reshape(b, t, 2 * LIN_W),
                                    jnp.swapaxes(gk3, 1, 2), ab, abt, w["gconst"], w["gconst_t"], gdn_s0)

    oh3 = oh.reshape(b, t, H_W)
    vt = jnp.swapaxes(oh3[:, :, LIN_W:2 * LIN_W], 1, 2)
    ohf, ohb, hg_state = _hgrn_scan(w["layer"], oh3, vt, w["lb"], seg256, hg_s0)

    x_new = _post(xf, ogf.reshape(m, LIN_W), ogb.reshape(m, LIN_W), og, w["gnw"],
                  ohf.reshape(m, LIN_W), ohb.reshape(m, LIN_W), oh, w["hnw"], o_mla, seg256,
                  w["wout"], w["g_post_mix"], gt1, w["g_pre_ffn"], sc2, sh2, gt2, w["g_post_ffn"],
                  w["wa"], w["wb"], w["wo"], tm)
    mkr = oa[:, A_KR:A_KR + MLA_ROPE]
    return x_new.reshape(b, t, D_MODEL), gdn_state, hg_state, ckv.reshape(b, t, MLA_KV_RANK), mkr.reshape(b, t, MLA_ROPE)


def _hg_state_to_block(s):
    st = jnp.swapaxes(s, -1, -2)
    eye = jnp.eye(HEADS, dtype=s.dtype)
    big = st[:, :, :, :, None, :] * eye[None, None, :, None, :, None]
    return big.reshape(s.shape[0], 2, LIN_W, LIN_W)


def _hg_block_to_state(big):
    b = big.shape[0]
    r = big.reshape(b, 2, HEADS, HD, HEADS, HD)
    diag = jnp.stack([r[:, :, h, :, h, :] for h in range(HEADS)], axis=2)
    return jnp.swapaxes(diag, -1, -2)


def kernel(x_prompt, x_sample, cache_mla_ckv, cache_mla_krope, state_gdn, state_hgrn, c, c_ctx, w_ada, b_ada,
           g_pre_mix, g_post_mix, g_pre_ffn, g_post_ffn, w_in, w_out, gdn_conv_w, gdn_a_log, gdn_dt_bias,
           gdn_norm_w, hgrn_lb, hgrn_norm_w, mla_q_norm_w, mla_w_uq, mla_kv_norm_w, mla_w_ukv, w_ffn_in, w_ffn_out):
    bp, tp, _ = x_prompt.shape
    bd, td, _ = x_sample.shape
    past = cache_mla_ckv.shape[2]
    assert 1 + bd <= MOD_ROWS and tp % SCAN_BLOCK == 0 and td % SCAN_BLOCK == 0

    lower_bounds = jnp.pad(hgrn_lb.astype(F32), ((0, 8 - DEPTH), (0, 0)))

    cond = jnp.concatenate([c_ctx[None, :], c, jnp.zeros((MOD_ROWS - 1 - bd, D_MODEL), F32)], axis=0)
    mod_all = _ada(cond, w_ada, b_ada)

    seg512 = _seg_ones(2 * LIN_W)
    seg256 = _seg_ones(LIN_W)
    tab_ctx = _rope_tables(tp, False)
    tab_lat = _rope_tables(td, True)
    ones_k = jnp.concatenate([jnp.ones((past, MLA_ROPE), F32), jnp.zeros((past, MLA_ROPE), F32)], axis=-1)
    zeros_k = jnp.zeros((past, 128), F32)

    weights = [_layer_weights(l, w_in, w_out, gdn_conv_w, gdn_a_log, gdn_dt_bias, gdn_norm_w, lower_bounds,
                              hgrn_norm_w, mla_q_norm_w, mla_w_uq, mla_kv_norm_w, mla_w_ukv, w_ffn_in, w_ffn_out,
                              g_pre_mix, g_post_mix, g_pre_ffn, g_post_ffn) for l in range(DEPTH)]

    def mods(l, lo, n):
        rows = mod_all[l, lo:lo + n].reshape(n, 1, 6 * D_MODEL)
        return tuple(rows[:, :, i * D_MODEL:(i + 1) * D_MODEL] for i in range(6))

    xp = x_prompt
    zero_g = jnp.zeros((bp, 2 * HEADS, HD, HD), F32)
    zero_h = jnp.zeros((bp, 2, LIN_W, LIN_W), F32)
    ckv_l, kr_l, gs_l, hs_l = [], [], [], []
    for l in range(DEPTH):
        xp, gs, hs, ckv, mkr = _trunk_layer(xp, mods(l, 0, 1), weights[l], zero_g, zero_h, tab_ctx, None,
                                            seg512, seg256)
        ckv_l.append(ckv)
        kr_l.append(mkr)
        gs_l.append(gs.reshape(bp, 2, HEADS, HD, HD))
        hs_l.append(_hg_block_to_state(hs))

    xs = x_sample
    for l in range(DEPTH):
        kr128 = jnp.pad(cache_mla_krope[:, l].reshape(bd * past, MLA_ROPE), ((0, 0), (0, 128 - MLA_ROPE)))
        k_ctx, v_ctx = _kv_up(cache_mla_ckv[:, l].reshape(bd * past, MLA_KV_RANK), kr128, ones_k, zeros_k,
                              weights[l]["wukv"], past)
        ctx_keys = (k_ctx.reshape(bd, past, HEADS * MLA_HEAD_PAD), v_ctx.reshape(bd, past, HEADS * MLA_DV))
        xs, _, _, _, _ = _trunk_layer(xs, mods(l, 1, bd), weights[l],
                                      state_gdn[:, l].reshape(bd, 2 * HEADS, HD, HD),
                                      _hg_state_to_block(state_hgrn[:, l]), tab_lat, ctx_keys, seg512, seg256)

    return (xp, xs, jnp.stack(ckv_l, axis=1), jnp.stack(kr_l, axis=1),
            jnp.stack(gs_l, axis=1), jnp.stack(hs_l, axis=1))
```

```python
import functools
import math

import numpy as np
import jax
import jax.numpy as jnp
from jax import lax
from jax.experimental import pallas as pl
from jax.experimental.pallas import tpu as pltpu

F32 = jnp.float32
BF16 = jnp.bfloat16

D_MODEL = 1024
DEPTH = 2
GRID_W = 64
EPS = 1e-6
GATE_FLOOR = 1e-30
HEADS = 4
HD = 64
LIN_W = HEADS * HD
GDN_CHUNK = 64
HG_CHUNK = 16
SCAN_BLOCK = 128
GDN_BLOCK = 256
LOG2E = 1.4426950408889634
MLA_Q_RANK = 384
MLA_KV_RANK = 256
MLA_NOPE = 128
MLA_ROPE = 64
MLA_DV = 128
MLA_QK = MLA_NOPE + MLA_ROPE
MLA_HEAD_PAD = 256
ROPE_BASE = 10000.0
D_FF = -(-8 * D_MODEL // (3 * 256)) * 256
FF_SPLIT = 2
TOKEN_TILE = 512
ATTN_Q_TILE = 256
MOD_ROWS = 16
VMEM_LIMIT = 56 * 1024 * 1024

_OFF = {}
_o = 0
for _n, _s in (("gq", 256), ("gk", 256), ("gv", 256), ("gz", 256), ("ga", 8), ("gb", 8),
               ("hq", 256), ("hi", 256), ("hf", 512), ("hg", 256),
               ("mcq", MLA_Q_RANK), ("mckv", MLA_KV_RANK), ("mkr", MLA_ROPE)):
    _OFF[_n] = _o
    _o += _s
IN_DIM = _o

G_W = 1280
G_CONV = 1024
H_W = 1280
A_W = 896
A_KR = MLA_Q_RANK + MLA_KV_RANK
A_AB = A_KR + 2 * MLA_ROPE


def _rope_swap_idx():
    j = np.arange(MLA_ROPE)
    return np.where((j % 32) < 16, j + 16, j - 16)


def _in_perm():
    ar = np.arange
    vk = np.concatenate([np.concatenate([_OFF["gv"] + 64 * h + ar(64), _OFF["gk"] + 64 * h + ar(64)])
                         for h in range(HEADS)])
    g = np.concatenate([_OFF["gq"] + ar(256), _OFF["gk"] + ar(256), vk, _OFF["gz"] + ar(256)])
    hh = np.concatenate([_OFF["hq"] + ar(256), _OFF["hi"] + ar(256), _OFF["hf"] + ar(512), _OFF["hg"] + ar(256)])
    a = np.concatenate([_OFF["mcq"] + ar(MLA_Q_RANK), _OFF["mckv"] + ar(MLA_KV_RANK), _OFF["mkr"] + ar(MLA_ROPE),
                        _OFF["mkr"] + _rope_swap_idx(), _OFF["ga"] + ar(8), _OFF["gb"] + ar(8)])
    conv = np.concatenate([ar(256), 256 + ar(256),
                           np.concatenate([np.concatenate([512 + 64 * h + ar(64), 256 + 64 * h + ar(64)])
                                           for h in range(HEADS)])])
    return g, hh, a, conv


_PERM_G, _PERM_H, _PERM_A, _PERM_CONV = _in_perm()


def _uq_perm():
    sw = _rope_swap_idx()
    cols = []
    for h in range(HEADS):
        base = h * MLA_QK
        cols += [base + np.arange(MLA_NOPE), base + MLA_NOPE + np.arange(MLA_ROPE), base + MLA_NOPE + sw]
    return np.concatenate(cols)


def _ukv_perm():
    per = MLA_NOPE + MLA_DV
    kn = np.concatenate([h * per + np.arange(MLA_NOPE) for h in range(HEADS)])
    vv = np.concatenate([h * per + MLA_NOPE + np.arange(MLA_DV) for h in range(HEADS)])
    return np.concatenate([kn, vv])


_PERM_UQ = _uq_perm()
_PERM_UKV = _ukv_perm()


def _seg_ones(n):
    i = np.arange(n) // HD
    return jnp.asarray((i[:, None] == i[None, :]).astype(np.float32), dtype=BF16)


def _rms(x, w):
    return x * lax.rsqrt(jnp.mean(x * x, axis=-1, keepdims=True) + EPS) * w


def _dot(a, b):
    return jnp.dot(a, b, preferred_element_type=F32)


def _dot_t(a, b):
    return lax.dot_general(a, b, (((1,), (1,)), ((), ())), preferred_element_type=F32)


def _split3(x):
    hi = x.astype(BF16)
    r1 = x - hi.astype(F32)
    mid = r1.astype(BF16)
    lo = (r1 - mid.astype(F32)).astype(BF16)
    return hi, mid, lo


def _split2(x):
    hi = x.astype(BF16)
    return hi, (x - hi.astype(F32)).astype(BF16)


def _seg_sum(x, seg):
    hi, lo = _split2(x)
    return _dot(hi, seg) + _dot(lo, seg)


def _mask_dot(mask01, x):
    hi, mid, lo = _split3(x)
    return _dot(mask01, hi) + _dot(mask01, mid) + _dot(mask01, lo)


def _dot_mask_t(x, mask01):
    hi, mid, lo = _split3(x)
    return _dot_t(hi, mask01) + _dot_t(mid, mask01) + _dot_t(lo, mask01)


def _softplus(x):
    return jnp.maximum(x, 0.0) + jnp.log1p(jnp.exp(-jnp.abs(x)))


def _sigmoid(x):
    return jax.nn.sigmoid(x)


def _silu(x):
    return x * jax.nn.sigmoid(x)


def _ada_kernel(c_ref, w_ref, b_ref, o_ref):
    s = _silu(c_ref[...]).astype(BF16)
    o_ref[0] = _dot(s, w_ref[0].astype(BF16)) + b_ref[0]


def _ada(cond, w_ada, b_ada):
    n = w_ada.shape[-1]
    tn = 1536
    return pl.pallas_call(
        _ada_kernel,
        grid=(DEPTH, n // tn),
        in_specs=[pl.BlockSpec((MOD_ROWS, D_MODEL), lambda l, j: (0, 0)),
                  pl.BlockSpec((1, D_MODEL, tn), lambda l, j: (l, 0, j)),
                  pl.BlockSpec((1, 1, tn), lambda l, j: (l, 0, j))],
        out_specs=pl.BlockSpec((1, MOD_ROWS, tn), lambda l, j: (l, 0, j)),
        out_shape=jax.ShapeDtypeStruct((DEPTH, MOD_ROWS, n), F32),
        compiler_params=pltpu.CompilerParams(dimension_semantics=("arbitrary", "arbitrary"),
                                             vmem_limit_bytes=VMEM_LIMIT),
        name="ada",
    )(cond, w_ada, b_ada.reshape(DEPTH, 1, n))


def _premix_kernel(x_ref, g_ref, sc_ref, sh_ref, w_ref, og_ref, oh_ref, oa_ref, vt_ref, abt_ref):
    h = _rms(x_ref[...], g_ref[...]) * (1.0 + sc_ref[0]) + sh_ref[0]
    hb = h.astype(BF16)
    og_ref[...] = _dot(hb, w_ref[:, 0:G_W])
    oh = _dot(hb, w_ref[:, G_W:G_W + H_W])
    oh_ref[...] = oh
    oa = _dot(hb, w_ref[:, G_W + H_W:G_W + H_W + A_W])
    oa_ref[...] = oa
    vt_ref[0] = oh[:, LIN_W:2 * LIN_W].T
    abt_ref[0] = oa[:, A_AB:A_AB + 128].T[0:16, :]


def _premix(x, g, sc, sh, w, tm, t_len):
    m = x.shape[0]
    tiles = m // tm
    per = tiles // sc.shape[0]
    row = lambda i: (i // per, 0, 0)
    tps = t_len // tm
    seq = lambda i: (i // tps, 0, i % tps)
    wt = G_W + H_W + A_W
    return pl.pallas_call(
        _premix_kernel,
        grid=(tiles,),
        in_specs=[pl.BlockSpec((tm, D_MODEL), lambda i: (i, 0)),
                  pl.BlockSpec((1, D_MODEL), lambda i: (0, 0)),
                  pl.BlockSpec((1, 1, D_MODEL), row),
                  pl.BlockSpec((1, 1, D_MODEL), row),
                  pl.BlockSpec((D_MODEL, wt), lambda i: (0, 0))],
        out_specs=[pl.BlockSpec((tm, G_W), lambda i: (i, 0)),
                   pl.BlockSpec((tm, H_W), lambda i: (i, 0)),
                   pl.BlockSpec((tm, A_W), lambda i: (i, 0)),
                   pl.BlockSpec((1, LIN_W, tm), seq),
                   pl.BlockSpec((1, 16, tm), seq)],
        out_shape=[jax.ShapeDtypeStruct((m, G_W), F32),
                   jax.ShapeDtypeStruct((m, H_W), F32),
                   jax.ShapeDtypeStruct((m, A_W), F32),
                   jax.ShapeDtypeStruct((m // t_len, LIN_W, t_len), F32),
                   jax.ShapeDtypeStruct((m // t_len, 16, t_len), F32)],
        compiler_params=pltpu.CompilerParams(dimension_semantics=("arbitrary",), vmem_limit_bytes=VMEM_LIMIT),
        name="premix",
    )(x, g, sc, sh, w)


def _keys_from_latent(ckv, kr128, ck, sk, wukv_ref, k_out, v_out):
    kv = _dot(ckv.astype(BF16), wukv_ref[...])
    krot = (kr128 * ck + pltpu.roll(kr128, 64, 1) * sk).astype(BF16)
    for h in range(HEADS):
        k_out[:, MLA_HEAD_PAD * h:MLA_HEAD_PAD * h + MLA_NOPE] = kv[:, MLA_NOPE * h:MLA_NOPE * (h + 1)].astype(BF16)
        k_out[:, MLA_HEAD_PAD * h + MLA_NOPE:MLA_HEAD_PAD * (h + 1)] = krot
    v_out[...] = kv[:, HEADS * MLA_NOPE:].astype(BF16)


def _mla_prep_kernel(a_ref, qnw_ref, kvnw_ref, wuq_ref, wukv_ref, qa_ref, qb_ref, ck_ref, sk_ref,
                     q_out, k_out, v_out, ckv_out):
    qn = _rms(a_ref[:, 0:MLA_Q_RANK], qnw_ref[...])
    y = _dot(qn.astype(BF16), wuq_ref[...])
    z = pltpu.roll(y, HEADS * MLA_HEAD_PAD - MLA_ROPE, 1)
    qa = qa_ref[...]
    qb = qb_ref[...]
    for h in range(HEADS):
        sl = slice(MLA_HEAD_PAD * h, MLA_HEAD_PAD * (h + 1))
        q_out[:, sl] = (y[:, sl] * qa + z[:, sl] * qb).astype(BF16)
    ckv = _rms(a_ref[:, MLA_Q_RANK:A_KR], kvnw_ref[...])
    ckv_out[...] = ckv
    _keys_from_latent(ckv, a_ref[:, A_KR:A_AB], ck_ref[...], sk_ref[...], wukv_ref, k_out, v_out)


def _mla_prep(oa, qnw, kvnw, wuq, wukv, qa, qb, ck, sk, tm, t_len):
    m = oa.shape[0]
    tiles = m // tm
    tps = t_len // tm
    pos = lambda i: (i % tps, 0)
    full = lambda i: (0, 0)
    kw = HEADS * MLA_HEAD_PAD
    return pl.pallas_call(
        _mla_prep_kernel,
        grid=(tiles,),
        in_specs=[pl.BlockSpec((tm, A_W), lambda i: (i, 0)),
                  pl.BlockSpec((1, MLA_Q_RANK), full),
                  pl.BlockSpec((1, MLA_KV_RANK), full),
                  pl.BlockSpec((MLA_Q_RANK, kw), full),
                  pl.BlockSpec((MLA_KV_RANK, HEADS * (MLA_NOPE + MLA_DV)), full),
                  pl.BlockSpec((tm, MLA_HEAD_PAD), pos),
                  pl.BlockSpec((tm, MLA_HEAD_PAD), pos),
                  pl.BlockSpec((tm, 128), pos),
                  pl.BlockSpec((tm, 128), pos)],
        out_specs=[pl.BlockSpec((tm, kw), lambda i: (i, 0)),
                   pl.BlockSpec((tm, kw), lambda i: (i, 0)),
                   pl.BlockSpec((tm, HEADS * MLA_DV), lambda i: (i, 0)),
                   pl.BlockSpec((tm, MLA_KV_RANK), lambda i: (i, 0))],
        out_shape=[jax.ShapeDtypeStruct((m, kw), BF16),
                   jax.ShapeDtypeStruct((m, kw), BF16),
                   jax.ShapeDtypeStruct((m, HEADS * MLA_DV), BF16),
                   jax.ShapeDtypeStruct((m, MLA_KV_RANK), F32)],
        compiler_params=pltpu.CompilerParams(dimension_semantics=("arbitrary",), vmem_limit_bytes=VMEM_LIMIT),
        name="mla_prep",
    )(oa, qnw, kvnw, wuq, wukv, qa, qb, ck, sk)


def _kv_up_kernel(ckv_ref, kr_ref, ck_ref, sk_ref, wukv_ref, k_out, v_out):
    _keys_from_latent(ckv_ref[...], kr_ref[...], ck_ref[...], sk_ref[...], wukv_ref, k_out, v_out)


def _kv_up(ckv, kr128, ck, sk, wukv, tm):
    m = ckv.shape[0]
    kw = HEADS * MLA_HEAD_PAD
    return pl.pallas_call(
        _kv_up_kernel,
        grid=(m // tm,),
        in_specs=[pl.BlockSpec((tm, MLA_KV_RANK), lambda i: (i, 0)),
                  pl.BlockSpec((tm, 128), lambda i: (i, 0)),
                  pl.BlockSpec((tm, 128), lambda i: (0, 0)),
                  pl.BlockSpec((tm, 128), lambda i: (0, 0)),
                  pl.BlockSpec((MLA_KV_RANK, HEADS * (MLA_NOPE + MLA_DV)), lambda i: (0, 0))],
        out_specs=[pl.BlockSpec((tm, kw), lambda i: (i, 0)),
                   pl.BlockSpec((tm, HEADS * MLA_DV), lambda i: (i, 0))],
        out_shape=[jax.ShapeDtypeStruct((m, kw), BF16),
                   jax.ShapeDtypeStruct((m, HEADS * MLA_DV), BF16)],
        compiler_params=pltpu.CompilerParams(dimension_semantics=("arbitrary",), vmem_limit_bytes=VMEM_LIMIT),
        name="kv_up",
    )(ckv, kr128, ck, sk, wukv)


def _attn_kernel(q_ref, k_ref, v_ref, o_ref):
    for h in range(HEADS):
        sl = slice(MLA_HEAD_PAD * h, MLA_HEAD_PAD * (h + 1))
        s = _dot_t(q_ref[0, :, sl], k_ref[0, :, sl])
        p = jnp.exp2(s - jnp.max(s, axis=-1, keepdims=True))
        l = jnp.sum(p, axis=-1, keepdims=True)
        o = _dot(p.astype(BF16), v_ref[0, :, MLA_DV * h:MLA_DV * (h + 1)])
        o_ref[0, :, MLA_DV * h:MLA_DV * (h + 1)] = o / l


def _attn(q, k, v, tq):
    b, t, kw = q.shape
    s = k.shape[1]
    return pl.pallas_call(
        _attn_kernel,
        grid=(b, t // tq),
        in_specs=[pl.BlockSpec((1, tq, kw), lambda bi, i: (bi, i, 0)),
                  pl.BlockSpec((1, s, kw), lambda bi, i: (bi, 0, 0)),
                  pl.BlockSpec((1, s, HEADS * MLA_DV), lambda bi, i: (bi, 0, 0))],
        out_specs=pl.BlockSpec((1, tq, HEADS * MLA_DV), lambda bi, i: (bi, i, 0)),
        out_shape=jax.ShapeDtypeStruct((b, t, HEADS * MLA_DV), F32),
        compiler_params=pltpu.CompilerParams(dimension_semantics=("arbitrary", "arbitrary"),
                                             vmem_limit_bytes=VMEM_LIMIT),
        name="attn",
    )(q, k, v)


def _gdn_prep_kernel(tps, cur_ref, prev_ref, next_ref, w_ref, seg_ref, q_out, k_out, vk_out, kt_out):
    i = pl.program_id(0)
    tm = cur_ref.shape[0]
    has_prev = ((i % tps) != 0).astype(F32)
    has_next = ((i % tps) != (tps - 1)).astype(F32)
    xc = jnp.concatenate([prev_ref[...] * has_prev, cur_ref[...], next_ref[...] * has_next], axis=0)
    n = tm + 16
    y = xc[8:8 + tm] * w_ref[2:3, :]
    for j in (0, 1, 3, 4):
        d = j - 2
        y = y + pltpu.roll(xc, (n - d) % n, 0)[8:8 + tm] * w_ref[j:j + 1, :]
    y = _silu(y)
    seg = seg_ref[...]
    qk = y[:, 0:2 * LIN_W]
    qk = qk * lax.rsqrt(_seg_sum(qk * qk, seg) + EPS)
    q_out[...] = qk[:, 0:LIN_W] * (HD ** -0.5)
    k_out[...] = qk[:, LIN_W:2 * LIN_W]
    kt_out[0] = qk[:, LIN_W:2 * LIN_W].T
    vk = y[:, 2 * LIN_W:]
    is_k = (lax.broadcasted_iota(jnp.int32, vk.shape, 1) & HD) != 0
    vk_out[...] = jnp.where(is_k, vk * lax.rsqrt(_seg_sum(vk * vk, seg) + EPS), vk)


def _gdn_prep(og, convw, seg, tm, t_len):
    m = og.shape[0]
    tiles = m // tm
    tps = t_len // tm
    r8 = tm // 8
    last8 = m // 8 - 1
    return pl.pallas_call(
        functools.partial(_gdn_prep_kernel, tps),
        grid=(tiles,),
        in_specs=[pl.BlockSpec((tm, G_CONV), lambda i: (i, 0)),
                  pl.BlockSpec((8, G_CONV), lambda i: (jnp.maximum(i * r8 - 1, 0), 0)),
                  pl.BlockSpec((8, G_CONV), lambda i: (jnp.minimum((i + 1) * r8, last8), 0)),
                  pl.BlockSpec((8, G_CONV), lambda i: (0, 0)),
                  pl.BlockSpec((2 * LIN_W, 2 * LIN_W), lambda i: (0, 0))],
        out_specs=[pl.BlockSpec((tm, LIN_W), lambda i: (i, 0)),
                   pl.BlockSpec((tm, LIN_W), lambda i: (i, 0)),
                   pl.BlockSpec((tm, 2 * LIN_W), lambda i: (i, 0)),
                   pl.BlockSpec((1, LIN_W, tm), lambda i: (i // tps, 0, i % tps))],
        out_shape=[jax.ShapeDtypeStruct((m, LIN_W), F32),
                   jax.ShapeDtypeStruct((m, LIN_W), F32),
                   jax.ShapeDtypeStruct((m, 2 * LIN_W), F32),
                   jax.ShapeDtypeStruct((m // t_len, LIN_W, t_len), F32)],
        compiler_params=pltpu.CompilerParams(dimension_semantics=("arbitrary",), vmem_limit_bytes=VMEM_LIMIT),
        name="gdn_prep",
    )(og, og, og, convw, seg)


def _gdn_units(d, q_ref, k_ref, vk_ref, kt_ref, ab_ref, abt_ref, gc_ref, gct_ref, o_ref):
    c = GDN_CHUNK
    ri = lax.broadcasted_iota(jnp.int32, (c, c), 0)
    ci = lax.broadcasted_iota(jnp.int32, (c, c), 1)
    incl = (ri >= ci) if d == 0 else (ri <= ci)
    strict = (ri > ci) if d == 0 else (ri < ci)
    incl01 = incl.astype(F32).astype(BF16)
    last = c - 1 if d == 0 else 0
    ab = ab_ref[0]
    g_cols = gc_ref[0:1, :] * _softplus(ab + gc_ref[1:2, :])
    beta = _sigmoid(ab)
    g_rows = gct_ref[0:16, :] * _softplus(abt_ref[0] + gct_ref[16:32, :])
    lane_is_v = lax.broadcasted_iota(jnp.int32, (c, 2 * HD), 1) < HD
    n_chunks = GDN_BLOCK // c
    units = []
    for rank, cc in enumerate(range(n_chunks) if d == 0 else range(n_chunks - 1, -1, -1)):
        rows = slice(cc * c, (cc + 1) * c)
        gcs = _mask_dot(incl01, g_cols[rows])
        gcr = _dot_mask_t(g_rows[:, rows], incl01)
        for h in range(HEADS):
            j = d * HEADS + h
            hs = slice(HD * h, HD * (h + 1))
            gcol = gcs[:, j:j + 1]
            grow = gcr[j:j + 1, :]
            bcol = beta[rows, 8 + j:9 + j]
            gl = gcol[last:last + 1, :]
            eg = jnp.exp(gcol)
            vk = vk_ref[0, rows, 2 * HD * h:2 * HD * (h + 1)]
            units.append(dict(
                rank=rank, j=j, rows=rows, hs=hs, o_ref=o_ref, strict=strict, gl=gl, bcol=bcol, eg=eg,
                q=q_ref[0, rows, hs], k=k_ref[0, rows, hs],
                k_tail=(kt_ref[0, hs, rows] * jnp.exp(gl - grow)).astype(BF16),
                decay=jnp.where(incl, jnp.exp(jnp.where(incl, gcol - grow, 0.0)), 0.0),
                x=vk * jnp.where(lane_is_v, bcol, bcol * eg)))
    return units


def _gdn_scan_kernel(qf, kf, vkf, ktf, abf, abtf, qb, kb, vkb, ktb, abb, abtb, gc_ref, gct_ref, s0_ref,
                     of_ref, ob_ref, s_ref):
    @pl.when(pl.program_id(1) == 0)
    def _():
        s_ref[...] = s0_ref[...]

    c = GDN_CHUNK
    ri = lax.broadcasted_iota(jnp.int32, (c, c), 0)
    ci = lax.broadcasted_iota(jnp.int32, (c, c), 1)
    diag_blk = jnp.right_shift(ri, 4) == jnp.right_shift(ci, 4)
    eye = (ri == ci).astype(F32)
    units = (_gdn_units(0, qf, kf, vkf, ktf, abf, abtf, gc_ref, gct_ref, of_ref)
             + _gdn_units(1, qb, kb, vkb, ktb, abb, abtb, gc_ref, gct_ref, ob_ref))
    for u in units:
        kb16 = u["k"].astype(BF16)
        u["kk"] = _dot_t((u["k"] * u["bcol"]).astype(BF16), kb16)
        u["qk"] = _dot_t(u["q"].astype(BF16), kb16)
    for u in units:
        p = jnp.where(u["strict"], u["kk"] * u["decay"], 0.0)
        pd = jnp.where(diag_blk, p, 0.0)
        u["att"] = (u["qk"] * u["decay"]).astype(BF16)
        u["poff"] = (p - pd).astype(BF16)
        u["td"] = eye - pd
        u["a"] = pd.astype(BF16)
    for _ in range(3):
        for u in units:
            u["a"] = _dot(u["a"], u["a"]).astype(BF16)
        for u in units:
            u["td"] = u["td"] + _dot(u["td"].astype(BF16), u["a"])
    for u in units:
        tdb = u["td"].astype(BF16)
        u["y"] = _dot(tdb, u["x"].astype(BF16))
        u["n"] = _dot(tdb, u["poff"])
    for u in units:
        u["n"] = u["n"].astype(BF16)
        u["x"] = u["y"]
    for _ in range(3):
        for u in units:
            u["x"] = u["y"] - _dot(u["n"], u["x"].astype(BF16))
    for rank in range(GDN_BLOCK // c):
        ranked = [u for u in units if u["rank"] == rank]
        for u in ranked:
            u["s"] = s_ref[0, u["j"]]
            u["sb"] = u["s"].astype(BF16)
            u["vb"] = (u["x"][:, 0:HD] - _dot(u["x"][:, HD:2 * HD].astype(BF16), u["sb"])).astype(BF16)
        for u in ranked:
            u["o_ref"][0, u["rows"], u["hs"]] = (_dot((u["q"] * u["eg"]).astype(BF16), u["sb"])
                                                 + _dot(u["att"], u["vb"]))
            s_ref[0, u["j"]] = u["s"] * jnp.exp(u["gl"]) + _dot(u["k_tail"], u["vb"])


def _gdn_scan(q, k, vk, kt, ab, abt, gconst, gconst_t, s0):
    b, t, _ = q.shape
    blk = GDN_BLOCK
    nb = t // blk
    fwd = lambda bi, i: (bi, i, 0)
    bwd = lambda bi, i: (bi, nb - 1 - i, 0)
    fwd_t = lambda bi, i: (bi, 0, i)
    bwd_t = lambda bi, i: (bi, 0, nb - 1 - i)

    def specs(im, im_t):
        gate_cols = lambda bi, i: im(bi, i)[:2] + (A_AB // 128,)
        return [pl.BlockSpec((1, blk, LIN_W), im),
                pl.BlockSpec((1, blk, LIN_W), im),
                pl.BlockSpec((1, blk, 2 * LIN_W), im),
                pl.BlockSpec((1, LIN_W, blk), im_t),
                pl.BlockSpec((1, blk, 128), gate_cols),
                pl.BlockSpec((1, 16, blk), im_t)]

    state = pl.BlockSpec((1, 2 * HEADS, HD, HD), lambda bi, i: (bi, 0, 0, 0))
    return pl.pallas_call(
        _gdn_scan_kernel,
        grid=(b, nb),
        in_specs=specs(fwd, fwd_t) + specs(bwd, bwd_t)
        + [pl.BlockSpec((8, 128), lambda bi, i: (0, 0)),
           pl.BlockSpec((32, blk), lambda bi, i: (0, 0)),
           state],
        out_specs=[pl.BlockSpec((1, blk, LIN_W), fwd),
                   pl.BlockSpec((1, blk, LIN_W), bwd),
                   state],
        out_shape=[jax.ShapeDtypeStruct((b, t, LIN_W), F32),
                   jax.ShapeDtypeStruct((b, t, LIN_W), F32),
                   jax.ShapeDtypeStruct((b, 2 * HEADS, HD, HD), F32)],
        compiler_params=pltpu.CompilerParams(dimension_semantics=("arbitrary", "arbitrary"),
                                             vmem_limit_bytes=VMEM_LIMIT),
        name="gdn_scan",
    )(q, k, vk, kt, ab, abt, q, k, vk, kt, ab, abt, gconst, gconst_t, s0)


def _hgrn_lower_bound(layer, lb_ref):
    raw = lb_ref[0:DEPTH, :]
    e = jnp.exp(raw - jnp.max(raw, axis=0, keepdims=True))
    gamma = e / jnp.sum(e, axis=0, keepdims=True)
    lb = jnp.zeros((1, LIN_W), F32)
    for i in range(1, layer + 1):
        lb = lb + gamma[i:i + 1, :]
    return lb


def _hgrn_block(d, lb, q_ref, v_ref, f_ref):
    c = HG_CHUNK
    ri = lax.broadcasted_iota(jnp.int32, (SCAN_BLOCK, SCAN_BLOCK), 0)
    ci = lax.broadcasted_iota(jnp.int32, (SCAN_BLOCK, SCAN_BLOCK), 1)
    same_chunk = jnp.right_shift(ri, 4) == jnp.right_shift(ci, 4)
    incl01 = (same_chunk & ((ri >= ci) if d == 0 else (ri <= ci))).astype(F32).astype(BF16)
    oml = 1.0 - lb
    q = q_ref[0]
    f = f_ref[0]
    logf = jnp.log(jnp.maximum(lb + oml * _sigmoid(f), GATE_FLOOR))
    bc = _mask_dot(incl01, logf)
    return dict(d=d, q=q, v=v_ref[0], kk=oml * _sigmoid(-f), bc=bc, bc2=bc * LOG2E,
                qe=(q * jnp.exp(bc)).astype(BF16), last=c - 1 if d == 0 else 0)


def _hgrn_chunk(blk, cc, vt_ref, ones, seg_mask, st, o_ref):
    c = HG_CHUNK
    half = c // 2
    d = blk["d"]
    base = cc * c
    rows = slice(base, base + c)
    kk, bc = blk["kk"][rows], blk["bc"][rows]
    bl = bc[blk["last"]:blk["last"] + 1, :]
    row_half = lax.broadcasted_iota(jnp.int32, (half, LIN_W), 0)
    zero_half = jnp.zeros((half, LIN_W), F32)

    def live_halves(s):
        if d == 0:
            return (1,) if s >= half else (0, 1)
        return (0,) if s < half else (0, 1)

    parts = []
    for s in range(c):
        src = slice(base + s, base + s + 1)
        bs, ks = blk["bc2"][src], blk["kk"][src]
        for hf in (0, 1):
            if hf not in live_halves(s):
                parts.append(zero_half)
                continue
            tr = slice(base + hf * half, base + (hf + 1) * half)
            rid = row_half + hf * half
            keep = (rid >= s) if d == 0 else (rid <= s)
            parts.append(jnp.where(keep, blk["q"][tr] * jnp.exp2(blk["bc2"][tr] - bs) * ks, 0.0))
    r = _dot(jnp.concatenate(parts, axis=0).astype(BF16), ones)
    o_inter = _dot_t(blk["qe"][rows], st.astype(BF16))
    o_half = [o_inter[0:half], o_inter[half:c]]
    for s in range(c):
        vs = blk["v"][base + s:base + s + 1]
        for hf in live_halves(s):
            o_half[hf] = o_half[hf] + r[s * c + hf * half:s * c + (hf + 1) * half, :] * vs
    o_ref[0, base:base + half, :] = o_half[0]
    o_ref[0, base + half:base + c, :] = o_half[1]
    k_tail = (kk * jnp.exp(bl - bc)).astype(BF16)
    upd = _dot(vt_ref[0, :, rows].astype(BF16), k_tail)
    return st * jnp.exp(bl) + upd * seg_mask


def _hgrn_scan_kernel(layer, qf, vf, ff, vtf, qb, vb, fb, vtb, lb_ref, ones_ref, s0_ref, of_ref, ob_ref, st_ref):
    @pl.when(pl.program_id(1) == 0)
    def _():
        st_ref[...] = s0_ref[...]
    lb = _hgrn_lower_bound(layer, lb_ref)
    ones = ones_ref[...]
    seg_mask = ones.astype(F32)
    fwd = _hgrn_block(0, lb, qf, vf, ff)
    bwd = _hgrn_block(1, lb, qb, vb, fb)
    st_f = st_ref[0, 0]
    st_b = st_ref[0, 1]
    n_chunks = SCAN_BLOCK // HG_CHUNK
    for step in range(n_chunks):
        st_f = _hgrn_chunk(fwd, step, vtf, ones, seg_mask, st_f, of_ref)
        st_b = _hgrn_chunk(bwd, n_chunks - 1 - step, vtb, ones, seg_mask, st_b, ob_ref)
    st_ref[0, 0] = st_f
    st_ref[0, 1] = st_b


def _hgrn_scan(layer, oh, vt, lb, ones, s0):
    b, t, _ = oh.shape
    nb = t // SCAN_BLOCK

    def col(cb, rev):
        return (lambda bi, i: (bi, nb - 1 - i, cb)) if rev else (lambda bi, i: (bi, i, cb))

    blk = lambda im: pl.BlockSpec((1, SCAN_BLOCK, LIN_W), im)
    state = pl.BlockSpec((1, 2, LIN_W, LIN_W), lambda bi, i: (bi, 0, 0, 0))
    return pl.pallas_call(
        functools.partial(_hgrn_scan_kernel, layer),
        grid=(b, nb),
        in_specs=[blk(col(0, False)), blk(col(1, False)), blk(col(2, False)),
                  pl.BlockSpec((1, LIN_W, SCAN_BLOCK), lambda bi, i: (bi, 0, i)),
                  blk(col(0, True)), blk(col(1, True)), blk(col(3, True)),
                  pl.BlockSpec((1, LIN_W, SCAN_BLOCK), lambda bi, i: (bi, 0, nb - 1 - i)),
                  pl.BlockSpec((8, LIN_W), lambda bi, i: (0, 0)),
                  pl.BlockSpec((LIN_W, LIN_W), lambda bi, i: (0, 0)),
                  state],
        out_specs=[blk(col(0, False)), blk(col(0, True)), state],
        out_shape=[jax.ShapeDtypeStruct((b, t, LIN_W), F32),
                   jax.ShapeDtypeStruct((b, t, LIN_W), F32),
                   jax.ShapeDtypeStruct((b, 2, LIN_W, LIN_W), F32)],
        compiler_params=pltpu.CompilerParams(dimension_semantics=("arbitrary", "arbitrary"),
                                             vmem_limit_bytes=VMEM_LIMIT),
        name="hgrn_scan",
    )(oh, oh, oh, vt, oh, oh, oh, vt, lb, ones, s0)


def _post_kernel(x_ref, ogf_ref, ogb_ref, z_ref, gnw_ref, ohf_ref, ohb_ref, hg_ref, hnw_ref, om_ref, seg_ref,
                 wout_ref, gpm_ref, gt1_ref, gpf_ref, sc2_ref, sh2_ref, gt2_ref, gff_ref,
                 wa_ref, wb_ref, wo_ref, out_ref):
    seg = seg_ref[...]
    inv = 1.0 / HD
    og = ogf_ref[...] + ogb_ref[...]
    og = og * lax.rsqrt(_seg_sum(og * og, seg) * inv + EPS) * gnw_ref[...] * _silu(z_ref[...])
    oh = ohf_ref[...] + ohb_ref[...]
    oh = oh * lax.rsqrt(_seg_sum(oh * oh, seg) * inv + EPS) * hnw_ref[...] * _sigmoid(hg_ref[...])
    mix = jnp.concatenate([og.astype(BF16), oh.astype(BF16), om_ref[...].astype(BF16)], axis=-1)
    x1 = x_ref[...] + gt1_ref[0] * _rms(_dot(mix, wout_ref[...]), gpm_ref[...])
    hb = (_rms(x1, gpf_ref[...]) * (1.0 + sc2_ref[0]) + sh2_ref[0]).astype(BF16)
    fw = D_FF // FF_SPLIT
    y = None
    for part in range(FF_SPLIT):
        cs = slice(part * fw, (part + 1) * fw)
        act = (_silu(_dot(hb, wa_ref[:, cs])) * _dot(hb, wb_ref[:, cs])).astype(BF16)
        contrib = _dot(act, wo_ref[cs, :])
        y = contrib if y is None else y + contrib
    out_ref[...] = x1 + gt2_ref[0] * _rms(y, gff_ref[...])


def _post(x, ogf, ogb, og, gnw, ohf, ohb, oh, hnw, om, seg, wout, gpm, gt1, gpf, sc2, sh2, gt2, gff, wa, wb, wo, tm):
    m = x.shape[0]
    tiles = m // tm
    per = tiles // gt1.shape[0]
    row = lambda i: (i // per, 0, 0)
    tok = lambda w: pl.BlockSpec((tm, w), lambda i: (i, 0))
    vec = lambda w: pl.BlockSpec((1, w), lambda i: (0, 0))
    mod = pl.BlockSpec((1, 1, D_MODEL), row)
    once = lambda r, c: pl.BlockSpec((r, c), lambda i: (0, 0), pipeline_mode=pl.Buffered(1))
    return pl.pallas_call(
        _post_kernel,
        grid=(tiles,),
        in_specs=[tok(D_MODEL), tok(LIN_W), tok(LIN_W),
                  pl.BlockSpec((tm, LIN_W), lambda i: (i, G_W // LIN_W - 1)), vec(LIN_W),
                  tok(LIN_W), tok(LIN_W),
                  pl.BlockSpec((tm, LIN_W), lambda i: (i, H_W // LIN_W - 1)), vec(LIN_W),
                  tok(HEADS * MLA_DV), once(LIN_W, LIN_W),
                  once(D_MODEL, D_MODEL), vec(D_MODEL), mod, vec(D_MODEL), mod, mod, mod, vec(D_MODEL),
                  once(D_MODEL, D_FF), once(D_MODEL, D_FF), once(D_FF, D_MODEL)],
        out_specs=tok(D_MODEL),
        out_shape=jax.ShapeDtypeStruct((m, D_MODEL), F32),
        compiler_params=pltpu.CompilerParams(dimension_semantics=("arbitrary",), vmem_limit_bytes=VMEM_LIMIT),
        name="post",
    )(x, ogf, ogb, og, gnw, ohf, ohb, oh, hnw, om, seg, wout, gpm, gt1, gpf, sc2, sh2, gt2, gff, wa, wb, wo)


def _rope_tables(t_len, use_rope):
    scale = MLA_QK ** -0.5 * LOG2E
    if use_rope:
        rows = t_len // GRID_W
        row = jnp.repeat(jnp.arange(rows, dtype=F32), GRID_W)
        col = jnp.tile(jnp.arange(GRID_W, dtype=F32), rows)
        nf = MLA_ROPE // 4
        inv = ROPE_BASE ** (-jnp.arange(nf, dtype=F32) / nf)
        ar, ac = row[:, None] * inv, col[:, None] * inv
        cos = jnp.concatenate([jnp.cos(ar), jnp.cos(ar), jnp.cos(ac), jnp.cos(ac)], axis=-1)
        sin = jnp.concatenate([-jnp.sin(ar), jnp.sin(ar), -jnp.sin(ac), jnp.sin(ac)], axis=-1)
    else:
        cos = jnp.ones((t_len, MLA_ROPE), F32)
        sin = jnp.zeros((t_len, MLA_ROPE), F32)
    z64 = jnp.zeros((t_len, MLA_ROPE), F32)
    qa = scale * jnp.concatenate([jnp.ones((t_len, MLA_NOPE), F32), cos, z64], axis=-1)
    qb = scale * jnp.concatenate([jnp.zeros((t_len, MLA_NOPE), F32), sin, z64], axis=-1)
    ck = jnp.concatenate([cos, z64], axis=-1)
    sk = jnp.concatenate([sin, z64], axis=-1)
    return qa, qb, ck, sk


def _layer_weights(l, w_in, w_out, gdn_conv_w, gdn_a_log, gdn_dt_bias, gdn_norm_w, lower_bounds, hgrn_norm_w,
                   mla_q_norm_w, mla_w_uq, mla_kv_norm_w, mla_w_ukv, w_ffn_in, w_ffn_out,
                   g_pre_mix, g_post_mix, g_pre_ffn, g_post_ffn):
    wi = w_in[l]
    w_cat = jnp.concatenate([wi[:, _PERM_G], wi[:, _PERM_H], wi[:, _PERM_A],
                             jnp.zeros((D_MODEL, A_W - _PERM_A.shape[0]), F32)], axis=1).astype(BF16)
    convw = jnp.concatenate([gdn_conv_w[l][_PERM_CONV].T, jnp.zeros((3, G_CONV), F32)], axis=0)
    neg_a = -jnp.exp(gdn_a_log[l].astype(F32)).reshape(1, 8)
    dt = gdn_dt_bias[l].astype(F32).reshape(1, 8)
    pad = lambda r: jnp.pad(r, ((0, 0), (0, 128 - r.shape[1])))
    gconst = jnp.concatenate([pad(neg_a), pad(dt), jnp.zeros((6, 128), F32)], axis=0)
    z8 = jnp.zeros((8, GDN_BLOCK), F32)
    gconst_t = jnp.concatenate([jnp.broadcast_to(neg_a.T, (8, GDN_BLOCK)), z8,
                                jnp.broadcast_to(dt.T, (8, GDN_BLOCK)), z8], axis=0)
    vec = lambda a: a.reshape(1, -1).astype(F32)
    return dict(
        w_cat=w_cat, convw=convw, gconst=gconst, gconst_t=gconst_t,
        gnw=vec(jnp.tile(gdn_norm_w[l], HEADS)), hnw=vec(jnp.tile(hgrn_norm_w[l], HEADS)),
        layer=l, lb=lower_bounds,
        qnw=vec(mla_q_norm_w[l]), kvnw=vec(mla_kv_norm_w[l]),
        wuq=mla_w_uq[l][:, _PERM_UQ].astype(BF16), wukv=mla_w_ukv[l][:, _PERM_UKV].astype(BF16),
        wout=w_out[l].astype(BF16), wa=w_ffn_in[l][:, :D_FF].astype(BF16), wb=w_ffn_in[l][:, D_FF:].astype(BF16),
        wo=w_ffn_out[l].astype(BF16),
        g_pre_mix=vec(g_pre_mix[l]), g_post_mix=vec(g_post_mix[l]),
        g_pre_ffn=vec(g_pre_ffn[l]), g_post_ffn=vec(g_post_ffn[l]))


def _trunk_layer(x, mod, w, gdn_s0, hg_s0, tables, ctx_keys, seg512, seg256):
    b, t, _ = x.shape
    m = b * t
    tm = min(TOKEN_TILE, t)
    sh1, sc1, gt1, sh2, sc2, gt2 = mod
    xf = x.reshape(m, D_MODEL)
    og, oh, oa, vt, abt = _premix(xf, w["g_pre_mix"], sc1, sh1, w["w_cat"], tm, t)

    qa, qb, ck, sk = tables
    qcat, kcat, vcat, ckv = _mla_prep(oa, w["qnw"], w["kvnw"], w["wuq"], w["wukv"], qa, qb, ck, sk, tm, t)
    kw = HEADS * MLA_HEAD_PAD
    qcat = qcat.reshape(b, t, kw)
    kcat = kcat.reshape(b, t, kw)
    vcat = vcat.reshape(b, t, HEADS * MLA_DV)
    if ctx_keys is not None:
        kcat = jnp.concatenate([kcat, ctx_keys[0]], axis=1)
        vcat = jnp.concatenate([vcat, ctx_keys[1]], axis=1)
    o_mla = _attn(qcat, kcat, vcat, min(ATTN_Q_TILE, t)).reshape(m, HEADS * MLA_DV)

    gq, gk, gvk, gkt = _gdn_prep(og, w["convw"], seg512, tm, t)
    ogf, ogb, gdn_state = _gdn_scan(gq.reshape(b, t, LIN_W), gk.reshape(b, t, LIN_W), gvk.reshape(b, t, 2 * LIN_W),
                                    gkt, oa.reshape(b, t, A_W), abt, w["gconst"], w["gconst_t"], gdn_s0)

    ohf, ohb, hg_state = _hgrn_scan(w["layer"], oh.reshape(b, t, H_W), vt, w["lb"], seg256, hg_s0)

    x_new = _post(xf, ogf.reshape(m, LIN_W), ogb.reshape(m, LIN_W), og, w["gnw"],
                  ohf.reshape(m, LIN_W), ohb.reshape(m, LIN_W), oh, w["hnw"], o_mla, seg256,
                  w["wout"], w["g_post_mix"], gt1, w["g_pre_ffn"], sc2, sh2, gt2, w["g_post_ffn"],
                  w["wa"], w["wb"], w["wo"], tm)
    mkr = oa[:, A_KR:A_KR + MLA_ROPE]
    return x_new.reshape(b, t, D_MODEL), gdn_state, hg_state, ckv.reshape(b, t, MLA_KV_RANK), mkr.reshape(b, t, MLA_ROPE)


def _hg_state_to_block(s):
    st = jnp.swapaxes(s, -1, -2)
    eye = jnp.eye(HEADS, dtype=s.dtype)
    big = st[:, :, :, :, None, :] * eye[None, None, :, None, :, None]
    return big.reshape(s.shape[0], 2, LIN_W, LIN_W)


def _hg_block_to_state(big):
    b = big.shape[0]
    r = big.reshape(b, 2, HEADS, HD, HEADS, HD)
    diag = jnp.stack([r[:, :, h, :, h, :] for h in range(HEADS)], axis=2)
    return jnp.swapaxes(diag, -1, -2)


def kernel(x_prompt, x_sample, cache_mla_ckv, cache_mla_krope, state_gdn, state_hgrn, c, c_ctx, w_ada, b_ada,
           g_pre_mix, g_post_mix, g_pre_ffn, g_post_ffn, w_in, w_out, gdn_conv_w, gdn_a_log, gdn_dt_bias,
           gdn_norm_w, hgrn_lb, hgrn_norm_w, mla_q_norm_w, mla_w_uq, mla_kv_norm_w, mla_w_ukv, w_ffn_in, w_ffn_out):
    bp, tp, _ = x_prompt.shape
    bd, td, _ = x_sample.shape
    past = cache_mla_ckv.shape[2]
    assert 1 + bd <= MOD_ROWS and tp % GDN_BLOCK == 0 and td % GDN_BLOCK == 0

    lower_bounds = jnp.pad(hgrn_lb.astype(F32), ((0, 8 - DEPTH), (0, 0)))

    cond = jnp.concatenate([c_ctx[None, :], c, jnp.zeros((MOD_ROWS - 1 - bd, D_MODEL), F32)], axis=0)
    mod_all = _ada(cond, w_ada, b_ada)

    seg512 = _seg_ones(2 * LIN_W)
    seg256 = _seg_ones(LIN_W)
    tab_ctx = _rope_tables(tp, False)
    tab_lat = _rope_tables(td, True)
    ones_k = jnp.concatenate([jnp.ones((past, MLA_ROPE), F32), jnp.zeros((past, MLA_ROPE), F32)], axis=-1)
    zeros_k = jnp.zeros((past, 128), F32)

    weights = [_layer_weights(l, w_in, w_out, gdn_conv_w, gdn_a_log, gdn_dt_bias, gdn_norm_w, lower_bounds,
                              hgrn_norm_w, mla_q_norm_w, mla_w_uq, mla_kv_norm_w, mla_w_ukv, w_ffn_in, w_ffn_out,
                              g_pre_mix, g_post_mix, g_pre_ffn, g_post_ffn) for l in range(DEPTH)]

    def mods(l, lo, n):
        rows = mod_all[l, lo:lo + n].reshape(n, 1, 6 * D_MODEL)
        return tuple(rows[:, :, i * D_MODEL:(i + 1) * D_MODEL] for i in range(6))

    xp = x_prompt
    zero_g = jnp.zeros((bp, 2 * HEADS, HD, HD), F32)
    zero_h = jnp.zeros((bp, 2, LIN_W, LIN_W), F32)
    ckv_l, kr_l, gs_l, hs_l = [], [], [], []
    for l in range(DEPTH):
        xp, gs, hs, ckv, mkr = _trunk_layer(xp, mods(l, 0, 1), weights[l], zero_g, zero_h, tab_ctx, None,
                                            seg512, seg256)
        ckv_l.append(ckv)
        kr_l.append(mkr)
        gs_l.append(gs.reshape(bp, 2, HEADS, HD, HD))
        hs_l.append(_hg_block_to_state(hs))

    xs = x_sample
    for l in range(DEPTH):
        kr128 = jnp.pad(cache_mla_krope[:, l].reshape(bd * past, MLA_ROPE), ((0, 0), (0, 128 - MLA_ROPE)))
        k_ctx, v_ctx = _kv_up(cache_mla_ckv[:, l].reshape(bd * past, MLA_KV_RANK), kr128, ones_k, zeros_k,
                              weights[l]["wukv"], past)
        ctx_keys = (k_ctx.reshape(bd, past, HEADS * MLA_HEAD_PAD), v_ctx.reshape(bd, past, HEADS * MLA_DV))
        xs, _, _, _, _ = _trunk_layer(xs, mods(l, 1, bd), weights[l],
                                      state_gdn[:, l].reshape(bd, 2 * HEADS, HD, HD),
                                      _hg_state_to_block(state_hgrn[:, l]), tab_lat, ctx_keys, seg512, seg256)

    return (xp, xs, jnp.stack(ckv_l, axis=1), jnp.stack(kr_l, axis=1),
            jnp.stack(gs_l, axis=1), jnp.stack(hs_l, axis=1))
```

```python
import functools
import math

import numpy as np
import jax
import jax.numpy as jnp
from jax import lax
from jax.experimental import pallas as pl
from jax.experimental.pallas import tpu as pltpu

F32 = jnp.float32
BF16 = jnp.bfloat16

D_MODEL = 1024
DEPTH = 2
GRID_W = 64
EPS = 1e-6
GATE_FLOOR = 1e-30
HEADS = 4
HD = 64
LIN_W = HEADS * HD
GDN_CHUNK = 64
HG_CHUNK = 16
SCAN_BLOCK = 256
LOG2E = 1.4426950408889634
MLA_Q_RANK = 384
MLA_KV_RANK = 256
MLA_NOPE = 128
MLA_ROPE = 64
MLA_DV = 128
MLA_QK = MLA_NOPE + MLA_ROPE
MLA_HEAD_PAD = 256
ROPE_BASE = 10000.0
D_FF = -(-8 * D_MODEL // (3 * 256)) * 256
FF_SPLIT = 2
TOKEN_TILE = 512
ATTN_Q_TILE = 256
MOD_ROWS = 16
VMEM_LIMIT = 56 * 1024 * 1024

_OFF = {}
_o = 0
for _n, _s in (("gq", 256), ("gk", 256), ("gv", 256), ("gz", 256), ("ga", 8), ("gb", 8),
               ("hq", 256), ("hi", 256), ("hf", 512), ("hg", 256),
               ("mcq", MLA_Q_RANK), ("mckv", MLA_KV_RANK), ("mkr", MLA_ROPE)):
    _OFF[_n] = _o
    _o += _s
IN_DIM = _o

G_W = 1280
G_CONV = 1024
H_W = 1280
A_W = 896
A_KR = MLA_Q_RANK + MLA_KV_RANK
A_AB = A_KR + 2 * MLA_ROPE


def _rope_swap_idx():
    j = np.arange(MLA_ROPE)
    return np.where((j % 32) < 16, j + 16, j - 16)


def _in_perm():
    ar = np.arange
    vk = np.concatenate([np.concatenate([_OFF["gv"] + 64 * h + ar(64), _OFF["gk"] + 64 * h + ar(64)])
                         for h in range(HEADS)])
    g = np.concatenate([_OFF["gq"] + ar(256), _OFF["gk"] + ar(256), vk, _OFF["gz"] + ar(256)])
    hh = np.concatenate([_OFF["hq"] + ar(256), _OFF["hi"] + ar(256), _OFF["hf"] + ar(512), _OFF["hg"] + ar(256)])
    a = np.concatenate([_OFF["mcq"] + ar(MLA_Q_RANK), _OFF["mckv"] + ar(MLA_KV_RANK), _OFF["mkr"] + ar(MLA_ROPE),
                        _OFF["mkr"] + _rope_swap_idx(), _OFF["ga"] + ar(8), _OFF["gb"] + ar(8)])
    conv = np.concatenate([ar(256), 256 + ar(256),
                           np.concatenate([np.concatenate([512 + 64 * h + ar(64), 256 + 64 * h + ar(64)])
                                           for h in range(HEADS)])])
    return g, hh, a, conv


_PERM_G, _PERM_H, _PERM_A, _PERM_CONV = _in_perm()


def _uq_perm():
    sw = _rope_swap_idx()
    cols = []
    for h in range(HEADS):
        base = h * MLA_QK
        cols += [base + np.arange(MLA_NOPE), base + MLA_NOPE + np.arange(MLA_ROPE), base + MLA_NOPE + sw]
    return np.concatenate(cols)


def _ukv_perm():
    per = MLA_NOPE + MLA_DV
    kn = np.concatenate([h * per + np.arange(MLA_NOPE) for h in range(HEADS)])
    vv = np.concatenate([h * per + MLA_NOPE + np.arange(MLA_DV) for h in range(HEADS)])
    return np.concatenate([kn, vv])


_PERM_UQ = _uq_perm()
_PERM_UKV = _ukv_perm()


def _seg_ones(n):
    i = np.arange(n) // HD
    return jnp.asarray((i[:, None] == i[None, :]).astype(np.float32), dtype=BF16)


def _rms(x, w):
    return x * lax.rsqrt(jnp.mean(x * x, axis=-1, keepdims=True) + EPS) * w


def _dot(a, b):
    return jnp.dot(a, b, preferred_element_type=F32)


def _dot_t(a, b):
    return lax.dot_general(a, b, (((1,), (1,)), ((), ())), preferred_element_type=F32)


def _split3(x):
    hi = x.astype(BF16)
    r1 = x - hi.astype(F32)
    mid = r1.astype(BF16)
    lo = (r1 - mid.astype(F32)).astype(BF16)
    return hi, mid, lo


def _split2(x):
    hi = x.astype(BF16)
    return hi, (x - hi.astype(F32)).astype(BF16)


def _seg_sum(x, seg):
    hi, lo = _split2(x)
    return _dot(hi, seg) + _dot(lo, seg)


def _mask_dot(mask01, x):
    hi, mid, lo = _split3(x)
    return _dot(mask01, hi) + _dot(mask01, mid) + _dot(mask01, lo)


def _dot_mask_t(x, mask01):
    hi, mid, lo = _split3(x)
    return _dot_t(hi, mask01) + _dot_t(mid, mask01) + _dot_t(lo, mask01)


def _softplus(x):
    return jnp.maximum(x, 0.0) + jnp.log1p(jnp.exp(-jnp.abs(x)))


def _sigmoid(x):
    return jax.nn.sigmoid(x)


def _silu(x):
    return x * jax.nn.sigmoid(x)


def _ada_kernel(c_ref, w_ref, b_ref, o_ref):
    s = _silu(c_ref[...]).astype(BF16)
    o_ref[0] = _dot(s, w_ref[0].astype(BF16)) + b_ref[0]


def _ada(cond, w_ada, b_ada):
    n = w_ada.shape[-1]
    tn = 1536
    return pl.pallas_call(
        _ada_kernel,
        grid=(DEPTH, n // tn),
        in_specs=[pl.BlockSpec((MOD_ROWS, D_MODEL), lambda l, j: (0, 0)),
                  pl.BlockSpec((1, D_MODEL, tn), lambda l, j: (l, 0, j)),
                  pl.BlockSpec((1, 1, tn), lambda l, j: (l, 0, j))],
        out_specs=pl.BlockSpec((1, MOD_ROWS, tn), lambda l, j: (l, 0, j)),
        out_shape=jax.ShapeDtypeStruct((DEPTH, MOD_ROWS, n), F32),
        compiler_params=pltpu.CompilerParams(dimension_semantics=("arbitrary", "arbitrary"),
                                             vmem_limit_bytes=VMEM_LIMIT),
        name="ada",
    )(cond, w_ada, b_ada.reshape(DEPTH, 1, n))


def _premix_kernel(x_ref, g_ref, sc_ref, sh_ref, w_ref, og_ref, oh_ref, oa_ref, vt_ref):
    h = _rms(x_ref[...], g_ref[...]) * (1.0 + sc_ref[0]) + sh_ref[0]
    hb = h.astype(BF16)
    og_ref[...] = _dot(hb, w_ref[:, 0:G_W])
    oh = _dot(hb, w_ref[:, G_W:G_W + H_W])
    oh_ref[...] = oh
    oa_ref[...] = _dot(hb, w_ref[:, G_W + H_W:G_W + H_W + A_W])
    vt_ref[0] = oh[:, LIN_W:2 * LIN_W].T


def _premix(x, g, sc, sh, w, tm, t_len):
    m = x.shape[0]
    tiles = m // tm
    per = tiles // sc.shape[0]
    row = lambda i: (i // per, 0, 0)
    tps = t_len // tm
    seq = lambda i: (i // tps, 0, i % tps)
    wt = G_W + H_W + A_W
    return pl.pallas_call(
        _premix_kernel,
        grid=(tiles,),
        in_specs=[pl.BlockSpec((tm, D_MODEL), lambda i: (i, 0)),
                  pl.BlockSpec((1, D_MODEL), lambda i: (0, 0)),
                  pl.BlockSpec((1, 1, D_MODEL), row),
                  pl.BlockSpec((1, 1, D_MODEL), row),
                  pl.BlockSpec((D_MODEL, wt), lambda i: (0, 0))],
        out_specs=[pl.BlockSpec((tm, G_W), lambda i: (i, 0)),
                   pl.BlockSpec((tm, H_W), lambda i: (i, 0)),
                   pl.BlockSpec((tm, A_W), lambda i: (i, 0)),
                   pl.BlockSpec((1, LIN_W, tm), seq)],
        out_shape=[jax.ShapeDtypeStruct((m, G_W), F32),
                   jax.ShapeDtypeStruct((m, H_W), F32),
                   jax.ShapeDtypeStruct((m, A_W), F32),
                   jax.ShapeDtypeStruct((m // t_len, LIN_W, t_len), F32)],
        compiler_params=pltpu.CompilerParams(dimension_semantics=("arbitrary",), vmem_limit_bytes=VMEM_LIMIT),
        name="premix",
    )(x, g, sc, sh, w)


def _keys_from_latent(ckv, kr128, ck, sk, wukv_ref, k_out, v_out):
    kv = _dot(ckv.astype(BF16), wukv_ref[...])
    krot = (kr128 * ck + pltpu.roll(kr128, 64, 1) * sk).astype(BF16)
    for h in range(HEADS):
        k_out[:, MLA_HEAD_PAD * h:MLA_HEAD_PAD * h + MLA_NOPE] = kv[:, MLA_NOPE * h:MLA_NOPE * (h + 1)].astype(BF16)
        k_out[:, MLA_HEAD_PAD * h + MLA_NOPE:MLA_HEAD_PAD * (h + 1)] = krot
    v_out[...] = kv[:, HEADS * MLA_NOPE:].astype(BF16)


def _mla_prep_kernel(a_ref, qnw_ref, kvnw_ref, wuq_ref, wukv_ref, qa_ref, qb_ref, ck_ref, sk_ref,
                     q_out, k_out, v_out, ckv_out):
    qn = _rms(a_ref[:, 0:MLA_Q_RANK], qnw_ref[...])
    y = _dot(qn.astype(BF16), wuq_ref[...])
    z = pltpu.roll(y, HEADS * MLA_HEAD_PAD - MLA_ROPE, 1)
    qa = qa_ref[...]
    qb = qb_ref[...]
    for h in range(HEADS):
        sl = slice(MLA_HEAD_PAD * h, MLA_HEAD_PAD * (h + 1))
        q_out[:, sl] = (y[:, sl] * qa + z[:, sl] * qb).astype(BF16)
    ckv = _rms(a_ref[:, MLA_Q_RANK:A_KR], kvnw_ref[...])
    ckv_out[...] = ckv
    _keys_from_latent(ckv, a_ref[:, A_KR:A_AB], ck_ref[...], sk_ref[...], wukv_ref, k_out, v_out)


def _mla_prep(oa, qnw, kvnw, wuq, wukv, qa, qb, ck, sk, tm, t_len):
    m = oa.shape[0]
    tiles = m // tm
    tps = t_len // tm
    pos = lambda i: (i % tps, 0)
    full = lambda i: (0, 0)
    kw = HEADS * MLA_HEAD_PAD
    return pl.pallas_call(
        _mla_prep_kernel,
        grid=(tiles,),
        in_specs=[pl.BlockSpec((tm, A_W), lambda i: (i, 0)),
                  pl.BlockSpec((1, MLA_Q_RANK), full),
                  pl.BlockSpec((1, MLA_KV_RANK), full),
                  pl.BlockSpec((MLA_Q_RANK, kw), full),
                  pl.BlockSpec((MLA_KV_RANK, HEADS * (MLA_NOPE + MLA_DV)), full),
                  pl.BlockSpec((tm, MLA_HEAD_PAD), pos),
                  pl.BlockSpec((tm, MLA_HEAD_PAD), pos),
                  pl.BlockSpec((tm, 128), pos),
                  pl.BlockSpec((tm, 128), pos)],
        out_specs=[pl.BlockSpec((tm, kw), lambda i: (i, 0)),
                   pl.BlockSpec((tm, kw), lambda i: (i, 0)),
                   pl.BlockSpec((tm, HEADS * MLA_DV), lambda i: (i, 0)),
                   pl.BlockSpec((tm, MLA_KV_RANK), lambda i: (i, 0))],
        out_shape=[jax.ShapeDtypeStruct((m, kw), BF16),
                   jax.ShapeDtypeStruct((m, kw), BF16),
                   jax.ShapeDtypeStruct((m, HEADS * MLA_DV), BF16),
                   jax.ShapeDtypeStruct((m, MLA_KV_RANK), F32)],
        compiler_params=pltpu.CompilerParams(dimension_semantics=("arbitrary",), vmem_limit_bytes=VMEM_LIMIT),
        name="mla_prep",
    )(oa, qnw, kvnw, wuq, wukv, qa, qb, ck, sk)


def _kv_up_kernel(ckv_ref, kr_ref, ck_ref, sk_ref, wukv_ref, k_out, v_out):
    _keys_from_latent(ckv_ref[...], kr_ref[...], ck_ref[...], sk_ref[...], wukv_ref, k_out, v_out)


def _kv_up(ckv, kr128, ck, sk, wukv, tm):
    m = ckv.shape[0]
    kw = HEADS * MLA_HEAD_PAD
    return pl.pallas_call(
        _kv_up_kernel,
        grid=(m // tm,),
        in_specs=[pl.BlockSpec((tm, MLA_KV_RANK), lambda i: (i, 0)),
                  pl.BlockSpec((tm, 128), lambda i: (i, 0)),
                  pl.BlockSpec((tm, 128), lambda i: (0, 0)),
                  pl.BlockSpec((tm, 128), lambda i: (0, 0)),
                  pl.BlockSpec((MLA_KV_RANK, HEADS * (MLA_NOPE + MLA_DV)), lambda i: (0, 0))],
        out_specs=[pl.BlockSpec((tm, kw), lambda i: (i, 0)),
                   pl.BlockSpec((tm, HEADS * MLA_DV), lambda i: (i, 0))],
        out_shape=[jax.ShapeDtypeStruct((m, kw), BF16),
                   jax.ShapeDtypeStruct((m, HEADS * MLA_DV), BF16)],
        compiler_params=pltpu.CompilerParams(dimension_semantics=("arbitrary",), vmem_limit_bytes=VMEM_LIMIT),
        name="kv_up",
    )(ckv, kr128, ck, sk, wukv)


def _attn_kernel(q_ref, k_ref, v_ref, o_ref):
    for h in range(HEADS):
        sl = slice(MLA_HEAD_PAD * h, MLA_HEAD_PAD * (h + 1))
        s = _dot_t(q_ref[0, :, sl], k_ref[0, :, sl])
        p = jnp.exp2(s - jnp.max(s, axis=-1, keepdims=True))
        l = jnp.sum(p, axis=-1, keepdims=True)
        o = _dot(p.astype(BF16), v_ref[0, :, MLA_DV * h:MLA_DV * (h + 1)])
        o_ref[0, :, MLA_DV * h:MLA_DV * (h + 1)] = o / l


def _attn(q, k, v, tq):
    b, t, kw = q.shape
    s = k.shape[1]
    return pl.pallas_call(
        _attn_kernel,
        grid=(b, t // tq),
        in_specs=[pl.BlockSpec((1, tq, kw), lambda bi, i: (bi, i, 0)),
                  pl.BlockSpec((1, s, kw), lambda bi, i: (bi, 0, 0)),
                  pl.BlockSpec((1, s, HEADS * MLA_DV), lambda bi, i: (bi, 0, 0))],
        out_specs=pl.BlockSpec((1, tq, HEADS * MLA_DV), lambda bi, i: (bi, i, 0)),
        out_shape=jax.ShapeDtypeStruct((b, t, HEADS * MLA_DV), F32),
        compiler_params=pltpu.CompilerParams(dimension_semantics=("arbitrary", "arbitrary"),
                                             vmem_limit_bytes=VMEM_LIMIT),
        name="attn",
    )(q, k, v)


def _gdn_prep_kernel(tps, cur_ref, prev_ref, next_ref, w_ref, seg_ref, q_out, k_out, vk_out, kt_out):
    i = pl.program_id(0)
    tm = cur_ref.shape[0]
    has_prev = ((i % tps) != 0).astype(F32)
    has_next = ((i % tps) != (tps - 1)).astype(F32)
    xc = jnp.concatenate([prev_ref[...] * has_prev, cur_ref[...], next_ref[...] * has_next], axis=0)
    n = tm + 16
    y = xc[8:8 + tm] * w_ref[2:3, :]
    for j in (0, 1, 3, 4):
        d = j - 2
        y = y + pltpu.roll(xc, (n - d) % n, 0)[8:8 + tm] * w_ref[j:j + 1, :]
    y = _silu(y)
    seg = seg_ref[...]
    qk = y[:, 0:2 * LIN_W]
    qk = qk * lax.rsqrt(_seg_sum(qk * qk, seg) + EPS)
    q_out[...] = qk[:, 0:LIN_W] * (HD ** -0.5)
    k_out[...] = qk[:, LIN_W:2 * LIN_W]
    kt_out[0] = qk[:, LIN_W:2 * LIN_W].T
    vk = y[:, 2 * LIN_W:]
    is_k = (lax.broadcasted_iota(jnp.int32, vk.shape, 1) & HD) != 0
    vk_out[...] = jnp.where(is_k, vk * lax.rsqrt(_seg_sum(vk * vk, seg) + EPS), vk)


def _gdn_prep(og, convw, seg, tm, t_len):
    m = og.shape[0]
    tiles = m // tm
    tps = t_len // tm
    r8 = tm // 8
    last8 = m // 8 - 1
    return pl.pallas_call(
        functools.partial(_gdn_prep_kernel, tps),
        grid=(tiles,),
        in_specs=[pl.BlockSpec((tm, G_CONV), lambda i: (i, 0)),
                  pl.BlockSpec((8, G_CONV), lambda i: (jnp.maximum(i * r8 - 1, 0), 0)),
                  pl.BlockSpec((8, G_CONV), lambda i: (jnp.minimum((i + 1) * r8, last8), 0)),
                  pl.BlockSpec((8, G_CONV), lambda i: (0, 0)),
                  pl.BlockSpec((2 * LIN_W, 2 * LIN_W), lambda i: (0, 0))],
        out_specs=[pl.BlockSpec((tm, LIN_W), lambda i: (i, 0)),
                   pl.BlockSpec((tm, LIN_W), lambda i: (i, 0)),
                   pl.BlockSpec((tm, 2 * LIN_W), lambda i: (i, 0)),
                   pl.BlockSpec((1, LIN_W, tm), lambda i: (i // tps, 0, i % tps))],
        out_shape=[jax.ShapeDtypeStruct((m, LIN_W), F32),
                   jax.ShapeDtypeStruct((m, LIN_W), F32),
                   jax.ShapeDtypeStruct((m, 2 * LIN_W), F32),
                   jax.ShapeDtypeStruct((m // t_len, LIN_W, t_len), F32)],
        compiler_params=pltpu.CompilerParams(dimension_semantics=("arbitrary",), vmem_limit_bytes=VMEM_LIMIT),
        name="gdn_prep",
    )(og, og, og, convw, seg)


def _gdn_units(d, q_ref, k_ref, vk_ref, kt_ref, ab_ref, gc_ref, o_ref):
    c = GDN_CHUNK
    ri = lax.broadcasted_iota(jnp.int32, (c, c), 0)
    ci = lax.broadcasted_iota(jnp.int32, (c, c), 1)
    incl = (ri >= ci) if d == 0 else (ri <= ci)
    strict = (ri > ci) if d == 0 else (ri < ci)
    rb = lax.broadcasted_iota(jnp.int32, (SCAN_BLOCK, SCAN_BLOCK), 0)
    cb = lax.broadcasted_iota(jnp.int32, (SCAN_BLOCK, SCAN_BLOCK), 1)
    same_chunk = jnp.right_shift(rb, 6) == jnp.right_shift(cb, 6)
    incl_blk = (same_chunk & ((rb >= cb) if d == 0 else (rb <= cb))).astype(F32).astype(BF16)
    last = c - 1 if d == 0 else 0
    ab = ab_ref[0]
    g_cols = gc_ref[0:1, :] * _softplus(ab + gc_ref[1:2, :])
    beta = _sigmoid(ab)
    gcs_blk = _mask_dot(incl_blk, g_cols)
    gcr_blk = gcs_blk.T
    lane_is_v = lax.broadcasted_iota(jnp.int32, (c, 2 * HD), 1) < HD
    n_chunks = SCAN_BLOCK // c
    units = []
    for rank, cc in enumerate(range(n_chunks) if d == 0 else range(n_chunks - 1, -1, -1)):
        rows = slice(cc * c, (cc + 1) * c)
        for h in range(HEADS):
            j = d * HEADS + h
            hs = slice(HD * h, HD * (h + 1))
            gcol = gcs_blk[rows, j:j + 1]
            grow = gcr_blk[j:j + 1, rows]
            bcol = beta[rows, 8 + j:9 + j]
            gl = gcol[last:last + 1, :]
            eg = jnp.exp(gcol)
            vk = vk_ref[0, rows, 2 * HD * h:2 * HD * (h + 1)]
            units.append(dict(
                rank=rank, j=j, rows=rows, hs=hs, o_ref=o_ref, strict=strict, gl=gl, bcol=bcol, eg=eg,
                q=q_ref[0, rows, hs], k=k_ref[0, rows, hs],
                k_tail=(kt_ref[0, hs, rows] * jnp.exp(gl - grow)).astype(BF16),
                decay=jnp.where(incl, jnp.exp(jnp.where(incl, gcol - grow, 0.0)), 0.0),
                x=vk * jnp.where(lane_is_v, bcol, bcol * eg)))
    return units


def _gdn_stages(units, s_ref):
    c = GDN_CHUNK
    ri = lax.broadcasted_iota(jnp.int32, (c, c), 0)
    ci = lax.broadcasted_iota(jnp.int32, (c, c), 1)
    diag_blk = jnp.right_shift(ri, 4) == jnp.right_shift(ci, 4)
    eye = (ri == ci).astype(F32)

    def gram():
        for u in units:
            kb16 = u["k"].astype(BF16)
            u["kk"] = _dot_t((u["k"] * u["bcol"]).astype(BF16), kb16)
            u["qk"] = _dot_t(u["q"].astype(BF16), kb16)

    def split():
        for u in units:
            p = jnp.where(u["strict"], u["kk"] * u["decay"], 0.0)
            pd = jnp.where(diag_blk, p, 0.0)
            u["att"] = (u["qk"] * u["decay"]).astype(BF16)
            u["poff"] = (p - pd).astype(BF16)
            u["td"] = eye - pd
            u["a"] = pd.astype(BF16)

    def square():
        for u in units:
            u["a"] = _dot(u["a"], u["a"]).astype(BF16)

    def extend():
        for u in units:
            u["td"] = u["td"] + _dot(u["td"].astype(BF16), u["a"])

    def apply_diag():
        for u in units:
            yn = _dot(u["td"].astype(BF16), jnp.concatenate([u["x"].astype(BF16), u["poff"]], axis=1))
            u["y"] = yn[:, 0:2 * HD]
            u["n"] = yn[:, 2 * HD:3 * HD].astype(BF16)
            u["x"] = u["y"]

    def substitute():
        for u in units:
            u["x"] = u["y"] - _dot(u["n"], u["x"].astype(BF16))

    def new_values(rank):
        for u in units:
            if u["rank"] == rank:
                u["s"] = s_ref[0, u["j"]]
                u["sb"] = u["s"].astype(BF16)
                u["vb"] = (u["x"][:, 0:HD] - _dot(u["x"][:, HD:2 * HD].astype(BF16), u["sb"])).astype(BF16)

    def emit(rank):
        for u in units:
            if u["rank"] == rank:
                u["o_ref"][0, u["rows"], u["hs"]] = (_dot((u["q"] * u["eg"]).astype(BF16), u["sb"])
                                                     + _dot(u["att"], u["vb"]))
                s_ref[0, u["j"]] = u["s"] * jnp.exp(u["gl"]) + _dot(u["k_tail"], u["vb"])

    stages = [gram, split] + [square, extend] * 3 + [apply_diag] + [substitute] * 3
    for rank in range(SCAN_BLOCK // c):
        stages += [functools.partial(new_values, rank), functools.partial(emit, rank)]
    return stages


def _hgrn_lower_bound(layer, lb_ref):
    raw = lb_ref[0:DEPTH, :]
    e = jnp.exp(raw - jnp.max(raw, axis=0, keepdims=True))
    gamma = e / jnp.sum(e, axis=0, keepdims=True)
    lb = jnp.zeros((1, LIN_W), F32)
    for i in range(1, layer + 1):
        lb = lb + gamma[i:i + 1, :]
    return lb


def _hgrn_block(d, lb, q_ref, v_ref, f_ref):
    c = HG_CHUNK
    ri = lax.broadcasted_iota(jnp.int32, (SCAN_BLOCK, SCAN_BLOCK), 0)
    ci = lax.broadcasted_iota(jnp.int32, (SCAN_BLOCK, SCAN_BLOCK), 1)
    same_chunk = jnp.right_shift(ri, 4) == jnp.right_shift(ci, 4)
    incl01 = (same_chunk & ((ri >= ci) if d == 0 else (ri <= ci))).astype(F32).astype(BF16)
    oml = 1.0 - lb
    q = q_ref[0]
    f = f_ref[0]
    logf = jnp.log(jnp.maximum(lb + oml * _sigmoid(f), GATE_FLOOR))
    bc = _mask_dot(incl01, logf)
    return dict(d=d, q=q, v=v_ref[0], kk=oml * _sigmoid(-f), bc=bc, bc2=bc * LOG2E,
                qe=(q * jnp.exp(bc)).astype(BF16), last=c - 1 if d == 0 else 0)


def _hgrn_live_halves(d, s):
    half = HG_CHUNK // 2
    if d == 0:
        return (1,) if s >= half else (0, 1)
    return (0,) if s < half else (0, 1)


def _hgrn_chunk_build(blk, cc):
    c = HG_CHUNK
    half = c // 2
    d = blk["d"]
    base = cc * c
    rows = slice(base, base + c)
    kk, bc = blk["kk"][rows], blk["bc"][rows]
    bl = bc[blk["last"]:blk["last"] + 1, :]
    row_half = lax.broadcasted_iota(jnp.int32, (half, LIN_W), 0)
    zero_half = jnp.zeros((half, LIN_W), F32)
    parts = []
    for s in range(c):
        src = slice(base + s, base + s + 1)
        bs, ks = blk["bc2"][src], blk["kk"][src]
        for hf in (0, 1):
            if hf not in _hgrn_live_halves(d, s):
                parts.append(zero_half)
                continue
            tr = slice(base + hf * half, base + (hf + 1) * half)
            rid = row_half + hf * half
            keep = (rid >= s) if d == 0 else (rid <= s)
            parts.append(jnp.where(keep, blk["q"][tr] * jnp.exp2(blk["bc2"][tr] - bs) * ks, 0.0))
    return dict(cols=jnp.concatenate(parts, axis=0).astype(BF16), k_tail=(kk * jnp.exp(bl - bc)).astype(BF16),
                decay=jnp.exp(bl))


def _hgrn_chunk_launch(blk, cc, built, vt_ref, ones, st):
    rows = slice(cc * HG_CHUNK, (cc + 1) * HG_CHUNK)
    return dict(
        r=_dot(built["cols"], ones),
        o_inter=_dot_t(blk["qe"][rows], st.astype(BF16)),
        upd=_dot(vt_ref[0, :, rows].astype(BF16), built["k_tail"]),
        decay=built["decay"])


def _hgrn_chunk_finish(blk, cc, pend, seg_mask, st, o_ref):
    c = HG_CHUNK
    half = c // 2
    base = cc * c
    r = pend["r"]
    o_half = [pend["o_inter"][0:half], pend["o_inter"][half:c]]
    for s in range(c):
        vs = blk["v"][base + s:base + s + 1]
        for hf in _hgrn_live_halves(blk["d"], s):
            o_half[hf] = o_half[hf] + r[s * c + hf * half:s * c + (hf + 1) * half, :] * vs
    o_ref[0, base:base + half, :] = o_half[0]
    o_ref[0, base + half:base + c, :] = o_half[1]
    return st * pend["decay"] + pend["upd"] * seg_mask


def _hgrn_steps(lb, fwd_refs, bwd_refs, ones, st_ref):
    seg_mask = ones.astype(F32)
    (qf, vf, ff, vtf, of_ref), (qb, vb, fb, vtb, ob_ref) = fwd_refs, bwd_refs
    fwd = _hgrn_block(0, lb, qf, vf, ff)
    bwd = _hgrn_block(1, lb, qb, vb, fb)
    carry = {0: st_ref[0, 0], 1: st_ref[0, 1]}
    built, pending = {}, {}
    n_chunks = SCAN_BLOCK // HG_CHUNK
    chunk_of = {0: lambda step: step, 1: lambda step: n_chunks - 1 - step}
    side = {0: (fwd, vtf, of_ref), 1: (bwd, vtb, ob_ref)}

    def build(d, step):
        built[(d, step)] = _hgrn_chunk_build(side[d][0], chunk_of[d](step))

    def launch(d, step):
        blk, vt_ref, _ = side[d]
        pending[(d, step)] = _hgrn_chunk_launch(blk, chunk_of[d](step), built.pop((d, step)), vt_ref, ones, carry[d])

    def finish(d, step):
        blk, _, o_ref = side[d]
        carry[d] = _hgrn_chunk_finish(blk, chunk_of[d](step), pending.pop((d, step)), seg_mask, carry[d], o_ref)

    def flush():
        st_ref[0, 0] = carry[0]
        st_ref[0, 1] = carry[1]

    groups = []
    for slot in range(n_chunks + 2):
        group = []
        for phase, step in ((finish, slot - 2), (launch, slot - 1), (build, slot)):
            if 0 <= step < n_chunks:
                group += [functools.partial(phase, 0, step), functools.partial(phase, 1, step)]
        groups.append(group)
    return groups, flush


def _scan_kernel(layer,
                 gqf, gkf, gvkf, gktf, gabf, gqb, gkb, gvkb, gktb, gabb, gc_ref, gs0_ref,
                 hqf, hvf, hff, hvtf, hqb, hvb, hfb, hvtb, lb_ref, ones_ref, hs0_ref,
                 gof_ref, gob_ref, gs_ref, hof_ref, hob_ref, hst_ref):
    @pl.when(pl.program_id(1) == 0)
    def _():
        gs_ref[...] = gs0_ref[...]
        hst_ref[...] = hs0_ref[...]

    units = (_gdn_units(0, gqf, gkf, gvkf, gktf, gabf, gc_ref, gof_ref)
             + _gdn_units(1, gqb, gkb, gvkb, gktb, gabb, gc_ref, gob_ref))
    stages = _gdn_stages(units, gs_ref)
    groups, flush = _hgrn_steps(_hgrn_lower_bound(layer, lb_ref), (hqf, hvf, hff, hvtf, hof_ref),
                                (hqb, hvb, hfb, hvtb, hob_ref), ones_ref[...], hst_ref)
    per_stage = -(-len(groups) // len(stages))
    for stage in stages:
        stage()
        for group in groups[:per_stage]:
            for step in group:
                step()
        groups = groups[per_stage:]
    for group in groups:
        for step in group:
            step()
    flush()


def _lin_scan(layer, gq, gk, gvk, gkt, oa, gconst, gs0, oh, vt, lb, ones, hs0):
    b, t, _ = gq.shape
    blk = SCAN_BLOCK
    nb = t // blk

    def tok(cb, rev):
        return (lambda bi, i: (bi, nb - 1 - i, cb)) if rev else (lambda bi, i: (bi, i, cb))

    def lanes(rev):
        return (lambda bi, i: (bi, 0, nb - 1 - i)) if rev else (lambda bi, i: (bi, 0, i))

    def gdn_specs(rev):
        return [pl.BlockSpec((1, blk, LIN_W), tok(0, rev)),
                pl.BlockSpec((1, blk, LIN_W), tok(0, rev)),
                pl.BlockSpec((1, blk, 2 * LIN_W), tok(0, rev)),
                pl.BlockSpec((1, LIN_W, blk), lanes(rev)),
                pl.BlockSpec((1, blk, 128), tok(A_AB // 128, rev))]

    def hgrn_specs(rev):
        f_col = 3 if rev else 2
        return [pl.BlockSpec((1, blk, LIN_W), tok(0, rev)),
                pl.BlockSpec((1, blk, LIN_W), tok(1, rev)),
                pl.BlockSpec((1, blk, LIN_W), tok(f_col, rev)),
                pl.BlockSpec((1, LIN_W, blk), lanes(rev))]

    const = lambda r, c: pl.BlockSpec((r, c), lambda bi, i: (0, 0))
    g_state = pl.BlockSpec((1, 2 * HEADS, HD, HD), lambda bi, i: (bi, 0, 0, 0))
    h_state = pl.BlockSpec((1, 2, LIN_W, LIN_W), lambda bi, i: (bi, 0, 0, 0))
    out_tok = lambda rev: pl.BlockSpec((1, blk, LIN_W), tok(0, rev))
    seq = jax.ShapeDtypeStruct((b, t, LIN_W), F32)
    return pl.pallas_call(
        functools.partial(_scan_kernel, layer),
        grid=(b, nb),
        in_specs=gdn_specs(False) + gdn_specs(True) + [const(8, 128), g_state]
        + hgrn_specs(False) + hgrn_specs(True) + [const(8, LIN_W), const(LIN_W, LIN_W), h_state],
        out_specs=[out_tok(False), out_tok(True), g_state, out_tok(False), out_tok(True), h_state],
        out_shape=[seq, seq, jax.ShapeDtypeStruct((b, 2 * HEADS, HD, HD), F32),
                   seq, seq, jax.ShapeDtypeStruct((b, 2, LIN_W, LIN_W), F32)],
        compiler_params=pltpu.CompilerParams(dimension_semantics=("arbitrary", "arbitrary"),
                                             vmem_limit_bytes=VMEM_LIMIT),
        name="lin_scan",
    )(gq, gk, gvk, gkt, oa, gq, gk, gvk, gkt, oa, gconst, gs0,
      oh, oh, oh, vt, oh, oh, oh, vt, lb, ones, hs0)


def _post_kernel(x_ref, ogf_ref, ogb_ref, z_ref, gnw_ref, ohf_ref, ohb_ref, hg_ref, hnw_ref, om_ref, seg_ref,
                 wout_ref, gpm_ref, gt1_ref, gpf_ref, sc2_ref, sh2_ref, gt2_ref, gff_ref,
                 wa_ref, wb_ref, wo_ref, out_ref):
    seg = seg_ref[...]
    inv = 1.0 / HD
    og = ogf_ref[...] + ogb_ref[...]
    og = og * lax.rsqrt(_seg_sum(og * og, seg) * inv + EPS) * gnw_ref[...] * _silu(z_ref[...])
    oh = ohf_ref[...] + ohb_ref[...]
    oh = oh * lax.rsqrt(_seg_sum(oh * oh, seg) * inv + EPS) * hnw_ref[...] * _sigmoid(hg_ref[...])
    mix = jnp.concatenate([og.astype(BF16), oh.astype(BF16), om_ref[...].astype(BF16)], axis=-1)
    x1 = x_ref[...] + gt1_ref[0] * _rms(_dot(mix, wout_ref[...]), gpm_ref[...])
    hb = (_rms(x1, gpf_ref[...]) * (1.0 + sc2_ref[0]) + sh2_ref[0]).astype(BF16)
    fw = D_FF // FF_SPLIT
    y = None
    for part in range(FF_SPLIT):
        cs = slice(part * fw, (part + 1) * fw)
        act = (_silu(_dot(hb, wa_ref[:, cs])) * _dot(hb, wb_ref[:, cs])).astype(BF16)
        contrib = _dot(act, wo_ref[cs, :])
        y = contrib if y is None else y + contrib
    out_ref[...] = x1 + gt2_ref[0] * _rms(y, gff_ref[...])


def _post(x, ogf, ogb, og, gnw, ohf, ohb, oh, hnw, om, seg, wout, gpm, gt1, gpf, sc2, sh2, gt2, gff, wa, wb, wo, tm):
    m = x.shape[0]
    tiles = m // tm
    per = tiles // gt1.shape[0]
    row = lambda i: (i // per, 0, 0)
    tok = lambda w: pl.BlockSpec((tm, w), lambda i: (i, 0))
    vec = lambda w: pl.BlockSpec((1, w), lambda i: (0, 0))
    mod = pl.BlockSpec((1, 1, D_MODEL), row)
    once = lambda r, c: pl.BlockSpec((r, c), lambda i: (0, 0), pipeline_mode=pl.Buffered(1))
    return pl.pallas_call(
        _post_kernel,
        grid=(tiles,),
        in_specs=[tok(D_MODEL), tok(LIN_W), tok(LIN_W),
                  pl.BlockSpec((tm, LIN_W), lambda i: (i, G_W // LIN_W - 1)), vec(LIN_W),
                  tok(LIN_W), tok(LIN_W),
                  pl.BlockSpec((tm, LIN_W), lambda i: (i, H_W // LIN_W - 1)), vec(LIN_W),
                  tok(HEADS * MLA_DV), once(LIN_W, LIN_W),
                  once(D_MODEL, D_MODEL), vec(D_MODEL), mod, vec(D_MODEL), mod, mod, mod, vec(D_MODEL),
                  once(D_MODEL, D_FF), once(D_MODEL, D_FF), once(D_FF, D_MODEL)],
        out_specs=tok(D_MODEL),
        out_shape=jax.ShapeDtypeStruct((m, D_MODEL), F32),
        compiler_params=pltpu.CompilerParams(dimension_semantics=("arbitrary",), vmem_limit_bytes=VMEM_LIMIT),
        name="post",
    )(x, ogf, ogb, og, gnw, ohf, ohb, oh, hnw, om, seg, wout, gpm, gt1, gpf, sc2, sh2, gt2, gff, wa, wb, wo)


def _rope_tables(t_len, use_rope):
    scale = MLA_QK ** -0.5 * LOG2E
    if use_rope:
        rows = t_len // GRID_W
        row = jnp.repeat(jnp.arange(rows, dtype=F32), GRID_W)
        col = jnp.tile(jnp.arange(GRID_W, dtype=F32), rows)
        nf = MLA_ROPE // 4
        inv = ROPE_BASE ** (-jnp.arange(nf, dtype=F32) / nf)
        ar, ac = row[:, None] * inv, col[:, None] * inv
        cos = jnp.concatenate([jnp.cos(ar), jnp.cos(ar), jnp.cos(ac), jnp.cos(ac)], axis=-1)
        sin = jnp.concatenate([-jnp.sin(ar), jnp.sin(ar), -jnp.sin(ac), jnp.sin(ac)], axis=-1)
    else:
        cos = jnp.ones((t_len, MLA_ROPE), F32)
        sin = jnp.zeros((t_len, MLA_ROPE), F32)
    z64 = jnp.zeros((t_len, MLA_ROPE), F32)
    qa = scale * jnp.concatenate([jnp.ones((t_len, MLA_NOPE), F32), cos, z64], axis=-1)
    qb = scale * jnp.concatenate([jnp.zeros((t_len, MLA_NOPE), F32), sin, z64], axis=-1)
    ck = jnp.concatenate([cos, z64], axis=-1)
    sk = jnp.concatenate([sin, z64], axis=-1)
    return qa, qb, ck, sk


def _layer_weights(l, w_in, w_out, gdn_conv_w, gdn_a_log, gdn_dt_bias, gdn_norm_w, lower_bounds, hgrn_norm_w,
                   mla_q_norm_w, mla_w_uq, mla_kv_norm_w, mla_w_ukv, w_ffn_in, w_ffn_out,
                   g_pre_mix, g_post_mix, g_pre_ffn, g_post_ffn):
    wi = w_in[l]
    w_cat = jnp.concatenate([wi[:, _PERM_G], wi[:, _PERM_H], wi[:, _PERM_A],
                             jnp.zeros((D_MODEL, A_W - _PERM_A.shape[0]), F32)], axis=1).astype(BF16)
    convw = jnp.concatenate([gdn_conv_w[l][_PERM_CONV].T, jnp.zeros((3, G_CONV), F32)], axis=0)
    neg_a = -jnp.exp(gdn_a_log[l].astype(F32)).reshape(1, 8)
    dt = gdn_dt_bias[l].astype(F32).reshape(1, 8)
    pad = lambda r: jnp.pad(r, ((0, 0), (0, 128 - r.shape[1])))
    gconst = jnp.concatenate([pad(neg_a), pad(dt), jnp.zeros((6, 128), F32)], axis=0)
    vec = lambda a: a.reshape(1, -1).astype(F32)
    return dict(
        w_cat=w_cat, convw=convw, gconst=gconst,
        gnw=vec(jnp.tile(gdn_norm_w[l], HEADS)), hnw=vec(jnp.tile(hgrn_norm_w[l], HEADS)),
        layer=l, lb=lower_bounds,
        qnw=vec(mla_q_norm_w[l]), kvnw=vec(mla_kv_norm_w[l]),
        wuq=mla_w_uq[l][:, _PERM_UQ].astype(BF16), wukv=mla_w_ukv[l][:, _PERM_UKV].astype(BF16),
        wout=w_out[l].astype(BF16), wa=w_ffn_in[l][:, :D_FF].astype(BF16), wb=w_ffn_in[l][:, D_FF:].astype(BF16),
        wo=w_ffn_out[l].astype(BF16),
        g_pre_mix=vec(g_pre_mix[l]), g_post_mix=vec(g_post_mix[l]),
        g_pre_ffn=vec(g_pre_ffn[l]), g_post_ffn=vec(g_post_ffn[l]))


def _trunk_layer(x, mod, w, gdn_s0, hg_s0, tables, ctx_keys, seg512, seg256):
    b, t, _ = x.shape
    m = b * t
    tm = min(TOKEN_TILE, t)
    sh1, sc1, gt1, sh2, sc2, gt2 = mod
    xf = x.reshape(m, D_MODEL)
    og, oh, oa, vt = _premix(xf, w["g_pre_mix"], sc1, sh1, w["w_cat"], tm, t)

    qa, qb, ck, sk = tables
    qcat, kcat, vcat, ckv = _mla_prep(oa, w["qnw"], w["kvnw"], w["wuq"], w["wukv"], qa, qb, ck, sk, tm, t)
    kw = HEADS * MLA_HEAD_PAD
    qcat = qcat.reshape(b, t, kw)
    kcat = kcat.reshape(b, t, kw)
    vcat = vcat.reshape(b, t, HEADS * MLA_DV)
    if ctx_keys is not None:
        kcat = jnp.concatenate([kcat, ctx_keys[0]], axis=1)
        vcat = jnp.concatenate([vcat, ctx_keys[1]], axis=1)
    o_mla = _attn(qcat, kcat, vcat, min(ATTN_Q_TILE, t)).reshape(m, HEADS * MLA_DV)

    gq, gk, gvk, gkt = _gdn_prep(og, w["convw"], seg512, tm, t)
    ogf, ogb, gdn_state, ohf, ohb, hg_state = _lin_scan(
        w["layer"], gq.reshape(b, t, LIN_W), gk.reshape(b, t, LIN_W), gvk.reshape(b, t, 2 * LIN_W), gkt,
        oa.reshape(b, t, A_W), w["gconst"], gdn_s0,
        oh.reshape(b, t, H_W), vt, w["lb"], seg256, hg_s0)

    x_new = _post(xf, ogf.reshape(m, LIN_W), ogb.reshape(m, LIN_W), og, w["gnw"],
                  ohf.reshape(m, LIN_W), ohb.reshape(m, LIN_W), oh, w["hnw"], o_mla, seg256,
                  w["wout"], w["g_post_mix"], gt1, w["g_pre_ffn"], sc2, sh2, gt2, w["g_post_ffn"],
                  w["wa"], w["wb"], w["wo"], tm)
    mkr = oa[:, A_KR:A_KR + MLA_ROPE]
    return x_new.reshape(b, t, D_MODEL), gdn_state, hg_state, ckv.reshape(b, t, MLA_KV_RANK), mkr.reshape(b, t, MLA_ROPE)


def _hg_state_to_block(s):
    st = jnp.swapaxes(s, -1, -2)
    eye = jnp.eye(HEADS, dtype=s.dtype)
    big = st[:, :, :, :, None, :] * eye[None, None, :, None, :, None]
    return big.reshape(s.shape[0], 2, LIN_W, LIN_W)


def _hg_block_to_state(big):
    b = big.shape[0]
    r = big.reshape(b, 2, HEADS, HD, HEADS, HD)
    diag = jnp.stack([r[:, :, h, :, h, :] for h in range(HEADS)], axis=2)
    return jnp.swapaxes(diag, -1, -2)


def kernel(x_prompt, x_sample, cache_mla_ckv, cache_mla_krope, state_gdn, state_hgrn, c, c_ctx, w_ada, b_ada,
           g_pre_mix, g_post_mix, g_pre_ffn, g_post_ffn, w_in, w_out, gdn_conv_w, gdn_a_log, gdn_dt_bias,
           gdn_norm_w, hgrn_lb, hgrn_norm_w, mla_q_norm_w, mla_w_uq, mla_kv_norm_w, mla_w_ukv, w_ffn_in, w_ffn_out):
    bp, tp, _ = x_prompt.shape
    bd, td, _ = x_sample.shape
    past = cache_mla_ckv.shape[2]
    assert 1 + bd <= MOD_ROWS and tp % SCAN_BLOCK == 0 and td % SCAN_BLOCK == 0

    lower_bounds = jnp.pad(hgrn_lb.astype(F32), ((0, 8 - DEPTH), (0, 0)))

    cond = jnp.concatenate([c_ctx[None, :], c, jnp.zeros((MOD_ROWS - 1 - bd, D_MODEL), F32)], axis=0)
    mod_all = _ada(cond, w_ada, b_ada)

    seg512 = _seg_ones(2 * LIN_W)
    seg256 = _seg_ones(LIN_W)
    tab_ctx = _rope_tables(tp, False)
    tab_lat = _rope_tables(td, True)
    ones_k = jnp.concatenate([jnp.ones((past, MLA_ROPE), F32), jnp.zeros((past, MLA_ROPE), F32)], axis=-1)
    zeros_k = jnp.zeros((past, 128), F32)

    weights = [_layer_weights(l, w_in, w_out, gdn_conv_w, gdn_a_log, gdn_dt_bias, gdn_norm_w, lower_bounds,
                              hgrn_norm_w, mla_q_norm_w, mla_w_uq, mla_kv_norm_w, mla_w_ukv, w_ffn_in, w_ffn_out,
                              g_pre_mix, g_post_mix, g_pre_ffn, g_post_ffn) for l in range(DEPTH)]

    def mods(l, lo, n):
        rows = mod_all[l, lo:lo + n].reshape(n, 1, 6 * D_MODEL)
        return tuple(rows[:, :, i * D_MODEL:(i + 1) * D_MODEL] for i in range(6))

    xp = x_prompt
    zero_g = jnp.zeros((bp, 2 * HEADS, HD, HD), F32)
    zero_h = jnp.zeros((bp, 2, LIN_W, LIN_W), F32)
    ckv_l, kr_l, gs_l, hs_l = [], [], [], []
    for l in range(DEPTH):
        xp, gs, hs, ckv, mkr = _trunk_layer(xp, mods(l, 0, 1), weights[l], zero_g, zero_h, tab_ctx, None,
                                            seg512, seg256)
        ckv_l.append(ckv)
        kr_l.append(mkr)
        gs_l.append(gs.reshape(bp, 2, HEADS, HD, HD))
        hs_l.append(_hg_block_to_state(hs))

    xs = x_sample
    for l in range(DEPTH):
        kr128 = jnp.pad(cache_mla_krope[:, l].reshape(bd * past, MLA_ROPE), ((0, 0), (0, 128 - MLA_ROPE)))
        k_ctx, v_ctx = _kv_up(cache_mla_ckv[:, l].reshape(bd * past, MLA_KV_RANK), kr128, ones_k, zeros_k,
                              weights[l]["wukv"], past)
        ctx_keys = (k_ctx.reshape(bd, past, HEADS * MLA_HEAD_PAD), v_ctx.reshape(bd, past, HEADS * MLA_DV))
        xs, _, _, _, _ = _trunk_layer(xs, mods(l, 1, bd), weights[l],
                                      state_gdn[:, l].reshape(bd, 2 * HEADS, HD, HD),
                                      _hg_state_to_block(state_hgrn[:, l]), tab_lat, ctx_keys, seg512, seg256)

    return (xp, xs, jnp.stack(ckv_l, axis=1), jnp.stack(kr_l, axis=1),
            jnp.stack(gs_l, axis=1), jnp.stack(hs_l, axis=1))
```

```python
import functools
import math

import numpy as np
import jax
import jax.numpy as jnp
from jax import lax
from jax.experimental import pallas as pl
from jax.experimental.pallas import tpu as pltpu

F32 = jnp.float32
BF16 = jnp.bfloat16

D_MODEL = 1024
DEPTH = 2
GRID_W = 64
EPS = 1e-6
GATE_FLOOR = 1e-30
HEADS = 4
HD = 64
LIN_W = HEADS * HD
GDN_CHUNK = 64
HG_CHUNK = 16
SCAN_BLOCK = 256
LOG2E = 1.4426950408889634
MLA_Q_RANK = 384
MLA_KV_RANK = 256
MLA_NOPE = 128
MLA_ROPE = 64
MLA_DV = 128
MLA_QK = MLA_NOPE + MLA_ROPE
MLA_HEAD_PAD = 256
ROPE_BASE = 10000.0
D_FF = -(-8 * D_MODEL // (3 * 256)) * 256
FF_SPLIT = 2
TOKEN_TILE = 512
ATTN_Q_TILE = 256
MOD_ROWS = 16
VMEM_LIMIT = 56 * 1024 * 1024

_OFF = {}
_o = 0
for _n, _s in (("gq", 256), ("gk", 256), ("gv", 256), ("gz", 256), ("ga", 8), ("gb", 8),
               ("hq", 256), ("hi", 256), ("hf", 512), ("hg", 256),
               ("mcq", MLA_Q_RANK), ("mckv", MLA_KV_RANK), ("mkr", MLA_ROPE)):
    _OFF[_n] = _o
    _o += _s
IN_DIM = _o

G_W = 1024
G_CONV = 768
H_W = 1280
A_W = 896
A_KR = MLA_Q_RANK + MLA_KV_RANK
A_AB = A_KR + 2 * MLA_ROPE


def _rope_swap_idx():
    j = np.arange(MLA_ROPE)
    return np.where((j % 32) < 16, j + 16, j - 16)


def _in_perm():
    ar = np.arange
    g = np.concatenate([_OFF["gq"] + ar(256), _OFF["gk"] + ar(256), _OFF["gv"] + ar(256), _OFF["gz"] + ar(256)])
    hh = np.concatenate([_OFF["hq"] + ar(256), _OFF["hi"] + ar(256), _OFF["hf"] + ar(512), _OFF["hg"] + ar(256)])
    a = np.concatenate([_OFF["mcq"] + ar(MLA_Q_RANK), _OFF["mckv"] + ar(MLA_KV_RANK), _OFF["mkr"] + ar(MLA_ROPE),
                        _OFF["mkr"] + _rope_swap_idx(), _OFF["ga"] + ar(8), _OFF["gb"] + ar(8)])
    return g, hh, a


_PERM_G, _PERM_H, _PERM_A = _in_perm()


def _uq_perm():
    sw = _rope_swap_idx()
    cols = []
    for h in range(HEADS):
        base = h * MLA_QK
        cols += [base + np.arange(MLA_NOPE), base + MLA_NOPE + np.arange(MLA_ROPE), base + MLA_NOPE + sw]
    return np.concatenate(cols)


def _ukv_perm():
    per = MLA_NOPE + MLA_DV
    kn = np.concatenate([h * per + np.arange(MLA_NOPE) for h in range(HEADS)])
    vv = np.concatenate([h * per + MLA_NOPE + np.arange(MLA_DV) for h in range(HEADS)])
    return np.concatenate([kn, vv])


_PERM_UQ = _uq_perm()
_PERM_UKV = _ukv_perm()


def _seg_ones(n):
    i = np.arange(n) // HD
    return jnp.asarray((i[:, None] == i[None, :]).astype(np.float32), dtype=BF16)


def _rms(x, w):
    return x * lax.rsqrt(jnp.mean(x * x, axis=-1, keepdims=True) + EPS) * w


def _dot(a, b):
    return jnp.dot(a, b, preferred_element_type=F32)


def _dot_t(a, b):
    return lax.dot_general(a, b, (((1,), (1,)), ((), ())), preferred_element_type=F32)


def _split3(x):
    hi = x.astype(BF16)
    r1 = x - hi.astype(F32)
    mid = r1.astype(BF16)
    lo = (r1 - mid.astype(F32)).astype(BF16)
    return hi, mid, lo


def _split2(x):
    hi = x.astype(BF16)
    return hi, (x - hi.astype(F32)).astype(BF16)


def _seg_sum(x, seg):
    hi, lo = _split2(x)
    return _dot(hi, seg) + _dot(lo, seg)


def _mask_dot(mask01, x):
    hi, mid, lo = _split3(x)
    return _dot(mask01, hi) + _dot(mask01, mid) + _dot(mask01, lo)


def _dot_mask_t(x, mask01):
    hi, mid, lo = _split3(x)
    return _dot_t(hi, mask01) + _dot_t(mid, mask01) + _dot_t(lo, mask01)


def _softplus(x):
    return jnp.maximum(x, 0.0) + jnp.log1p(jnp.exp(-jnp.abs(x)))


def _sigmoid(x):
    return jax.nn.sigmoid(x)


def _silu(x):
    return x * jax.nn.sigmoid(x)


def _ada_kernel(c_ref, w_ref, b_ref, o_ref):
    s = _silu(c_ref[...]).astype(BF16)
    o_ref[0] = _dot(s, w_ref[0].astype(BF16)) + b_ref[0]


def _ada(cond, w_ada, b_ada):
    n = w_ada.shape[-1]
    tn = 1536
    return pl.pallas_call(
        _ada_kernel,
        grid=(DEPTH, n // tn),
        in_specs=[pl.BlockSpec((MOD_ROWS, D_MODEL), lambda l, j: (0, 0)),
                  pl.BlockSpec((1, D_MODEL, tn), lambda l, j: (l, 0, j)),
                  pl.BlockSpec((1, 1, tn), lambda l, j: (l, 0, j))],
        out_specs=pl.BlockSpec((1, MOD_ROWS, tn), lambda l, j: (l, 0, j)),
        out_shape=jax.ShapeDtypeStruct((DEPTH, MOD_ROWS, n), F32),
        compiler_params=pltpu.CompilerParams(dimension_semantics=("arbitrary", "arbitrary"),
                                             vmem_limit_bytes=VMEM_LIMIT),
        name="ada",
    )(cond, w_ada, b_ada.reshape(DEPTH, 1, n))


def _premix_kernel(x_ref, g_ref, sc_ref, sh_ref, w_ref, og_ref, oh_ref, oa_ref, vt_ref):
    h = _rms(x_ref[...], g_ref[...]) * (1.0 + sc_ref[0]) + sh_ref[0]
    hb = h.astype(BF16)
    og_ref[...] = _dot(hb, w_ref[:, 0:G_W])
    oh = _dot(hb, w_ref[:, G_W:G_W + H_W])
    oh_ref[...] = oh
    oa_ref[...] = _dot(hb, w_ref[:, G_W + H_W:G_W + H_W + A_W])
    vt_ref[0] = oh[:, LIN_W:2 * LIN_W].T


def _premix(x, g, sc, sh, w, tm, t_len):
    m = x.shape[0]
    tiles = m // tm
    per = tiles // sc.shape[0]
    row = lambda i: (i // per, 0, 0)
    tps = t_len // tm
    seq = lambda i: (i // tps, 0, i % tps)
    wt = G_W + H_W + A_W
    return pl.pallas_call(
        _premix_kernel,
        grid=(tiles,),
        in_specs=[pl.BlockSpec((tm, D_MODEL), lambda i: (i, 0)),
                  pl.BlockSpec((1, D_MODEL), lambda i: (0, 0)),
                  pl.BlockSpec((1, 1, D_MODEL), row),
                  pl.BlockSpec((1, 1, D_MODEL), row),
                  pl.BlockSpec((D_MODEL, wt), lambda i: (0, 0))],
        out_specs=[pl.BlockSpec((tm, G_W), lambda i: (i, 0)),
                   pl.BlockSpec((tm, H_W), lambda i: (i, 0)),
                   pl.BlockSpec((tm, A_W), lambda i: (i, 0)),
                   pl.BlockSpec((1, LIN_W, tm), seq)],
        out_shape=[jax.ShapeDtypeStruct((m, G_W), F32),
                   jax.ShapeDtypeStruct((m, H_W), F32),
                   jax.ShapeDtypeStruct((m, A_W), F32),
                   jax.ShapeDtypeStruct((m // t_len, LIN_W, t_len), F32)],
        compiler_params=pltpu.CompilerParams(dimension_semantics=("arbitrary",), vmem_limit_bytes=VMEM_LIMIT),
        name="premix",
    )(x, g, sc, sh, w)


def _keys_from_latent(ckv, krot, wukv_ref, k_out, v_out):
    kv = _dot(ckv.astype(BF16), wukv_ref[...])
    for h in range(HEADS):
        k_out[:, MLA_HEAD_PAD * h:MLA_HEAD_PAD * h + MLA_NOPE] = kv[:, MLA_NOPE * h:MLA_NOPE * (h + 1)].astype(BF16)
        k_out[:, MLA_HEAD_PAD * h + MLA_NOPE:MLA_HEAD_PAD * (h + 1)] = krot
    v_out[...] = kv[:, HEADS * MLA_NOPE:].astype(BF16)


def _mla_prep_kernel(tps, past, a_ref, qnw_ref, kvnw_ref, wuq_ref, wukv_ref, qa_ref, qb_ref, ck_ref, sk_ref, *refs):
    if past:
        cache_ckv_ref, cache_kr_ref, q_out, k_out, v_out, ckv_out = refs
    else:
        q_out, k_out, v_out, ckv_out = refs
    j = pl.program_id(1)

    @pl.when(j < tps)
    def _():
        qn = _rms(a_ref[:, 0:MLA_Q_RANK], qnw_ref[...])
        y = _dot(qn.astype(BF16), wuq_ref[...])
        z = pltpu.roll(y, HEADS * MLA_HEAD_PAD - MLA_ROPE, 1)
        qa = qa_ref[...]
        qb = qb_ref[...]
        for h in range(HEADS):
            sl = slice(MLA_HEAD_PAD * h, MLA_HEAD_PAD * (h + 1))
            q_out[:, sl] = (y[:, sl] * qa + z[:, sl] * qb).astype(BF16)
        ckv = _rms(a_ref[:, MLA_Q_RANK:A_KR], kvnw_ref[...])
        ckv_out[...] = ckv
        kr = a_ref[:, A_KR:A_AB]
        krot = (kr * ck_ref[...] + pltpu.roll(kr, 64, 1) * sk_ref[...]).astype(BF16)
        _keys_from_latent(ckv, krot, wukv_ref, k_out.at[0], v_out.at[0])

    if past:
        @pl.when(j == tps)
        def _():
            _keys_from_latent(cache_ckv_ref[0], cache_kr_ref[0].astype(BF16), wukv_ref,
                              k_out.at[0, 0:past], v_out.at[0, 0:past])


def _mla_prep(oa, qnw, kvnw, wuq, wukv, qa, qb, ck, sk, tm, t_len, cache=None):
    m = oa.shape[0]
    b = m // t_len
    tps = t_len // tm
    past = 0 if cache is None else cache[0].shape[1]
    tok = lambda bi, j: (bi * tps + jnp.minimum(j, tps - 1), 0)
    pos = lambda bi, j: (jnp.minimum(j, tps - 1), 0)
    full = lambda bi, j: (0, 0)
    seq = lambda bi, j: (bi, j, 0)
    kw = HEADS * MLA_HEAD_PAD
    in_specs = [pl.BlockSpec((tm, A_W), tok),
                pl.BlockSpec((1, MLA_Q_RANK), full),
                pl.BlockSpec((1, MLA_KV_RANK), full),
                pl.BlockSpec((MLA_Q_RANK, kw), full),
                pl.BlockSpec((MLA_KV_RANK, HEADS * (MLA_NOPE + MLA_DV)), full),
                pl.BlockSpec((tm, MLA_HEAD_PAD), pos),
                pl.BlockSpec((tm, MLA_HEAD_PAD), pos),
                pl.BlockSpec((tm, 128), pos),
                pl.BlockSpec((tm, 128), pos)]
    operands = [oa, qnw, kvnw, wuq, wukv, qa, qb, ck, sk]
    if past:
        in_specs += [pl.BlockSpec((1, past, MLA_KV_RANK), lambda bi, j: (bi, 0, 0)),
                     pl.BlockSpec((1, past, 128), lambda bi, j: (bi, 0, 0))]
        operands += list(cache)
    return pl.pallas_call(
        functools.partial(_mla_prep_kernel, tps, past),
        grid=(b, tps + (1 if past else 0)),
        in_specs=in_specs,
        out_specs=[pl.BlockSpec((tm, kw), tok),
                   pl.BlockSpec((1, tm, kw), seq),
                   pl.BlockSpec((1, tm, HEADS * MLA_DV), seq),
                   pl.BlockSpec((tm, MLA_KV_RANK), tok)],
        out_shape=[jax.ShapeDtypeStruct((m, kw), BF16),
                   jax.ShapeDtypeStruct((b, t_len + past, kw), BF16),
                   jax.ShapeDtypeStruct((b, t_len + past, HEADS * MLA_DV), BF16),
                   jax.ShapeDtypeStruct((m, MLA_KV_RANK), F32)],
        compiler_params=pltpu.CompilerParams(dimension_semantics=("arbitrary", "arbitrary"),
                                             vmem_limit_bytes=VMEM_LIMIT),
        name="mla_prep",
    )(*operands)


def _attn_kernel(q_ref, k_ref, v_ref, o_ref):
    for h in range(HEADS):
        sl = slice(MLA_HEAD_PAD * h, MLA_HEAD_PAD * (h + 1))
        s = _dot_t(q_ref[0, :, sl], k_ref[0, :, sl])
        p = jnp.exp2(s - jnp.max(s, axis=-1, keepdims=True))
        l = jnp.sum(p, axis=-1, keepdims=True)
        o = _dot(p.astype(BF16), v_ref[0, :, MLA_DV * h:MLA_DV * (h + 1)])
        o_ref[0, :, MLA_DV * h:MLA_DV * (h + 1)] = o / l


def _attn(q, k, v, tq):
    b, t, kw = q.shape
    s = k.shape[1]
    return pl.pallas_call(
        _attn_kernel,
        grid=(b, t // tq),
        in_specs=[pl.BlockSpec((1, tq, kw), lambda bi, i: (bi, i, 0)),
                  pl.BlockSpec((1, s, kw), lambda bi, i: (bi, 0, 0)),
                  pl.BlockSpec((1, s, HEADS * MLA_DV), lambda bi, i: (bi, 0, 0))],
        out_specs=pl.BlockSpec((1, tq, HEADS * MLA_DV), lambda bi, i: (bi, i, 0)),
        out_shape=jax.ShapeDtypeStruct((b, t, HEADS * MLA_DV), F32),
        compiler_params=pltpu.CompilerParams(dimension_semantics=("arbitrary", "arbitrary"),
                                             vmem_limit_bytes=VMEM_LIMIT),
        name="attn",
    )(q, k, v)


def _gdn_prep_kernel(tps, cur_ref, prev_ref, next_ref, w_ref, seg_ref, q_out, k_out, v_out, kt_out):
    i = pl.program_id(0)
    tm = cur_ref.shape[0]
    has_prev = ((i % tps) != 0).astype(F32)
    has_next = ((i % tps) != (tps - 1)).astype(F32)
    xc = jnp.concatenate([prev_ref[...] * has_prev, cur_ref[...], next_ref[...] * has_next], axis=0)
    n = tm + 16
    y = xc[8:8 + tm] * w_ref[2:3, :]
    for j in (0, 1, 3, 4):
        d = j - 2
        y = y + pltpu.roll(xc, (n - d) % n, 0)[8:8 + tm] * w_ref[j:j + 1, :]
    y = _silu(y)
    seg = seg_ref[...]
    qk = y[:, 0:2 * LIN_W]
    qk = qk * lax.rsqrt(_seg_sum(qk * qk, seg) + EPS)
    q_out[...] = qk[:, 0:LIN_W] * (HD ** -0.5)
    k_out[...] = qk[:, LIN_W:2 * LIN_W]
    kt_out[0] = qk[:, LIN_W:2 * LIN_W].T
    v_out[...] = y[:, 2 * LIN_W:]


def _gdn_prep(og, convw, seg, tm, t_len):
    m = og.shape[0]
    tiles = m // tm
    tps = t_len // tm
    r8 = tm // 8
    last8 = m // 8 - 1
    return pl.pallas_call(
        functools.partial(_gdn_prep_kernel, tps),
        grid=(tiles,),
        in_specs=[pl.BlockSpec((tm, G_CONV), lambda i: (i, 0)),
                  pl.BlockSpec((8, G_CONV), lambda i: (jnp.maximum(i * r8 - 1, 0), 0)),
                  pl.BlockSpec((8, G_CONV), lambda i: (jnp.minimum((i + 1) * r8, last8), 0)),
                  pl.BlockSpec((8, G_CONV), lambda i: (0, 0)),
                  pl.BlockSpec((2 * LIN_W, 2 * LIN_W), lambda i: (0, 0))],
        out_specs=[pl.BlockSpec((tm, LIN_W), lambda i: (i, 0)),
                   pl.BlockSpec((tm, LIN_W), lambda i: (i, 0)),
                   pl.BlockSpec((tm, LIN_W), lambda i: (i, 0)),
                   pl.BlockSpec((1, LIN_W, tm), lambda i: (i // tps, 0, i % tps))],
        out_shape=[jax.ShapeDtypeStruct((m, LIN_W), F32),
                   jax.ShapeDtypeStruct((m, LIN_W), F32),
                   jax.ShapeDtypeStruct((m, LIN_W), F32),
                   jax.ShapeDtypeStruct((m // t_len, LIN_W, t_len), F32)],
        compiler_params=pltpu.CompilerParams(dimension_semantics=("arbitrary",), vmem_limit_bytes=VMEM_LIMIT),
        name="gdn_prep",
    )(og, og, og, convw, seg)


def _gdn_units(d, q_ref, k_ref, v_ref, kt_ref, ab_ref, gc_ref, o_ref):
    c = GDN_CHUNK
    ri = lax.broadcasted_iota(jnp.int32, (c, c), 0)
    ci = lax.broadcasted_iota(jnp.int32, (c, c), 1)
    incl = (ri >= ci) if d == 0 else (ri <= ci)
    strict = (ri > ci) if d == 0 else (ri < ci)
    rb = lax.broadcasted_iota(jnp.int32, (SCAN_BLOCK, SCAN_BLOCK), 0)
    cb = lax.broadcasted_iota(jnp.int32, (SCAN_BLOCK, SCAN_BLOCK), 1)
    same_chunk = jnp.right_shift(rb, 6) == jnp.right_shift(cb, 6)
    incl_blk = (same_chunk & ((rb >= cb) if d == 0 else (rb <= cb))).astype(F32).astype(BF16)
    last = c - 1 if d == 0 else 0
    ab = ab_ref[0]
    g_cols = gc_ref[0:1, :] * _softplus(ab + gc_ref[1:2, :])
    beta = _sigmoid(ab)
    gcs_blk = _mask_dot(incl_blk, g_cols)
    gcr_blk = gcs_blk.T
    n_chunks = SCAN_BLOCK // c
    units = []
    for rank, cc in enumerate(range(n_chunks) if d == 0 else range(n_chunks - 1, -1, -1)):
        rows = slice(cc * c, (cc + 1) * c)
        for h in range(HEADS):
            j = d * HEADS + h
            hs = slice(HD * h, HD * (h + 1))
            gcol = gcs_blk[rows, j:j + 1]
            grow = gcr_blk[j:j + 1, rows]
            bcol = beta[rows, 8 + j:9 + j]
            gl = gcol[last:last + 1, :]
            eg = jnp.exp(gcol)
            k = k_ref[0, rows, hs]
            units.append(dict(
                rank=rank, j=j, rows=rows, hs=hs, o_ref=o_ref, strict=strict, gl=gl, bcol=bcol, eg=eg,
                q=q_ref[0, rows, hs], k=k,
                k_tail=(kt_ref[0, hs, rows] * jnp.exp(gl - grow)).astype(BF16),
                decay=jnp.where(incl, jnp.exp(jnp.where(incl, gcol - grow, 0.0)), 0.0),
                x=jnp.concatenate([v_ref[0, rows, hs] * bcol, k * (bcol * eg)], axis=1)))
    return units


def _gdn_stages(units, s_ref):
    c = GDN_CHUNK
    ri = lax.broadcasted_iota(jnp.int32, (c, c), 0)
    ci = lax.broadcasted_iota(jnp.int32, (c, c), 1)
    diag_blk = jnp.right_shift(ri, 4) == jnp.right_shift(ci, 4)
    eye = (ri == ci).astype(F32)

    def gram():
        for u in units:
            kb16 = u["k"].astype(BF16)
            u["kk"] = _dot_t((u["k"] * u["bcol"]).astype(BF16), kb16)
            u["qk"] = _dot_t(u["q"].astype(BF16), kb16)

    def split():
        for u in units:
            p = jnp.where(u["strict"], u["kk"] * u["decay"], 0.0)
            pd = jnp.where(diag_blk, p, 0.0)
            u["att"] = (u["qk"] * u["decay"]).astype(BF16)
            u["poff"] = (p - pd).astype(BF16)
            u["td"] = eye - pd
            u["a"] = pd.astype(BF16)

    def square():
        for u in units:
            u["a"] = _dot(u["a"], u["a"]).astype(BF16)

    def extend():
        for u in units:
            u["td"] = u["td"] + _dot(u["td"].astype(BF16), u["a"])

    def apply_diag():
        for u in units:
            yn = _dot(u["td"].astype(BF16), jnp.concatenate([u["x"].astype(BF16), u["poff"]], axis=1))
            u["y"] = yn[:, 0:2 * HD]
            u["n"] = yn[:, 2 * HD:3 * HD].astype(BF16)
            u["x"] = u["y"]

    def substitute():
        for u in units:
            u["x"] = u["y"] - _dot(u["n"], u["x"].astype(BF16))

    def new_values(rank):
        for u in units:
            if u["rank"] == rank:
                u["s"] = s_ref[0, u["j"]]
                u["sb"] = u["s"].astype(BF16)
                u["vb"] = (u["x"][:, 0:HD] - _dot(u["x"][:, HD:2 * HD].astype(BF16), u["sb"])).astype(BF16)

    def emit(rank):
        for u in units:
            if u["rank"] == rank:
                u["o_ref"][0, u["rows"], u["hs"]] = (_dot((u["q"] * u["eg"]).astype(BF16), u["sb"])
                                                     + _dot(u["att"], u["vb"]))
                s_ref[0, u["j"]] = u["s"] * jnp.exp(u["gl"]) + _dot(u["k_tail"], u["vb"])

    stages = [gram, split] + [square, extend] * 3 + [apply_diag] + [substitute] * 3
    for rank in range(SCAN_BLOCK // c):
        stages += [functools.partial(new_values, rank), functools.partial(emit, rank)]
    return stages


def _hgrn_lower_bound(layer, lb_ref):
    raw = lb_ref[0:DEPTH, :]
    e = jnp.exp(raw - jnp.max(raw, axis=0, keepdims=True))
    gamma = e / jnp.sum(e, axis=0, keepdims=True)
    lb = jnp.zeros((1, LIN_W), F32)
    for i in range(1, layer + 1):
        lb = lb + gamma[i:i + 1, :]
    return lb


def _hgrn_block(d, lb, q_ref, v_ref, f_ref):
    c = HG_CHUNK
    ri = lax.broadcasted_iota(jnp.int32, (SCAN_BLOCK, SCAN_BLOCK), 0)
    ci = lax.broadcasted_iota(jnp.int32, (SCAN_BLOCK, SCAN_BLOCK), 1)
    same_chunk = jnp.right_shift(ri, 4) == jnp.right_shift(ci, 4)
    incl01 = (same_chunk & ((ri >= ci) if d == 0 else (ri <= ci))).astype(F32).astype(BF16)
    oml = 1.0 - lb
    q = q_ref[0]
    f = f_ref[0]
    logf = jnp.log(jnp.maximum(lb + oml * _sigmoid(f), GATE_FLOOR))
    bc = _mask_dot(incl01, logf)
    return dict(d=d, q=q, v=v_ref[0], kk=oml * _sigmoid(-f), bc=bc, bc2=bc * LOG2E,
                qe=(q * jnp.exp(bc)).astype(BF16), last=c - 1 if d == 0 else 0)


def _hgrn_live_halves(d, s):
    half = HG_CHUNK // 2
    if d == 0:
        return (1,) if s >= half else (0, 1)
    return (0,) if s < half else (0, 1)


def _hgrn_chunk_build(blk, cc):
    c = HG_CHUNK
    half = c // 2
    d = blk["d"]
    base = cc * c
    rows = slice(base, base + c)
    kk, bc = blk["kk"][rows], blk["bc"][rows]
    bl = bc[blk["last"]:blk["last"] + 1, :]
    row_half = lax.broadcasted_iota(jnp.int32, (half, LIN_W), 0)
    zero_half = jnp.zeros((half, LIN_W), F32)
    parts = []
    for s in range(c):
        src = slice(base + s, base + s + 1)
        bs, ks = blk["bc2"][src], blk["kk"][src]
        for hf in (0, 1):
            if hf not in _hgrn_live_halves(d, s):
                parts.append(zero_half)
                continue
            tr = slice(base + hf * half, base + (hf + 1) * half)
            rid = row_half + hf * half
            keep = (rid >= s) if d == 0 else (rid <= s)
            parts.append(jnp.where(keep, blk["q"][tr] * jnp.exp2(blk["bc2"][tr] - bs) * ks, 0.0))
    return dict(cols=jnp.concatenate(parts, axis=0).astype(BF16), k_tail=(kk * jnp.exp(bl - bc)).astype(BF16),
                decay=jnp.exp(bl))


def _hgrn_chunk_launch(blk, cc, built, vt_ref, ones, st):
    rows = slice(cc * HG_CHUNK, (cc + 1) * HG_CHUNK)
    return dict(
        r=_dot(built["cols"], ones),
        o_inter=_dot_t(blk["qe"][rows], st.astype(BF16)),
        upd=_dot(vt_ref[0, :, rows].astype(BF16), built["k_tail"]),
        decay=built["decay"])


def _hgrn_chunk_finish(blk, cc, pend, seg_mask, st, o_ref):
    c = HG_CHUNK
    half = c // 2
    base = cc * c
    r = pend["r"]
    o_half = [pend["o_inter"][0:half], pend["o_inter"][half:c]]
    for s in range(c):
        vs = blk["v"][base + s:base + s + 1]
        for hf in _hgrn_live_halves(blk["d"], s):
            o_half[hf] = o_half[hf] + r[s * c + hf * half:s * c + (hf + 1) * half, :] * vs
    o_ref[0, base:base + half, :] = o_half[0]
    o_ref[0, base + half:base + c, :] = o_half[1]
    return st * pend["decay"] + pend["upd"] * seg_mask


def _hgrn_steps(lb, fwd_refs, bwd_refs, ones, st_ref):
    seg_mask = ones.astype(F32)
    (qf, vf, ff, vtf, of_ref), (qb, vb, fb, vtb, ob_ref) = fwd_refs, bwd_refs
    fwd = _hgrn_block(0, lb, qf, vf, ff)
    bwd = _hgrn_block(1, lb, qb, vb, fb)
    carry = {0: st_ref[0, 0], 1: st_ref[0, 1]}
    built, pending = {}, {}
    n_chunks = SCAN_BLOCK // HG_CHUNK
    chunk_of = {0: lambda step: step, 1: lambda step: n_chunks - 1 - step}
    side = {0: (fwd, vtf, of_ref), 1: (bwd, vtb, ob_ref)}

    def build(d, step):
        built[(d, step)] = _hgrn_chunk_build(side[d][0], chunk_of[d](step))

    def launch(d, step):
        blk, vt_ref, _ = side[d]
        pending[(d, step)] = _hgrn_chunk_launch(blk, chunk_of[d](step), built.pop((d, step)), vt_ref, ones, carry[d])

    def finish(d, step):
        blk, _, o_ref = side[d]
        carry[d] = _hgrn_chunk_finish(blk, chunk_of[d](step), pending.pop((d, step)), seg_mask, carry[d], o_ref)

    def flush():
        st_ref[0, 0] = carry[0]
        st_ref[0, 1] = carry[1]

    groups = []
    for slot in range(n_chunks + 2):
        group = []
        for phase, step in ((finish, slot - 2), (launch, slot - 1), (build, slot)):
            if 0 <= step < n_chunks:
                group += [functools.partial(phase, 0, step), functools.partial(phase, 1, step)]
        groups.append(group)
    return groups, flush


def _scan_kernel(layer,
                 gqf, gkf, gvkf, gktf, gabf, gqb, gkb, gvkb, gktb, gabb, gc_ref, gs0_ref,
                 hqf, hvf, hff, hvtf, hqb, hvb, hfb, hvtb, lb_ref, ones_ref, hs0_ref,
                 gof_ref, gob_ref, gs_ref, hof_ref, hob_ref, hst_ref):
    @pl.when(pl.program_id(1) == 0)
    def _():
        gs_ref[...] = gs0_ref[...]
        hst_ref[...] = hs0_ref[...]

    units = (_gdn_units(0, gqf, gkf, gvkf, gktf, gabf, gc_ref, gof_ref)
             + _gdn_units(1, gqb, gkb, gvkb, gktb, gabb, gc_ref, gob_ref))
    stages = _gdn_stages(units, gs_ref)
    groups, flush = _hgrn_steps(_hgrn_lower_bound(layer, lb_ref), (hqf, hvf, hff, hvtf, hof_ref),
                                (hqb, hvb, hfb, hvtb, hob_ref), ones_ref[...], hst_ref)
    per_stage = -(-len(groups) // len(stages))
    for stage in stages:
        stage()
        for group in groups[:per_stage]:
            for step in group:
                step()
        groups = groups[per_stage:]
    for group in groups:
        for step in group:
            step()
    flush()


def _lin_scan(layer, gq, gk, gvk, gkt, oa, gconst, gs0, oh, vt, lb, ones, hs0):
    b, t, _ = gq.shape
    blk = SCAN_BLOCK
    nb = t // blk

    def tok(cb, rev):
        return (lambda bi, i: (bi, nb - 1 - i, cb)) if rev else (lambda bi, i: (bi, i, cb))

    def lanes(rev):
        return (lambda bi, i: (bi, 0, nb - 1 - i)) if rev else (lambda bi, i: (bi, 0, i))

    def gdn_specs(rev):
        return [pl.BlockSpec((1, blk, LIN_W), tok(0, rev)),
                pl.BlockSpec((1, blk, LIN_W), tok(0, rev)),
                pl.BlockSpec((1, blk, LIN_W), tok(0, rev)),
                pl.BlockSpec((1, LIN_W, blk), lanes(rev)),
                pl.BlockSpec((1, blk, 128), tok(A_AB // 128, rev))]

    def hgrn_specs(rev):
        f_col = 3 if rev else 2
        return [pl.BlockSpec((1, blk, LIN_W), tok(0, rev)),
                pl.BlockSpec((1, blk, LIN_W), tok(1, rev)),
                pl.BlockSpec((1, blk, LIN_W), tok(f_col, rev)),
                pl.BlockSpec((1, LIN_W, blk), lanes(rev))]

    const = lambda r, c: pl.BlockSpec((r, c), lambda bi, i: (0, 0))
    g_state = pl.BlockSpec((1, 2 * HEADS, HD, HD), lambda bi, i: (bi, 0, 0, 0))
    h_state = pl.BlockSpec((1, 2, LIN_W, LIN_W), lambda bi, i: (bi, 0, 0, 0))
    out_tok = lambda rev: pl.BlockSpec((1, blk, LIN_W), tok(0, rev))
    seq = jax.ShapeDtypeStruct((b, t, LIN_W), F32)
    return pl.pallas_call(
        functools.partial(_scan_kernel, layer),
        grid=(b, nb),
        in_specs=gdn_specs(False) + gdn_specs(True) + [const(8, 128), g_state]
        + hgrn_specs(False) + hgrn_specs(True) + [const(8, LIN_W), const(LIN_W, LIN_W), h_state],
        out_specs=[out_tok(False), out_tok(True), g_state, out_tok(False), out_tok(True), h_state],
        out_shape=[seq, seq, jax.ShapeDtypeStruct((b, 2 * HEADS, HD, HD), F32),
                   seq, seq, jax.ShapeDtypeStruct((b, 2, LIN_W, LIN_W), F32)],
        compiler_params=pltpu.CompilerParams(dimension_semantics=("arbitrary", "arbitrary"),
                                             vmem_limit_bytes=VMEM_LIMIT),
        name="lin_scan",
    )(gq, gk, gvk, gkt, oa, gq, gk, gvk, gkt, oa, gconst, gs0,
      oh, oh, oh, vt, oh, oh, oh, vt, lb, ones, hs0)


def _post_kernel(x_ref, ogf_ref, ogb_ref, z_ref, gnw_ref, ohf_ref, ohb_ref, hg_ref, hnw_ref, om_ref, seg_ref,
                 wout_ref, gpm_ref, gt1_ref, gpf_ref, sc2_ref, sh2_ref, gt2_ref, gff_ref,
                 wa_ref, wb_ref, wo_ref, out_ref):
    seg = seg_ref[...]
    inv = 1.0 / HD
    og = ogf_ref[...] + ogb_ref[...]
    og = og * lax.rsqrt(_seg_sum(og * og, seg) * inv + EPS) * gnw_ref[...] * _silu(z_ref[...])
    oh = ohf_ref[...] + ohb_ref[...]
    oh = oh * lax.rsqrt(_seg_sum(oh * oh, seg) * inv + EPS) * hnw_ref[...] * _sigmoid(hg_ref[...])
    mix = jnp.concatenate([og.astype(BF16), oh.astype(BF16), om_ref[...].astype(BF16)], axis=-1)
    x1 = x_ref[...] + gt1_ref[0] * _rms(_dot(mix, wout_ref[...]), gpm_ref[...])
    hb = (_rms(x1, gpf_ref[...]) * (1.0 + sc2_ref[0]) + sh2_ref[0]).astype(BF16)
    fw = D_FF // FF_SPLIT
    y = None
    for part in range(FF_SPLIT):
        cs = slice(part * fw, (part + 1) * fw)
        act = (_silu(_dot(hb, wa_ref[:, cs])) * _dot(hb, wb_ref[:, cs])).astype(BF16)
        contrib = _dot(act, wo_ref[cs, :])
        y = contrib if y is None else y + contrib
    out_ref[...] = x1 + gt2_ref[0] * _rms(y, gff_ref[...])


def _post(x, ogf, ogb, og, gnw, ohf, ohb, oh, hnw, om, seg, wout, gpm, gt1, gpf, sc2, sh2, gt2, gff, wa, wb, wo, tm):
    m = x.shape[0]
    tiles = m // tm
    per = tiles // gt1.shape[0]
    row = lambda i: (i // per, 0, 0)
    tok = lambda w: pl.BlockSpec((tm, w), lambda i: (i, 0))
    vec = lambda w: pl.BlockSpec((1, w), lambda i: (0, 0))
    mod = pl.BlockSpec((1, 1, D_MODEL), row)
    once = lambda r, c: pl.BlockSpec((r, c), lambda i: (0, 0), pipeline_mode=pl.Buffered(1))
    return pl.pallas_call(
        _post_kernel,
        grid=(tiles,),
        in_specs=[tok(D_MODEL), tok(LIN_W), tok(LIN_W),
                  pl.BlockSpec((tm, LIN_W), lambda i: (i, G_W // LIN_W - 1)), vec(LIN_W),
                  tok(LIN_W), tok(LIN_W),
                  pl.BlockSpec((tm, LIN_W), lambda i: (i, H_W // LIN_W - 1)), vec(LIN_W),
                  tok(HEADS * MLA_DV), once(LIN_W, LIN_W),
                  once(D_MODEL, D_MODEL), vec(D_MODEL), mod, vec(D_MODEL), mod, mod, mod, vec(D_MODEL),
                  once(D_MODEL, D_FF), once(D_MODEL, D_FF), once(D_FF, D_MODEL)],
        out_specs=tok(D_MODEL),
        out_shape=jax.ShapeDtypeStruct((m, D_MODEL), F32),
        compiler_params=pltpu.CompilerParams(dimension_semantics=("arbitrary",), vmem_limit_bytes=VMEM_LIMIT),
        name="post",
    )(x, ogf, ogb, og, gnw, ohf, ohb, oh, hnw, om, seg, wout, gpm, gt1, gpf, sc2, sh2, gt2, gff, wa, wb, wo)


def _rope_tables(t_len, use_rope):
    scale = MLA_QK ** -0.5 * LOG2E
    if use_rope:
        rows = t_len // GRID_W
        row = jnp.repeat(jnp.arange(rows, dtype=F32), GRID_W)
        col = jnp.tile(jnp.arange(GRID_W, dtype=F32), rows)
        nf = MLA_ROPE // 4
        inv = ROPE_BASE ** (-jnp.arange(nf, dtype=F32) / nf)
        ar, ac = row[:, None] * inv, col[:, None] * inv
        cos = jnp.concatenate([jnp.cos(ar), jnp.cos(ar), jnp.cos(ac), jnp.cos(ac)], axis=-1)
        sin = jnp.concatenate([-jnp.sin(ar), jnp.sin(ar), -jnp.sin(ac), jnp.sin(ac)], axis=-1)
    else:
        cos = jnp.ones((t_len, MLA_ROPE), F32)
        sin = jnp.zeros((t_len, MLA_ROPE), F32)
    z64 = jnp.zeros((t_len, MLA_ROPE), F32)
    qa = scale * jnp.concatenate([jnp.ones((t_len, MLA_NOPE), F32), cos, z64], axis=-1)
    qb = scale * jnp.concatenate([jnp.zeros((t_len, MLA_NOPE), F32), sin, z64], axis=-1)
    ck = jnp.concatenate([cos, z64], axis=-1)
    sk = jnp.concatenate([sin, z64], axis=-1)
    return qa, qb, ck, sk


def _layer_weights(l, w_in, w_out, gdn_conv_w, gdn_a_log, gdn_dt_bias, gdn_norm_w, lower_bounds, hgrn_norm_w,
                   mla_q_norm_w, mla_w_uq, mla_kv_norm_w, mla_w_ukv, w_ffn_in, w_ffn_out,
                   g_pre_mix, g_post_mix, g_pre_ffn, g_post_ffn):
    wi = w_in[l].astype(BF16)
    w_cat = jnp.concatenate([wi[:, _PERM_G], wi[:, _PERM_H], wi[:, _PERM_A],
                             jnp.zeros((D_MODEL, A_W - _PERM_A.shape[0]), BF16)], axis=1)
    convw = jnp.concatenate([gdn_conv_w[l].T, jnp.zeros((3, G_CONV), F32)], axis=0)
    neg_a = -jnp.exp(gdn_a_log[l].astype(F32)).reshape(1, 8)
    dt = gdn_dt_bias[l].astype(F32).reshape(1, 8)
    pad = lambda r: jnp.pad(r, ((0, 0), (0, 128 - r.shape[1])))
    gconst = jnp.concatenate([pad(neg_a), pad(dt), jnp.zeros((6, 128), F32)], axis=0)
    vec = lambda a: a.reshape(1, -1).astype(F32)
    return dict(
        w_cat=w_cat, convw=convw, gconst=gconst,
        gnw=vec(jnp.tile(gdn_norm_w[l], HEADS)), hnw=vec(jnp.tile(hgrn_norm_w[l], HEADS)),
        layer=l, lb=lower_bounds,
        qnw=vec(mla_q_norm_w[l]), kvnw=vec(mla_kv_norm_w[l]),
        wuq=mla_w_uq[l][:, _PERM_UQ].astype(BF16), wukv=mla_w_ukv[l][:, _PERM_UKV].astype(BF16),
        wout=w_out[l].astype(BF16), wa=w_ffn_in[l][:, :D_FF].astype(BF16), wb=w_ffn_in[l][:, D_FF:].astype(BF16),
        wo=w_ffn_out[l].astype(BF16),
        g_pre_mix=vec(g_pre_mix[l]), g_post_mix=vec(g_post_mix[l]),
        g_pre_ffn=vec(g_pre_ffn[l]), g_post_ffn=vec(g_post_ffn[l]))


def _trunk_layer(x, mod, w, gdn_s0, hg_s0, tables, ctx_cache, seg512, seg256):
    b, t, _ = x.shape
    m = b * t
    tm = min(TOKEN_TILE, t)
    sh1, sc1, gt1, sh2, sc2, gt2 = mod
    xf = x.reshape(m, D_MODEL)
    og, oh, oa, vt = _premix(xf, w["g_pre_mix"], sc1, sh1, w["w_cat"], tm, t)

    qa, qb, ck, sk = tables
    qcat, kcat, vcat, ckv = _mla_prep(oa, w["qnw"], w["kvnw"], w["wuq"], w["wukv"], qa, qb, ck, sk, tm, t, ctx_cache)
    o_mla = _attn(qcat.reshape(b, t, HEADS * MLA_HEAD_PAD), kcat, vcat, min(ATTN_Q_TILE, t)).reshape(m, HEADS * MLA_DV)

    gq, gk, gvk, gkt = _gdn_prep(og, w["convw"], seg512, tm, t)
    ogf, ogb, gdn_state, ohf, ohb, hg_state = _lin_scan(
        w["layer"], gq.reshape(b, t, LIN_W), gk.reshape(b, t, LIN_W), gvk.reshape(b, t, LIN_W), gkt,
        oa.reshape(b, t, A_W), w["gconst"], gdn_s0,
        oh.reshape(b, t, H_W), vt, w["lb"], seg256, hg_s0)

    x_new = _post(xf, ogf.reshape(m, LIN_W), ogb.reshape(m, LIN_W), og, w["gnw"],
                  ohf.reshape(m, LIN_W), ohb.reshape(m, LIN_W), oh, w["hnw"], o_mla, seg256,
                  w["wout"], w["g_post_mix"], gt1, w["g_pre_ffn"], sc2, sh2, gt2, w["g_post_ffn"],
                  w["wa"], w["wb"], w["wo"], tm)
    mkr = oa[:, A_KR:A_KR + MLA_ROPE]
    return x_new.reshape(b, t, D_MODEL), gdn_state, hg_state, ckv.reshape(b, t, MLA_KV_RANK), mkr.reshape(b, t, MLA_ROPE)


def _hg_state_to_block(s):
    st = jnp.swapaxes(s, -1, -2)
    eye = jnp.eye(HEADS, dtype=s.dtype)
    big = st[:, :, :, :, None, :] * eye[None, None, :, None, :, None]
    return big.reshape(s.shape[0], 2, LIN_W, LIN_W)


def _hg_block_to_state(big):
    b = big.shape[0]
    r = big.reshape(b, 2, HEADS, HD, HEADS, HD)
    diag = jnp.stack([r[:, :, h, :, h, :] for h in range(HEADS)], axis=2)
    return jnp.swapaxes(diag, -1, -2)


def kernel(x_prompt, x_sample, cache_mla_ckv, cache_mla_krope, state_gdn, state_hgrn, c, c_ctx, w_ada, b_ada,
           g_pre_mix, g_post_mix, g_pre_ffn, g_post_ffn, w_in, w_out, gdn_conv_w, gdn_a_log, gdn_dt_bias,
           gdn_norm_w, hgrn_lb, hgrn_norm_w, mla_q_norm_w, mla_w_uq, mla_kv_norm_w, mla_w_ukv, w_ffn_in, w_ffn_out):
    bp, tp, _ = x_prompt.shape
    bd, td, _ = x_sample.shape
    past = cache_mla_ckv.shape[2]
    assert 1 + bd <= MOD_ROWS and tp % SCAN_BLOCK == 0 and td % SCAN_BLOCK == 0

    lower_bounds = jnp.pad(hgrn_lb.astype(F32), ((0, 8 - DEPTH), (0, 0)))

    cond = jnp.concatenate([c_ctx[None, :], c, jnp.zeros((MOD_ROWS - 1 - bd, D_MODEL), F32)], axis=0)
    mod_all = _ada(cond, w_ada, b_ada)

    seg512 = _seg_ones(2 * LIN_W)
    seg256 = _seg_ones(LIN_W)
    tab_ctx = _rope_tables(tp, False)
    tab_lat = _rope_tables(td, True)

    weights = [_layer_weights(l, w_in, w_out, gdn_conv_w, gdn_a_log, gdn_dt_bias, gdn_norm_w, lower_bounds,
                              hgrn_norm_w, mla_q_norm_w, mla_w_uq, mla_kv_norm_w, mla_w_ukv, w_ffn_in, w_ffn_out,
                              g_pre_mix, g_post_mix, g_pre_ffn, g_post_ffn) for l in range(DEPTH)]

    def mods(l, lo, n):
        rows = mod_all[l, lo:lo + n].reshape(n, 1, 6 * D_MODEL)
        return tuple(rows[:, :, i * D_MODEL:(i + 1) * D_MODEL] for i in range(6))

    xp = x_prompt
    zero_g = jnp.zeros((bp, 2 * HEADS, HD, HD), F32)
    zero_h = jnp.zeros((bp, 2, LIN_W, LIN_W), F32)
    ckv_l, kr_l, gs_l, hs_l = [], [], [], []
    for l in range(DEPTH):
        xp, gs, hs, ckv, mkr = _trunk_layer(xp, mods(l, 0, 1), weights[l], zero_g, zero_h, tab_ctx, None,
                                            seg512, seg256)
        ckv_l.append(ckv)
        kr_l.append(mkr)
        gs_l.append(gs.reshape(bp, 2, HEADS, HD, HD))
        hs_l.append(_hg_block_to_state(hs))

    xs = x_sample
    for l in range(DEPTH):
        ctx_cache = (cache_mla_ckv[:, l], jnp.pad(cache_mla_krope[:, l], ((0, 0), (0, 0), (0, 128 - MLA_ROPE))))
        xs, _, _, _, _ = _trunk_layer(xs, mods(l, 1, bd), weights[l],
                                      state_gdn[:, l].reshape(bd, 2 * HEADS, HD, HD),
                                      _hg_state_to_block(state_hgrn[:, l]), tab_lat, ctx_cache, seg512, seg256)

    return (xp, xs, jnp.stack(ckv_l, axis=1), jnp.stack(kr_l, axis=1),
            jnp.stack(gs_l, axis=1), jnp.stack(hs_l, axis=1))
```

```python
import functools
import math

import numpy as np
import jax
import jax.numpy as jnp
from jax import lax
from jax.experimental import pallas as pl
from jax.experimental.pallas import tpu as pltpu

F32 = jnp.float32
BF16 = jnp.bfloat16

D_MODEL = 1024
DEPTH = 2
GRID_W = 64
EPS = 1e-6
GATE_FLOOR = 1e-30
HEADS = 4
HD = 64
LIN_W = HEADS * HD
GDN_CHUNK = 64
HG_CHUNK = 16
SCAN_BLOCK = 256
LOG2E = 1.4426950408889634
MLA_Q_RANK = 384
MLA_KV_RANK = 256
MLA_NOPE = 128
MLA_ROPE = 64
MLA_DV = 128
MLA_QK = MLA_NOPE + MLA_ROPE
MLA_HEAD_PAD = 256
ROPE_BASE = 10000.0
D_FF = -(-8 * D_MODEL // (3 * 256)) * 256
FF_SPLIT = 2
TOKEN_TILE = 512
ATTN_Q_TILE = 256
MOD_ROWS = 16
VMEM_LIMIT = 56 * 1024 * 1024

_OFF = {}
_o = 0
for _n, _s in (("gq", 256), ("gk", 256), ("gv", 256), ("gz", 256), ("ga", 8), ("gb", 8),
               ("hq", 256), ("hi", 256), ("hf", 512), ("hg", 256),
               ("mcq", MLA_Q_RANK), ("mckv", MLA_KV_RANK), ("mkr", MLA_ROPE)):
    _OFF[_n] = _o
    _o += _s
IN_DIM = _o

G_W = 1024
G_CONV = 768
H_W = 1280
A_W = 896
A_KR = MLA_Q_RANK + MLA_KV_RANK
A_AB = A_KR + 2 * MLA_ROPE


def _swap_rope_halves(w):
    s = w.shape
    return w.reshape(s[:-1] + (2, 2, MLA_ROPE // 4))[..., ::-1, :].reshape(s)


def _w_in_layout(wi):
    return jnp.concatenate([wi[:, _OFF["gq"]:_OFF["ga"]], wi[:, _OFF["hq"]:_OFF["mcq"]], wi[:, _OFF["mcq"]:IN_DIM],
                            _swap_rope_halves(wi[:, _OFF["mkr"]:IN_DIM]), wi[:, _OFF["ga"]:_OFF["hq"]],
                            jnp.zeros((wi.shape[0], G_W + H_W + A_W - IN_DIM - MLA_ROPE), wi.dtype)], axis=1)


def _w_uq_layout(w):
    w = w.reshape(w.shape[0], HEADS, MLA_QK)
    rope = w[:, :, MLA_NOPE:]
    return jnp.concatenate([w[:, :, :MLA_NOPE], rope, _swap_rope_halves(rope)], axis=-1).reshape(w.shape[0], -1)


def _w_ukv_layout(w):
    w = w.reshape(w.shape[0], HEADS, MLA_NOPE + MLA_DV)
    return jnp.concatenate([w[:, :, :MLA_NOPE].reshape(w.shape[0], -1), w[:, :, MLA_NOPE:].reshape(w.shape[0], -1)],
                           axis=1)


def _seg_ones(n):
    i = np.arange(n) // HD
    return jnp.asarray((i[:, None] == i[None, :]).astype(np.float32), dtype=BF16)


def _rms(x, w):
    return x * lax.rsqrt(jnp.mean(x * x, axis=-1, keepdims=True) + EPS) * w


def _dot(a, b):
    return jnp.dot(a, b, preferred_element_type=F32)


def _dot_t(a, b):
    return lax.dot_general(a, b, (((1,), (1,)), ((), ())), preferred_element_type=F32)


def _split3(x):
    hi = x.astype(BF16)
    r1 = x - hi.astype(F32)
    mid = r1.astype(BF16)
    lo = (r1 - mid.astype(F32)).astype(BF16)
    return hi, mid, lo


def _split2(x):
    hi = x.astype(BF16)
    return hi, (x - hi.astype(F32)).astype(BF16)


def _seg_sum(x, seg):
    hi, lo = _split2(x)
    return _dot(hi, seg) + _dot(lo, seg)


def _mask_dot(mask01, x):
    hi, mid, lo = _split3(x)
    return _dot(mask01, hi) + _dot(mask01, mid) + _dot(mask01, lo)


def _dot_mask_t(x, mask01):
    hi, mid, lo = _split3(x)
    return _dot_t(hi, mask01) + _dot_t(mid, mask01) + _dot_t(lo, mask01)


def _softplus(x):
    return jnp.maximum(x, 0.0) + jnp.log1p(jnp.exp(-jnp.abs(x)))


def _sigmoid(x):
    return jax.nn.sigmoid(x)


def _silu(x):
    return x * jax.nn.sigmoid(x)


def _ada_kernel(c_ref, w_ref, b_ref, o_ref):
    s = _silu(c_ref[...]).astype(BF16)
    o_ref[0] = _dot(s, w_ref[0].astype(BF16)) + b_ref[0]


def _ada(cond, w_ada, b_ada):
    n = w_ada.shape[-1]
    tn = 1536
    return pl.pallas_call(
        _ada_kernel,
        grid=(DEPTH, n // tn),
        in_specs=[pl.BlockSpec((MOD_ROWS, D_MODEL), lambda l, j: (0, 0)),
                  pl.BlockSpec((1, D_MODEL, tn), lambda l, j: (l, 0, j)),
                  pl.BlockSpec((1, 1, tn), lambda l, j: (l, 0, j))],
        out_specs=pl.BlockSpec((1, MOD_ROWS, tn), lambda l, j: (l, 0, j)),
        out_shape=jax.ShapeDtypeStruct((DEPTH, MOD_ROWS, n), F32),
        compiler_params=pltpu.CompilerParams(dimension_semantics=("arbitrary", "arbitrary"),
                                             vmem_limit_bytes=VMEM_LIMIT),
        name="ada",
    )(cond, w_ada, b_ada.reshape(DEPTH, 1, n))


def _premix_kernel(x_ref, g_ref, sc_ref, sh_ref, w_ref, og_ref, oh_ref, oa_ref, vt_ref):
    h = _rms(x_ref[...], g_ref[...]) * (1.0 + sc_ref[0]) + sh_ref[0]
    hb = h.astype(BF16)
    og_ref[...] = _dot(hb, w_ref[:, 0:G_W])
    oh = _dot(hb, w_ref[:, G_W:G_W + H_W])
    oh_ref[...] = oh
    oa_ref[...] = _dot(hb, w_ref[:, G_W + H_W:G_W + H_W + A_W])
    vt_ref[0] = oh[:, LIN_W:2 * LIN_W].T


def _premix(x, g, sc, sh, w, tm, t_len):
    m = x.shape[0]
    tiles = m // tm
    per = tiles // sc.shape[0]
    row = lambda i: (i // per, 0, 0)
    tps = t_len // tm
    seq = lambda i: (i // tps, 0, i % tps)
    wt = G_W + H_W + A_W
    return pl.pallas_call(
        _premix_kernel,
        grid=(tiles,),
        in_specs=[pl.BlockSpec((tm, D_MODEL), lambda i: (i, 0)),
                  pl.BlockSpec((1, D_MODEL), lambda i: (0, 0)),
                  pl.BlockSpec((1, 1, D_MODEL), row),
                  pl.BlockSpec((1, 1, D_MODEL), row),
                  pl.BlockSpec((D_MODEL, wt), lambda i: (0, 0))],
        out_specs=[pl.BlockSpec((tm, G_W), lambda i: (i, 0)),
                   pl.BlockSpec((tm, H_W), lambda i: (i, 0)),
                   pl.BlockSpec((tm, A_W), lambda i: (i, 0)),
                   pl.BlockSpec((1, LIN_W, tm), seq)],
        out_shape=[jax.ShapeDtypeStruct((m, G_W), F32),
                   jax.ShapeDtypeStruct((m, H_W), F32),
                   jax.ShapeDtypeStruct((m, A_W), F32),
                   jax.ShapeDtypeStruct((m // t_len, LIN_W, t_len), F32)],
        compiler_params=pltpu.CompilerParams(dimension_semantics=("arbitrary",), vmem_limit_bytes=VMEM_LIMIT),
        name="premix",
    )(x, g, sc, sh, w)


def _keys_from_latent(ckv, krot, wukv_ref, k_out, v_out):
    kv = _dot(ckv.astype(BF16), wukv_ref[...])
    for h in range(HEADS):
        k_out[:, MLA_HEAD_PAD * h:MLA_HEAD_PAD * h + MLA_NOPE] = kv[:, MLA_NOPE * h:MLA_NOPE * (h + 1)].astype(BF16)
        k_out[:, MLA_HEAD_PAD * h + MLA_NOPE:MLA_HEAD_PAD * (h + 1)] = krot
    v_out[...] = kv[:, HEADS * MLA_NOPE:].astype(BF16)


def _mla_prep_kernel(tps, past, a_ref, qnw_ref, kvnw_ref, wuq_ref, wukv_ref, qa_ref, qb_ref, ck_ref, sk_ref, *refs):
    if past:
        cache_ckv_ref, cache_kr_ref, q_out, k_out, v_out, ckv_out = refs
    else:
        q_out, k_out, v_out, ckv_out = refs
    j = pl.program_id(1)

    @pl.when(j < tps)
    def _():
        qn = _rms(a_ref[:, 0:MLA_Q_RANK], qnw_ref[...])
        y = _dot(qn.astype(BF16), wuq_ref[...])
        z = pltpu.roll(y, HEADS * MLA_HEAD_PAD - MLA_ROPE, 1)
        qa = qa_ref[...]
        qb = qb_ref[...]
        for h in range(HEADS):
            sl = slice(MLA_HEAD_PAD * h, MLA_HEAD_PAD * (h + 1))
            q_out[:, sl] = (y[:, sl] * qa + z[:, sl] * qb).astype(BF16)
        ckv = _rms(a_ref[:, MLA_Q_RANK:A_KR], kvnw_ref[...])
        ckv_out[...] = ckv
        kr = a_ref[:, A_KR:A_AB]
        krot = (kr * ck_ref[...] + pltpu.roll(kr, 64, 1) * sk_ref[...]).astype(BF16)
        _keys_from_latent(ckv, krot, wukv_ref, k_out.at[0], v_out.at[0])

    if past:
        @pl.when(j == tps)
        def _():
            _keys_from_latent(cache_ckv_ref[0], cache_kr_ref[0].astype(BF16), wukv_ref,
                              k_out.at[0, 0:past], v_out.at[0, 0:past])


def _mla_prep(oa, qnw, kvnw, wuq, wukv, qa, qb, ck, sk, tm, t_len, cache=None):
    m = oa.shape[0]
    b = m // t_len
    tps = t_len // tm
    past = 0 if cache is None else cache[0].shape[1]
    tok = lambda bi, j: (bi * tps + jnp.minimum(j, tps - 1), 0)
    pos = lambda bi, j: (jnp.minimum(j, tps - 1), 0)
    full = lambda bi, j: (0, 0)
    seq = lambda bi, j: (bi, j, 0)
    kw = HEADS * MLA_HEAD_PAD
    in_specs = [pl.BlockSpec((tm, A_W), tok),
                pl.BlockSpec((1, MLA_Q_RANK), full),
                pl.BlockSpec((1, MLA_KV_RANK), full),
                pl.BlockSpec((MLA_Q_RANK, kw), full),
                pl.BlockSpec((MLA_KV_RANK, HEADS * (MLA_NOPE + MLA_DV)), full),
                pl.BlockSpec((tm, MLA_HEAD_PAD), pos),
                pl.BlockSpec((tm, MLA_HEAD_PAD), pos),
                pl.BlockSpec((tm, 128), pos),
                pl.BlockSpec((tm, 128), pos)]
    operands = [oa, qnw, kvnw, wuq, wukv, qa, qb, ck, sk]
    if past:
        in_specs += [pl.BlockSpec((1, past, MLA_KV_RANK), lambda bi, j: (bi, 0, 0)),
                     pl.BlockSpec((1, past, 128), lambda bi, j: (bi, 0, 0))]
        operands += list(cache)
    return pl.pallas_call(
        functools.partial(_mla_prep_kernel, tps, past),
        grid=(b, tps + (1 if past else 0)),
        in_specs=in_specs,
        out_specs=[pl.BlockSpec((tm, kw), tok),
                   pl.BlockSpec((1, tm, kw), seq),
                   pl.BlockSpec((1, tm, HEADS * MLA_DV), seq),
                   pl.BlockSpec((tm, MLA_KV_RANK), tok)],
        out_shape=[jax.ShapeDtypeStruct((m, kw), BF16),
                   jax.ShapeDtypeStruct((b, t_len + past, kw), BF16),
                   jax.ShapeDtypeStruct((b, t_len + past, HEADS * MLA_DV), BF16),
                   jax.ShapeDtypeStruct((m, MLA_KV_RANK), F32)],
        compiler_params=pltpu.CompilerParams(dimension_semantics=("arbitrary", "arbitrary"),
                                             vmem_limit_bytes=VMEM_LIMIT),
        name="mla_prep",
    )(*operands)


def _attn_kernel(q_ref, k_ref, v_ref, o_ref):
    for h in range(HEADS):
        sl = slice(MLA_HEAD_PAD * h, MLA_HEAD_PAD * (h + 1))
        s = _dot_t(q_ref[0, :, sl], k_ref[0, :, sl])
        p = jnp.exp2(s - jnp.max(s, axis=-1, keepdims=True))
        l = jnp.sum(p, axis=-1, keepdims=True)
        o = _dot(p.astype(BF16), v_ref[0, :, MLA_DV * h:MLA_DV * (h + 1)])
        o_ref[0, :, MLA_DV * h:MLA_DV * (h + 1)] = o / l


def _attn(q, k, v, tq):
    b, t, kw = q.shape
    s = k.shape[1]
    return pl.pallas_call(
        _attn_kernel,
        grid=(b, t // tq),
        in_specs=[pl.BlockSpec((1, tq, kw), lambda bi, i: (bi, i, 0)),
                  pl.BlockSpec((1, s, kw), lambda bi, i: (bi, 0, 0)),
                  pl.BlockSpec((1, s, HEADS * MLA_DV), lambda bi, i: (bi, 0, 0))],
        out_specs=pl.BlockSpec((1, tq, HEADS * MLA_DV), lambda bi, i: (bi, i, 0)),
        out_shape=jax.ShapeDtypeStruct((b, t, HEADS * MLA_DV), F32),
        compiler_params=pltpu.CompilerParams(dimension_semantics=("arbitrary", "arbitrary"),
                                             vmem_limit_bytes=VMEM_LIMIT),
        name="attn",
    )(q, k, v)


def _gdn_prep_kernel(tps, cur_ref, prev_ref, next_ref, w_ref, seg_ref, q_out, k_out, v_out, kt_out):
    i = pl.program_id(0)
    tm = cur_ref.shape[0]
    has_prev = ((i % tps) != 0).astype(F32)
    has_next = ((i % tps) != (tps - 1)).astype(F32)
    xc = jnp.concatenate([prev_ref[...] * has_prev, cur_ref[...], next_ref[...] * has_next], axis=0)
    n = tm + 16
    y = xc[8:8 + tm] * w_ref[2:3, :]
    for j in (0, 1, 3, 4):
        d = j - 2
        y = y + pltpu.roll(xc, (n - d) % n, 0)[8:8 + tm] * w_ref[j:j + 1, :]
    y = _silu(y)
    seg = seg_ref[...]
    qk = y[:, 0:2 * LIN_W]
    qk = qk * lax.rsqrt(_seg_sum(qk * qk, seg) + EPS)
    q_out[...] = qk[:, 0:LIN_W] * (HD ** -0.5)
    k_out[...] = qk[:, LIN_W:2 * LIN_W]
    kt_out[0] = qk[:, LIN_W:2 * LIN_W].T
    v_out[...] = y[:, 2 * LIN_W:]


def _gdn_prep(og, convw, seg, tm, t_len):
    m = og.shape[0]
    tiles = m // tm
    tps = t_len // tm
    r8 = tm // 8
    last8 = m // 8 - 1
    return pl.pallas_call(
        functools.partial(_gdn_prep_kernel, tps),
        grid=(tiles,),
        in_specs=[pl.BlockSpec((tm, G_CONV), lambda i: (i, 0)),
                  pl.BlockSpec((8, G_CONV), lambda i: (jnp.maximum(i * r8 - 1, 0), 0)),
                  pl.BlockSpec((8, G_CONV), lambda i: (jnp.minimum((i + 1) * r8, last8), 0)),
                  pl.BlockSpec((8, G_CONV), lambda i: (0, 0)),
                  pl.BlockSpec((2 * LIN_W, 2 * LIN_W), lambda i: (0, 0))],
        out_specs=[pl.BlockSpec((tm, LIN_W), lambda i: (i, 0)),
                   pl.BlockSpec((tm, LIN_W), lambda i: (i, 0)),
                   pl.BlockSpec((tm, LIN_W), lambda i: (i, 0)),
                   pl.BlockSpec((1, LIN_W, tm), lambda i: (i // tps, 0, i % tps))],
        out_shape=[jax.ShapeDtypeStruct((m, LIN_W), F32),
                   jax.ShapeDtypeStruct((m, LIN_W), F32),
                   jax.ShapeDtypeStruct((m, LIN_W), F32),
                   jax.ShapeDtypeStruct((m // t_len, LIN_W, t_len), F32)],
        compiler_params=pltpu.CompilerParams(dimension_semantics=("arbitrary",), vmem_limit_bytes=VMEM_LIMIT),
        name="gdn_prep",
    )(og, og, og, convw, seg)


def _cumsum_masks():
    r = np.arange(SCAN_BLOCK)[:, None]
    c = np.arange(SCAN_BLOCK)[None, :]
    masks = []
    for chunk in (GDN_CHUNK, HG_CHUNK):
        same = (r // chunk) == (c // chunk)
        masks += [same & (r >= c), same & (r <= c)]
    return jnp.asarray(np.stack(masks).astype(np.float32), dtype=BF16)


def _gdn_units(d, incl_blk, q_ref, k_ref, v_ref, kt_ref, ab_ref, gc_ref, o_ref):
    c = GDN_CHUNK
    ri = lax.broadcasted_iota(jnp.int32, (c, c), 0)
    ci = lax.broadcasted_iota(jnp.int32, (c, c), 1)
    incl = (ri >= ci) if d == 0 else (ri <= ci)
    strict = (ri > ci) if d == 0 else (ri < ci)
    last = c - 1 if d == 0 else 0
    ab = ab_ref[0]
    g_cols = gc_ref[0:1, :] * _softplus(ab + gc_ref[1:2, :])
    beta = _sigmoid(ab)
    gcs_blk = _mask_dot(incl_blk, g_cols)
    gcr_blk = gcs_blk.T
    n_chunks = SCAN_BLOCK // c
    units = []
    for rank, cc in enumerate(range(n_chunks) if d == 0 else range(n_chunks - 1, -1, -1)):
        rows = slice(cc * c, (cc + 1) * c)
        for h in range(HEADS):
            j = d * HEADS + h
            hs = slice(HD * h, HD * (h + 1))
            gcol = gcs_blk[rows, j:j + 1]
            grow = gcr_blk[j:j + 1, rows]
            bcol = beta[rows, 8 + j:9 + j]
            gl = gcol[last:last + 1, :]
            eg = jnp.exp(gcol)
            k = k_ref[0, rows, hs]
            units.append(dict(
                rank=rank, j=j, rows=rows, hs=hs, o_ref=o_ref, strict=strict, gl=gl, bcol=bcol, eg=eg,
                q=q_ref[0, rows, hs], k=k,
                k_tail=(kt_ref[0, hs, rows] * jnp.exp(gl - grow)).astype(BF16),
                decay=jnp.where(incl, jnp.exp(jnp.where(incl, gcol - grow, 0.0)), 0.0),
                x=jnp.concatenate([v_ref[0, rows, hs] * bcol, k * (bcol * eg)], axis=1)))
    return units


def _gdn_stages(units, s_ref):
    c = GDN_CHUNK
    ri = lax.broadcasted_iota(jnp.int32, (c, c), 0)
    ci = lax.broadcasted_iota(jnp.int32, (c, c), 1)
    diag_blk = jnp.right_shift(ri, 4) == jnp.right_shift(ci, 4)
    eye = (ri == ci).astype(F32)

    def gram():
        for u in units:
            kb16 = u["k"].astype(BF16)
            u["kk"] = _dot_t((u["k"] * u["bcol"]).astype(BF16), kb16)
            u["qk"] = _dot_t(u["q"].astype(BF16), kb16)

    def split():
        for u in units:
            p = jnp.where(u["strict"], u["kk"] * u["decay"], 0.0)
            pd = jnp.where(diag_blk, p, 0.0)
            u["att"] = (u["qk"] * u["decay"]).astype(BF16)
            u["poff"] = (p - pd).astype(BF16)
            u["td"] = eye - pd
            u["a"] = pd.astype(BF16)

    def square():
        for u in units:
            u["a"] = _dot(u["a"], u["a"]).astype(BF16)

    def extend():
        for u in units:
            u["td"] = u["td"] + _dot(u["td"].astype(BF16), u["a"])

    def apply_diag():
        for u in units:
            yn = _dot(u["td"].astype(BF16), jnp.concatenate([u["x"].astype(BF16), u["poff"]], axis=1))
            u["y"] = yn[:, 0:2 * HD]
            u["n"] = yn[:, 2 * HD:3 * HD].astype(BF16)
            u["x"] = u["y"]

    def substitute():
        for u in units:
            u["x"] = u["y"] - _dot(u["n"], u["x"].astype(BF16))

    def new_values(rank):
        for u in units:
            if u["rank"] == rank:
                u["s"] = s_ref[0, u["j"]]
                u["sb"] = u["s"].astype(BF16)
                u["vb"] = (u["x"][:, 0:HD] - _dot(u["x"][:, HD:2 * HD].astype(BF16), u["sb"])).astype(BF16)

    def emit(rank):
        for u in units:
            if u["rank"] == rank:
                u["o_ref"][0, u["rows"], u["hs"]] = (_dot((u["q"] * u["eg"]).astype(BF16), u["sb"])
                                                     + _dot(u["att"], u["vb"]))
                s_ref[0, u["j"]] = u["s"] * jnp.exp(u["gl"]) + _dot(u["k_tail"], u["vb"])

    stages = [gram, split] + [square, extend] * 3 + [apply_diag] + [substitute] * 3
    for rank in range(SCAN_BLOCK // c):
        stages += [functools.partial(new_values, rank), functools.partial(emit, rank)]
    return stages


def _hgrn_lower_bound(layer, lb_ref):
    raw = lb_ref[0:DEPTH, :]
    e = jnp.exp(raw - jnp.max(raw, axis=0, keepdims=True))
    gamma = e / jnp.sum(e, axis=0, keepdims=True)
    lb = jnp.zeros((1, LIN_W), F32)
    for i in range(1, layer + 1):
        lb = lb + gamma[i:i + 1, :]
    return lb


def _hgrn_block(d, incl01, lb, q_ref, v_ref, f_ref):
    c = HG_CHUNK
    oml = 1.0 - lb
    q = q_ref[0]
    f = f_ref[0]
    logf = jnp.log(jnp.maximum(lb + oml * _sigmoid(f), GATE_FLOOR))
    bc = _mask_dot(incl01, logf)
    bc2 = bc * LOG2E
    src2 = bc2 - (jnp.log2(oml) - _softplus(f) * LOG2E)
    return dict(d=d, q=q, v=v_ref[0], kk=oml * _sigmoid(-f), bc=bc, bc2=bc2, src2=src2,
                qe=(q * jnp.exp(bc)).astype(BF16), last=c - 1 if d == 0 else 0)


def _hgrn_live_halves(d, s):
    half = HG_CHUNK // 2
    if d == 0:
        return (1,) if s >= half else (0, 1)
    return (0,) if s < half else (0, 1)


def _hgrn_chunk_build(blk, cc):
    c = HG_CHUNK
    half = c // 2
    d = blk["d"]
    base = cc * c
    rows = slice(base, base + c)
    kk, bc = blk["kk"][rows], blk["bc"][rows]
    bl = bc[blk["last"]:blk["last"] + 1, :]
    row_half = lax.broadcasted_iota(jnp.int32, (half, LIN_W), 0)
    zero_half = jnp.zeros((half, LIN_W), F32)
    parts = []
    for s in range(c):
        bs = blk["src2"][base + s:base + s + 1]
        for hf in (0, 1):
            if hf not in _hgrn_live_halves(d, s):
                parts.append(zero_half)
                continue
            tr = slice(base + hf * half, base + (hf + 1) * half)
            col = blk["q"][tr] * jnp.exp2(blk["bc2"][tr] - bs)
            if hf == s // half:
                rid = row_half + hf * half
                col = jnp.where((rid >= s) if d == 0 else (rid <= s), col, 0.0)
            parts.append(col)
    return dict(cols=jnp.concatenate(parts, axis=0).astype(BF16), k_tail=(kk * jnp.exp(bl - bc)).astype(BF16),
                decay=jnp.exp(bl))


def _hgrn_chunk_launch(blk, cc, built, vt_ref, ones, st):
    rows = slice(cc * HG_CHUNK, (cc + 1) * HG_CHUNK)
    return dict(
        r=_dot(built["cols"], ones),
        o_inter=_dot_t(blk["qe"][rows], st.astype(BF16)),
        upd=_dot(vt_ref[0, :, rows].astype(BF16), built["k_tail"]),
        decay=built["decay"])


def _hgrn_chunk_finish(blk, cc, pend, seg_mask, st, o_ref):
    c = HG_CHUNK
    half = c // 2
    base = cc * c
    r = pend["r"]
    o_half = [pend["o_inter"][0:half], pend["o_inter"][half:c]]
    for s in range(c):
        vs = blk["v"][base + s:base + s + 1]
        for hf in _hgrn_live_halves(blk["d"], s):
            o_half[hf] = o_half[hf] + r[s * c + hf * half:s * c + (hf + 1) * half, :] * vs
    o_ref[0, base:base + half, :] = o_half[0]
    o_ref[0, base + half:base + c, :] = o_half[1]
    return st * pend["decay"] + pend["upd"] * seg_mask


def _hgrn_steps(lb, masks_ref, fwd_refs, bwd_refs, ones, st_ref):
    seg_mask = ones.astype(F32)
    (qf, vf, ff, vtf, of_ref), (qb, vb, fb, vtb, ob_ref) = fwd_refs, bwd_refs
    fwd = _hgrn_block(0, masks_ref[2], lb, qf, vf, ff)
    bwd = _hgrn_block(1, masks_ref[3], lb, qb, vb, fb)
    carry = {0: st_ref[0, 0], 1: st_ref[0, 1]}
    built, pending = {}, {}
    n_chunks = SCAN_BLOCK // HG_CHUNK
    chunk_of = {0: lambda step: step, 1: lambda step: n_chunks - 1 - step}
    side = {0: (fwd, vtf, of_ref), 1: (bwd, vtb, ob_ref)}

    def build(d, step):
        built[(d, step)] = _hgrn_chunk_build(side[d][0], chunk_of[d](step))

    def launch(d, step):
        blk, vt_ref, _ = side[d]
        pending[(d, step)] = _hgrn_chunk_launch(blk, chunk_of[d](step), built.pop((d, step)), vt_ref, ones, carry[d])

    def finish(d, step):
        blk, _, o_ref = side[d]
        carry[d] = _hgrn_chunk_finish(blk, chunk_of[d](step), pending.pop((d, step)), seg_mask, carry[d], o_ref)

    def flush():
        st_ref[0, 0] = carry[0]
        st_ref[0, 1] = carry[1]

    groups = []
    for slot in range(n_chunks + 2):
        group = []
        for phase, step in ((finish, slot - 2), (launch, slot - 1), (build, slot)):
            if 0 <= step < n_chunks:
                group += [functools.partial(phase, 0, step), functools.partial(phase, 1, step)]
        groups.append(group)
    return groups, flush


def _scan_kernel(layer, masks_ref,
                 gqf, gkf, gvkf, gktf, gabf, gqb, gkb, gvkb, gktb, gabb, gc_ref, gs0_ref,
                 hqf, hvf, hff, hvtf, hqb, hvb, hfb, hvtb, lb_ref, ones_ref, hs0_ref,
                 gof_ref, gob_ref, gs_ref, hof_ref, hob_ref, hst_ref):
    @pl.when(pl.program_id(1) == 0)
    def _():
        gs_ref[...] = gs0_ref[...]
        hst_ref[...] = hs0_ref[...]

    units = (_gdn_units(0, masks_ref[0], gqf, gkf, gvkf, gktf, gabf, gc_ref, gof_ref)
             + _gdn_units(1, masks_ref[1], gqb, gkb, gvkb, gktb, gabb, gc_ref, gob_ref))
    stages = _gdn_stages(units, gs_ref)
    groups, flush = _hgrn_steps(_hgrn_lower_bound(layer, lb_ref), masks_ref, (hqf, hvf, hff, hvtf, hof_ref),
                                (hqb, hvb, hfb, hvtb, hob_ref), ones_ref[...], hst_ref)
    per_stage = -(-len(groups) // len(stages))
    for stage in stages:
        stage()
        for group in groups[:per_stage]:
            for step in group:
                step()
        groups = groups[per_stage:]
    for group in groups:
        for step in group:
            step()
    flush()


def _lin_scan(layer, gq, gk, gvk, gkt, oa, gconst, gs0, oh, vt, lb, ones, hs0):
    b, t, _ = gq.shape
    blk = SCAN_BLOCK
    nb = t // blk

    def tok(cb, rev):
        return (lambda bi, i: (bi, nb - 1 - i, cb)) if rev else (lambda bi, i: (bi, i, cb))

    def lanes(rev):
        return (lambda bi, i: (bi, 0, nb - 1 - i)) if rev else (lambda bi, i: (bi, 0, i))

    def gdn_specs(rev):
        return [pl.BlockSpec((1, blk, LIN_W), tok(0, rev)),
                pl.BlockSpec((1, blk, LIN_W), tok(0, rev)),
                pl.BlockSpec((1, blk, LIN_W), tok(0, rev)),
                pl.BlockSpec((1, LIN_W, blk), lanes(rev)),
                pl.BlockSpec((1, blk, 128), tok(A_AB // 128, rev))]

    def hgrn_specs(rev):
        f_col = 3 if rev else 2
        return [pl.BlockSpec((1, blk, LIN_W), tok(0, rev)),
                pl.BlockSpec((1, blk, LIN_W), tok(1, rev)),
                pl.BlockSpec((1, blk, LIN_W), tok(f_col, rev)),
                pl.BlockSpec((1, LIN_W, blk), lanes(rev))]

    const = lambda r, c: pl.BlockSpec((r, c), lambda bi, i: (0, 0))
    g_state = pl.BlockSpec((1, 2 * HEADS, HD, HD), lambda bi, i: (bi, 0, 0, 0))
    h_state = pl.BlockSpec((1, 2, LIN_W, LIN_W), lambda bi, i: (bi, 0, 0, 0))
    out_tok = lambda rev: pl.BlockSpec((1, blk, LIN_W), tok(0, rev))
    seq = jax.ShapeDtypeStruct((b, t, LIN_W), F32)
    return pl.pallas_call(
        functools.partial(_scan_kernel, layer),
        grid=(b, nb),
        in_specs=[pl.BlockSpec((4, blk, blk), lambda bi, i: (0, 0, 0))]
        + gdn_specs(False) + gdn_specs(True) + [const(8, 128), g_state]
        + hgrn_specs(False) + hgrn_specs(True) + [const(8, LIN_W), const(LIN_W, LIN_W), h_state],
        out_specs=[out_tok(False), out_tok(True), g_state, out_tok(False), out_tok(True), h_state],
        out_shape=[seq, seq, jax.ShapeDtypeStruct((b, 2 * HEADS, HD, HD), F32),
                   seq, seq, jax.ShapeDtypeStruct((b, 2, LIN_W, LIN_W), F32)],
        compiler_params=pltpu.CompilerParams(dimension_semantics=("arbitrary", "arbitrary"),
                                             vmem_limit_bytes=VMEM_LIMIT),
        name="lin_scan",
    )(_cumsum_masks(), gq, gk, gvk, gkt, oa, gq, gk, gvk, gkt, oa, gconst, gs0,
      oh, oh, oh, vt, oh, oh, oh, vt, lb, ones, hs0)


def _post_kernel(x_ref, ogf_ref, ogb_ref, z_ref, gnw_ref, ohf_ref, ohb_ref, hg_ref, hnw_ref, om_ref, seg_ref,
                 wout_ref, gpm_ref, gt1_ref, gpf_ref, sc2_ref, sh2_ref, gt2_ref, gff_ref,
                 wf_ref, wo_ref, out_ref):
    seg = seg_ref[...]
    inv = 1.0 / HD
    og = ogf_ref[...] + ogb_ref[...]
    og = og * lax.rsqrt(_seg_sum(og * og, seg) * inv + EPS) * gnw_ref[...] * _silu(z_ref[...])
    oh = ohf_ref[...] + ohb_ref[...]
    oh = oh * lax.rsqrt(_seg_sum(oh * oh, seg) * inv + EPS) * hnw_ref[...] * _sigmoid(hg_ref[...])
    mix = jnp.concatenate([og.astype(BF16), oh.astype(BF16), om_ref[...].astype(BF16)], axis=-1)
    x1 = x_ref[...] + gt1_ref[0] * _rms(_dot(mix, wout_ref[...]), gpm_ref[...])
    hb = (_rms(x1, gpf_ref[...]) * (1.0 + sc2_ref[0]) + sh2_ref[0]).astype(BF16)
    fw = D_FF // FF_SPLIT
    y = None
    for part in range(FF_SPLIT):
        cs = slice(part * fw, (part + 1) * fw)
        gs = slice(D_FF + part * fw, D_FF + (part + 1) * fw)
        act = (_silu(_dot(hb, wf_ref[:, cs])) * _dot(hb, wf_ref[:, gs])).astype(BF16)
        contrib = _dot(act, wo_ref[cs, :])
        y = contrib if y is None else y + contrib
    out_ref[...] = x1 + gt2_ref[0] * _rms(y, gff_ref[...])


def _post(x, ogf, ogb, og, gnw, ohf, ohb, oh, hnw, om, seg, wout, gpm, gt1, gpf, sc2, sh2, gt2, gff, wf, wo, tm):
    m = x.shape[0]
    tiles = m // tm
    per = tiles // gt1.shape[0]
    row = lambda i: (i // per, 0, 0)
    tok = lambda w: pl.BlockSpec((tm, w), lambda i: (i, 0))
    vec = lambda w: pl.BlockSpec((1, w), lambda i: (0, 0))
    mod = pl.BlockSpec((1, 1, D_MODEL), row)
    once = lambda r, c: pl.BlockSpec((r, c), lambda i: (0, 0), pipeline_mode=pl.Buffered(1))
    return pl.pallas_call(
        _post_kernel,
        grid=(tiles,),
        in_specs=[tok(D_MODEL), tok(LIN_W), tok(LIN_W),
                  pl.BlockSpec((tm, LIN_W), lambda i: (i, G_W // LIN_W - 1)), vec(LIN_W),
                  tok(LIN_W), tok(LIN_W),
                  pl.BlockSpec((tm, LIN_W), lambda i: (i, H_W // LIN_W - 1)), vec(LIN_W),
                  tok(HEADS * MLA_DV), once(LIN_W, LIN_W),
                  once(D_MODEL, D_MODEL), vec(D_MODEL), mod, vec(D_MODEL), mod, mod, mod, vec(D_MODEL),
                  once(D_MODEL, 2 * D_FF), once(D_FF, D_MODEL)],
        out_specs=tok(D_MODEL),
        out_shape=jax.ShapeDtypeStruct((m, D_MODEL), F32),
        compiler_params=pltpu.CompilerParams(dimension_semantics=("arbitrary",), vmem_limit_bytes=VMEM_LIMIT),
        name="post",
    )(x, ogf, ogb, og, gnw, ohf, ohb, oh, hnw, om, seg, wout, gpm, gt1, gpf, sc2, sh2, gt2, gff, wf, wo)


def _rope_tables(t_len, use_rope):
    scale = MLA_QK ** -0.5 * LOG2E
    if use_rope:
        rows = t_len // GRID_W
        row = np.repeat(np.arange(rows, dtype=np.float64), GRID_W)
        col = np.tile(np.arange(GRID_W, dtype=np.float64), rows)
        nf = MLA_ROPE // 4
        inv = ROPE_BASE ** (-np.arange(nf, dtype=np.float64) / nf)
        ar, ac = row[:, None] * inv, col[:, None] * inv
        cos = np.concatenate([np.cos(ar), np.cos(ar), np.cos(ac), np.cos(ac)], axis=-1)
        sin = np.concatenate([-np.sin(ar), np.sin(ar), -np.sin(ac), np.sin(ac)], axis=-1)
    else:
        cos = np.ones((t_len, MLA_ROPE))
        sin = np.zeros((t_len, MLA_ROPE))
    z64 = np.zeros((t_len, MLA_ROPE))
    qa = scale * np.concatenate([np.ones((t_len, MLA_NOPE)), cos, z64], axis=-1)
    qb = scale * np.concatenate([np.zeros((t_len, MLA_NOPE)), sin, z64], axis=-1)
    ck = np.concatenate([cos, z64], axis=-1)
    sk = np.concatenate([sin, z64], axis=-1)
    return tuple(jnp.asarray(a, dtype=F32) for a in (qa, qb, ck, sk))


def _layer_weights(l, w_in, w_out, gdn_conv_w, gdn_a_log, gdn_dt_bias, gdn_norm_w, lower_bounds, hgrn_norm_w,
                   mla_q_norm_w, mla_w_uq, mla_kv_norm_w, mla_w_ukv, w_ffn_in, w_ffn_out,
                   g_pre_mix, g_post_mix, g_pre_ffn, g_post_ffn):
    w_cat = _w_in_layout(w_in[l].astype(BF16))
    convw = jnp.concatenate([gdn_conv_w[l].T, jnp.zeros((3, G_CONV), F32)], axis=0)
    neg_a = -jnp.exp(gdn_a_log[l].astype(F32)).reshape(1, 8)
    dt = gdn_dt_bias[l].astype(F32).reshape(1, 8)
    pad = lambda r: jnp.pad(r, ((0, 0), (0, 128 - r.shape[1])))
    gconst = jnp.concatenate([pad(neg_a), pad(dt), jnp.zeros((6, 128), F32)], axis=0)
    vec = lambda a: a.reshape(1, -1).astype(F32)
    return dict(
        w_cat=w_cat, convw=convw, gconst=gconst,
        gnw=vec(jnp.tile(gdn_norm_w[l], HEADS)), hnw=vec(jnp.tile(hgrn_norm_w[l], HEADS)),
        layer=l, lb=lower_bounds,
        qnw=vec(mla_q_norm_w[l]), kvnw=vec(mla_kv_norm_w[l]),
        wuq=_w_uq_layout(mla_w_uq[l].astype(BF16)), wukv=_w_ukv_layout(mla_w_ukv[l].astype(BF16)),
        wout=w_out[l].astype(BF16), wf=w_ffn_in[l].astype(BF16),
        wo=w_ffn_out[l].astype(BF16),
        g_pre_mix=vec(g_pre_mix[l]), g_post_mix=vec(g_post_mix[l]),
        g_pre_ffn=vec(g_pre_ffn[l]), g_post_ffn=vec(g_post_ffn[l]))


def _trunk_layer(x, mod, w, gdn_s0, hg_s0, tables, ctx_cache, seg512, seg256):
    b, t, _ = x.shape
    m = b * t
    tm = min(TOKEN_TILE, t)
    sh1, sc1, gt1, sh2, sc2, gt2 = mod
    xf = x.reshape(m, D_MODEL)
    og, oh, oa, vt = _premix(xf, w["g_pre_mix"], sc1, sh1, w["w_cat"], tm, t)

    qa, qb, ck, sk = tables
    qcat, kcat, vcat, ckv = _mla_prep(oa, w["qnw"], w["kvnw"], w["wuq"], w["wukv"], qa, qb, ck, sk, tm, t, ctx_cache)
    o_mla = _attn(qcat.reshape(b, t, HEADS * MLA_HEAD_PAD), kcat, vcat, min(ATTN_Q_TILE, t)).reshape(m, HEADS * MLA_DV)

    gq, gk, gvk, gkt = _gdn_prep(og, w["convw"], seg512, tm, t)
    ogf, ogb, gdn_state, ohf, ohb, hg_state = _lin_scan(
        w["layer"], gq.reshape(b, t, LIN_W), gk.reshape(b, t, LIN_W), gvk.reshape(b, t, LIN_W), gkt,
        oa.reshape(b, t, A_W), w["gconst"], gdn_s0,
        oh.reshape(b, t, H_W), vt, w["lb"], seg256, hg_s0)

    x_new = _post(xf, ogf.reshape(m, LIN_W), ogb.reshape(m, LIN_W), og, w["gnw"],
                  ohf.reshape(m, LIN_W), ohb.reshape(m, LIN_W), oh, w["hnw"], o_mla, seg256,
                  w["wout"], w["g_post_mix"], gt1, w["g_pre_ffn"], sc2, sh2, gt2, w["g_post_ffn"],
                  w["wf"], w["wo"], tm)
    mkr = oa[:, A_KR:A_KR + MLA_ROPE]
    return x_new.reshape(b, t, D_MODEL), gdn_state, hg_state, ckv.reshape(b, t, MLA_KV_RANK), mkr.reshape(b, t, MLA_ROPE)


def _hg_state_to_block(s):
    st = jnp.swapaxes(s, -1, -2)
    eye = jnp.eye(HEADS, dtype=s.dtype)
    big = st[:, :, :, :, None, :] * eye[None, None, :, None, :, None]
    return big.reshape(s.shape[0], 2, LIN_W, LIN_W)


def _hg_block_to_state(big):
    b = big.shape[0]
    r = big.reshape(b, 2, HEADS, HD, HEADS, HD)
    diag = jnp.stack([r[:, :, h, :, h, :] for h in range(HEADS)], axis=2)
    return jnp.swapaxes(diag, -1, -2)


def kernel(x_prompt, x_sample, cache_mla_ckv, cache_mla_krope, state_gdn, state_hgrn, c, c_ctx, w_ada, b_ada,
           g_pre_mix, g_post_mix, g_pre_ffn, g_post_ffn, w_in, w_out, gdn_conv_w, gdn_a_log, gdn_dt_bias,
           gdn_norm_w, hgrn_lb, hgrn_norm_w, mla_q_norm_w, mla_w_uq, mla_kv_norm_w, mla_w_ukv, w_ffn_in, w_ffn_out):
    bp, tp, _ = x_prompt.shape
    bd, td, _ = x_sample.shape
    past = cache_mla_ckv.shape[2]
    assert 1 + bd <= MOD_ROWS and tp % SCAN_BLOCK == 0 and td % SCAN_BLOCK == 0

    lower_bounds = jnp.pad(hgrn_lb.astype(F32), ((0, 8 - DEPTH), (0, 0)))

    cond = jnp.concatenate([c_ctx[None, :], c, jnp.zeros((MOD_ROWS - 1 - bd, D_MODEL), F32)], axis=0)
    mod_all = _ada(cond, w_ada, b_ada)

    seg512 = _seg_ones(2 * LIN_W)
    seg256 = _seg_ones(LIN_W)
    tab_ctx = _rope_tables(tp, False)
    tab_lat = _rope_tables(td, True)

    weights = [_layer_weights(l, w_in, w_out, gdn_conv_w, gdn_a_log, gdn_dt_bias, gdn_norm_w, lower_bounds,
                              hgrn_norm_w, mla_q_norm_w, mla_w_uq, mla_kv_norm_w, mla_w_ukv, w_ffn_in, w_ffn_out,
                              g_pre_mix, g_post_mix, g_pre_ffn, g_post_ffn) for l in range(DEPTH)]

    def mods(l, lo, n):
        rows = mod_all[l, lo:lo + n].reshape(n, 1, 6 * D_MODEL)
        return tuple(rows[:, :, i * D_MODEL:(i + 1) * D_MODEL] for i in range(6))

    xp = x_prompt
    zero_g = jnp.zeros((bp, 2 * HEADS, HD, HD), F32)
    zero_h = jnp.zeros((bp, 2, LIN_W, LIN_W), F32)
    ckv_l, kr_l, gs_l, hs_l = [], [], [], []
    for l in range(DEPTH):
        xp, gs, hs, ckv, mkr = _trunk_layer(xp, mods(l, 0, 1), weights[l], zero_g, zero_h, tab_ctx, None,
                                            seg512, seg256)
        ckv_l.append(ckv)
        kr_l.append(mkr)
        gs_l.append(gs.reshape(bp, 2, HEADS, HD, HD))
        hs_l.append(_hg_block_to_state(hs))

    xs = x_sample
    for l in range(DEPTH):
        ctx_cache = (cache_mla_ckv[:, l], jnp.pad(cache_mla_krope[:, l], ((0, 0), (0, 0), (0, 128 - MLA_ROPE))))
        xs, _, _, _, _ = _trunk_layer(xs, mods(l, 1, bd), weights[l],
                                      state_gdn[:, l].reshape(bd, 2 * HEADS, HD, HD),
                                      _hg_state_to_block(state_hgrn[:, l]), tab_lat, ctx_cache, seg512, seg256)

    return (xp, xs, jnp.stack(ckv_l, axis=1), jnp.stack(kr_l, axis=1),
            jnp.stack(gs_l, axis=1), jnp.stack(hs_l, axis=1))
```

```python
import functools
import math

import numpy as np
import jax
import jax.numpy as jnp
from jax import lax
from jax.experimental import pallas as pl
from jax.experimental.pallas import tpu as pltpu

F32 = jnp.float32
BF16 = jnp.bfloat16

D_MODEL = 1024
DEPTH = 2
GRID_W = 64
EPS = 1e-6
GATE_FLOOR = 1e-30
HEADS = 4
HD = 64
LIN_W = HEADS * HD
GDN_CHUNK = 64
HG_CHUNK = 16
SCAN_BLOCK = 256
LOG2E = 1.4426950408889634
MLA_Q_RANK = 384
MLA_KV_RANK = 256
MLA_NOPE = 128
MLA_ROPE = 64
MLA_DV = 128
MLA_QK = MLA_NOPE + MLA_ROPE
MLA_HEAD_PAD = 256
ROPE_BASE = 10000.0
D_FF = -(-8 * D_MODEL // (3 * 256)) * 256
FF_SPLIT = 2
TOKEN_TILE = 512
ATTN_Q_TILE = 256
MOD_ROWS = 16
VMEM_LIMIT = 56 * 1024 * 1024

_OFF = {}
_o = 0
for _n, _s in (("gq", 256), ("gk", 256), ("gv", 256), ("gz", 256), ("ga", 8), ("gb", 8),
               ("hq", 256), ("hi", 256), ("hf", 512), ("hg", 256),
               ("mcq", MLA_Q_RANK), ("mckv", MLA_KV_RANK), ("mkr", MLA_ROPE)):
    _OFF[_n] = _o
    _o += _s
IN_DIM = _o

G_W = 1024
G_CONV = 768
H_W = 1280
A_W = 896
A_KR = MLA_Q_RANK + MLA_KV_RANK
A_AB = A_KR + 2 * MLA_ROPE


def _swap_rope_halves(w):
    s = w.shape
    return w.reshape(s[:-1] + (2, 2, MLA_ROPE // 4))[..., ::-1, :].reshape(s)


def _w_in_layout(wi):
    return jnp.concatenate([wi[:, _OFF["gq"]:_OFF["ga"]], wi[:, _OFF["hq"]:_OFF["mcq"]], wi[:, _OFF["mcq"]:IN_DIM],
                            _swap_rope_halves(wi[:, _OFF["mkr"]:IN_DIM]), wi[:, _OFF["ga"]:_OFF["hq"]],
                            jnp.zeros((wi.shape[0], G_W + H_W + A_W - IN_DIM - MLA_ROPE), wi.dtype)], axis=1)


def _w_uq_layout(w):
    w = w.reshape(w.shape[0], HEADS, MLA_QK)
    rope = w[:, :, MLA_NOPE:]
    return jnp.concatenate([w[:, :, :MLA_NOPE], rope, _swap_rope_halves(rope)], axis=-1).reshape(w.shape[0], -1)


def _w_ukv_layout(w):
    w = w.reshape(w.shape[0], HEADS, MLA_NOPE + MLA_DV)
    return jnp.concatenate([w[:, :, :MLA_NOPE].reshape(w.shape[0], -1), w[:, :, MLA_NOPE:].reshape(w.shape[0], -1)],
                           axis=1)


def _seg_ones(n):
    i = np.arange(n) // HD
    return jnp.asarray((i[:, None] == i[None, :]).astype(np.float32), dtype=BF16)


def _rms(x, w):
    return x * lax.rsqrt(jnp.mean(x * x, axis=-1, keepdims=True) + EPS) * w


def _dot(a, b):
    return jnp.dot(a, b, preferred_element_type=F32)


def _dot_t(a, b):
    return lax.dot_general(a, b, (((1,), (1,)), ((), ())), preferred_element_type=F32)


def _split3(x):
    hi = x.astype(BF16)
    r1 = x - hi.astype(F32)
    mid = r1.astype(BF16)
    lo = (r1 - mid.astype(F32)).astype(BF16)
    return hi, mid, lo


def _split2(x):
    hi = x.astype(BF16)
    return hi, (x - hi.astype(F32)).astype(BF16)


def _seg_sum(x, seg):
    hi, lo = _split2(x)
    return _dot(hi, seg) + _dot(lo, seg)


def _mask_dot(mask01, x):
    hi, mid, lo = _split3(x)
    return _dot(mask01, hi) + _dot(mask01, mid) + _dot(mask01, lo)


def _dot_mask_t(x, mask01):
    hi, mid, lo = _split3(x)
    return _dot_t(hi, mask01) + _dot_t(mid, mask01) + _dot_t(lo, mask01)


def _softplus(x):
    return jnp.maximum(x, 0.0) + jnp.log1p(jnp.exp(-jnp.abs(x)))


def _sigmoid(x):
    return jax.nn.sigmoid(x)


def _silu(x):
    return x * jax.nn.sigmoid(x)


def _ada_kernel(c_ref, w_ref, b_ref, o_ref):
    s = _silu(c_ref[...]).astype(BF16)
    o_ref[0] = _dot(s, w_ref[0].astype(BF16)) + b_ref[0]


def _ada(cond, w_ada, b_ada):
    n = w_ada.shape[-1]
    tn = 1536
    return pl.pallas_call(
        _ada_kernel,
        grid=(DEPTH, n // tn),
        in_specs=[pl.BlockSpec((MOD_ROWS, D_MODEL), lambda l, j: (0, 0)),
                  pl.BlockSpec((1, D_MODEL, tn), lambda l, j: (l, 0, j)),
                  pl.BlockSpec((1, 1, tn), lambda l, j: (l, 0, j))],
        out_specs=pl.BlockSpec((1, MOD_ROWS, tn), lambda l, j: (l, 0, j)),
        out_shape=jax.ShapeDtypeStruct((DEPTH, MOD_ROWS, n), F32),
        compiler_params=pltpu.CompilerParams(dimension_semantics=("arbitrary", "arbitrary"),
                                             vmem_limit_bytes=VMEM_LIMIT),
        name="ada",
    )(cond, w_ada, b_ada.reshape(DEPTH, 1, n))


def _premix_kernel(x_ref, g_ref, sc_ref, sh_ref, w_ref, og_ref, oh_ref, oa_ref, vt_ref):
    h = _rms(x_ref[...], g_ref[...]) * (1.0 + sc_ref[0]) + sh_ref[0]
    hb = h.astype(BF16)
    og_ref[...] = _dot(hb, w_ref[:, 0:G_W])
    oh = _dot(hb, w_ref[:, G_W:G_W + H_W])
    oh_ref[...] = oh
    oa_ref[...] = _dot(hb, w_ref[:, G_W + H_W:G_W + H_W + A_W])
    vt_ref[0] = oh[:, LIN_W:2 * LIN_W].T


def _premix(x, g, sc, sh, w, tm, t_len):
    m = x.shape[0]
    tiles = m // tm
    per = tiles // sc.shape[0]
    row = lambda i: (i // per, 0, 0)
    tps = t_len // tm
    seq = lambda i: (i // tps, 0, i % tps)
    wt = G_W + H_W + A_W
    return pl.pallas_call(
        _premix_kernel,
        grid=(tiles,),
        in_specs=[pl.BlockSpec((tm, D_MODEL), lambda i: (i, 0)),
                  pl.BlockSpec((1, D_MODEL), lambda i: (0, 0)),
                  pl.BlockSpec((1, 1, D_MODEL), row),
                  pl.BlockSpec((1, 1, D_MODEL), row),
                  pl.BlockSpec((D_MODEL, wt), lambda i: (0, 0))],
        out_specs=[pl.BlockSpec((tm, G_W), lambda i: (i, 0)),
                   pl.BlockSpec((tm, H_W), lambda i: (i, 0)),
                   pl.BlockSpec((tm, A_W), lambda i: (i, 0)),
                   pl.BlockSpec((1, LIN_W, tm), seq)],
        out_shape=[jax.ShapeDtypeStruct((m, G_W), F32),
                   jax.ShapeDtypeStruct((m, H_W), F32),
                   jax.ShapeDtypeStruct((m, A_W), F32),
                   jax.ShapeDtypeStruct((m // t_len, LIN_W, t_len), F32)],
        compiler_params=pltpu.CompilerParams(dimension_semantics=("arbitrary",), vmem_limit_bytes=VMEM_LIMIT),
        name="premix",
    )(x, g, sc, sh, w)


def _keys_from_latent(ckv, krot, wukv_ref, kt_out, v_out):
    kv = _dot(ckv.astype(BF16), wukv_ref[...])
    krot_t = krot.T.astype(BF16)
    for h in range(HEADS):
        kt_out[MLA_HEAD_PAD * h:MLA_HEAD_PAD * h + MLA_NOPE, :] = kv[:, MLA_NOPE * h:MLA_NOPE * (h + 1)].T.astype(BF16)
        kt_out[MLA_HEAD_PAD * h + MLA_NOPE:MLA_HEAD_PAD * (h + 1), :] = krot_t
    v_out[...] = kv[:, HEADS * MLA_NOPE:].astype(BF16)


def _mla_prep_kernel(tps, past, a_ref, qnw_ref, kvnw_ref, wuq_ref, wukv_ref, qa_ref, qb_ref, ck_ref, sk_ref, *refs):
    if past:
        cache_ckv_ref, cache_kr_ref, q_out, k_out, v_out, ckv_out = refs
    else:
        q_out, k_out, v_out, ckv_out = refs
    j = pl.program_id(1)

    @pl.when(j < tps)
    def _():
        qn = _rms(a_ref[:, 0:MLA_Q_RANK], qnw_ref[...])
        y = _dot(qn.astype(BF16), wuq_ref[...])
        z = pltpu.roll(y, HEADS * MLA_HEAD_PAD - MLA_ROPE, 1)
        qa = qa_ref[...]
        qb = qb_ref[...]
        for h in range(HEADS):
            sl = slice(MLA_HEAD_PAD * h, MLA_HEAD_PAD * (h + 1))
            q_out[:, sl] = (y[:, sl] * qa + z[:, sl] * qb).astype(BF16)
        ckv = _rms(a_ref[:, MLA_Q_RANK:A_KR], kvnw_ref[...])
        ckv_out[...] = ckv
        kr = a_ref[:, A_KR:A_AB]
        krot = kr * ck_ref[...] + pltpu.roll(kr, 64, 1) * sk_ref[...]
        _keys_from_latent(ckv, krot, wukv_ref, k_out.at[0], v_out.at[0])

    if past:
        @pl.when(j == tps)
        def _():
            _keys_from_latent(cache_ckv_ref[0], cache_kr_ref[0], wukv_ref,
                              k_out.at[0, :, 0:past], v_out.at[0, 0:past])


def _mla_prep(oa, qnw, kvnw, wuq, wukv, qa, qb, ck, sk, tm, t_len, cache=None):
    m = oa.shape[0]
    b = m // t_len
    tps = t_len // tm
    past = 0 if cache is None else cache[0].shape[1]
    tok = lambda bi, j: (bi * tps + jnp.minimum(j, tps - 1), 0)
    pos = lambda bi, j: (jnp.minimum(j, tps - 1), 0)
    full = lambda bi, j: (0, 0)
    seq = lambda bi, j: (bi, j, 0)
    kw = HEADS * MLA_HEAD_PAD
    in_specs = [pl.BlockSpec((tm, A_W), tok),
                pl.BlockSpec((1, MLA_Q_RANK), full),
                pl.BlockSpec((1, MLA_KV_RANK), full),
                pl.BlockSpec((MLA_Q_RANK, kw), full),
                pl.BlockSpec((MLA_KV_RANK, HEADS * (MLA_NOPE + MLA_DV)), full),
                pl.BlockSpec((tm, MLA_HEAD_PAD), pos),
                pl.BlockSpec((tm, MLA_HEAD_PAD), pos),
                pl.BlockSpec((tm, 128), pos),
                pl.BlockSpec((tm, 128), pos)]
    operands = [oa, qnw, kvnw, wuq, wukv, qa, qb, ck, sk]
    if past:
        in_specs += [pl.BlockSpec((1, past, MLA_KV_RANK), lambda bi, j: (bi, 0, 0)),
                     pl.BlockSpec((1, past, 128), lambda bi, j: (bi, 0, 0))]
        operands += list(cache)
    return pl.pallas_call(
        functools.partial(_mla_prep_kernel, tps, past),
        grid=(b, tps + (1 if past else 0)),
        in_specs=in_specs,
        out_specs=[pl.BlockSpec((tm, kw), tok),
                   pl.BlockSpec((1, kw, tm), lambda bi, j: (bi, 0, j)),
                   pl.BlockSpec((1, tm, HEADS * MLA_DV), seq),
                   pl.BlockSpec((tm, MLA_KV_RANK), tok)],
        out_shape=[jax.ShapeDtypeStruct((m, kw), BF16),
                   jax.ShapeDtypeStruct((b, kw, t_len + past), BF16),
                   jax.ShapeDtypeStruct((b, t_len + past, HEADS * MLA_DV), BF16),
                   jax.ShapeDtypeStruct((m, MLA_KV_RANK), F32)],
        compiler_params=pltpu.CompilerParams(dimension_semantics=("arbitrary", "arbitrary"),
                                             vmem_limit_bytes=VMEM_LIMIT),
        name="mla_prep",
    )(*operands)


def _attn_kernel(q_ref, k_ref, v_ref, o_ref):
    for h in range(HEADS):
        sl = slice(MLA_HEAD_PAD * h, MLA_HEAD_PAD * (h + 1))
        s = _dot(q_ref[0, :, sl], k_ref[0, sl, :])
        p = jnp.exp2(s - jnp.max(s, axis=-1, keepdims=True))
        l = jnp.sum(p, axis=-1, keepdims=True)
        o = _dot(p.astype(BF16), v_ref[0, :, MLA_DV * h:MLA_DV * (h + 1)])
        o_ref[0, :, MLA_DV * h:MLA_DV * (h + 1)] = o / l


def _attn(q, kt, v, tq):
    b, t, kw = q.shape
    s = kt.shape[2]
    return pl.pallas_call(
        _attn_kernel,
        grid=(b, t // tq),
        in_specs=[pl.BlockSpec((1, tq, kw), lambda bi, i: (bi, i, 0)),
                  pl.BlockSpec((1, kw, s), lambda bi, i: (bi, 0, 0)),
                  pl.BlockSpec((1, s, HEADS * MLA_DV), lambda bi, i: (bi, 0, 0))],
        out_specs=pl.BlockSpec((1, tq, HEADS * MLA_DV), lambda bi, i: (bi, i, 0)),
        out_shape=jax.ShapeDtypeStruct((b, t, HEADS * MLA_DV), F32),
        compiler_params=pltpu.CompilerParams(dimension_semantics=("arbitrary", "arbitrary"),
                                             vmem_limit_bytes=VMEM_LIMIT),
        name="attn",
    )(q, kt, v)


def _gdn_prep_kernel(tps, cur_ref, prev_ref, next_ref, w_ref, seg_ref, q_out, k_out, v_out, kt_out):
    i = pl.program_id(0)
    tm = cur_ref.shape[0]
    has_prev = ((i % tps) != 0).astype(F32)
    has_next = ((i % tps) != (tps - 1)).astype(F32)
    xc = jnp.concatenate([prev_ref[...] * has_prev, cur_ref[...], next_ref[...] * has_next], axis=0)
    n = tm + 16
    y = xc[8:8 + tm] * w_ref[2:3, :]
    for j in (0, 1, 3, 4):
        d = j - 2
        y = y + pltpu.roll(xc, (n - d) % n, 0)[8:8 + tm] * w_ref[j:j + 1, :]
    y = _silu(y)
    seg = seg_ref[...]
    qk = y[:, 0:2 * LIN_W]
    qk = qk * lax.rsqrt(_seg_sum(qk * qk, seg) + EPS)
    q_out[...] = qk[:, 0:LIN_W] * (HD ** -0.5)
    k_out[...] = qk[:, LIN_W:2 * LIN_W]
    kt_out[0] = qk[:, LIN_W:2 * LIN_W].T
    v_out[...] = y[:, 2 * LIN_W:]


def _gdn_prep(og, convw, seg, tm, t_len):
    m = og.shape[0]
    tiles = m // tm
    tps = t_len // tm
    r8 = tm // 8
    last8 = m // 8 - 1
    return pl.pallas_call(
        functools.partial(_gdn_prep_kernel, tps),
        grid=(tiles,),
        in_specs=[pl.BlockSpec((tm, G_CONV), lambda i: (i, 0)),
                  pl.BlockSpec((8, G_CONV), lambda i: (jnp.maximum(i * r8 - 1, 0), 0)),
                  pl.BlockSpec((8, G_CONV), lambda i: (jnp.minimum((i + 1) * r8, last8), 0)),
                  pl.BlockSpec((8, G_CONV), lambda i: (0, 0)),
                  pl.BlockSpec((2 * LIN_W, 2 * LIN_W), lambda i: (0, 0))],
        out_specs=[pl.BlockSpec((tm, LIN_W), lambda i: (i, 0)),
                   pl.BlockSpec((tm, LIN_W), lambda i: (i, 0)),
                   pl.BlockSpec((tm, LIN_W), lambda i: (i, 0)),
                   pl.BlockSpec((1, LIN_W, tm), lambda i: (i // tps, 0, i % tps))],
        out_shape=[jax.ShapeDtypeStruct((m, LIN_W), F32),
                   jax.ShapeDtypeStruct((m, LIN_W), F32),
                   jax.ShapeDtypeStruct((m, LIN_W), F32),
                   jax.ShapeDtypeStruct((m // t_len, LIN_W, t_len), F32)],
        compiler_params=pltpu.CompilerParams(dimension_semantics=("arbitrary",), vmem_limit_bytes=VMEM_LIMIT),
        name="gdn_prep",
    )(og, og, og, convw, seg)


def _cumsum_masks():
    r = np.arange(SCAN_BLOCK)[:, None]
    c = np.arange(SCAN_BLOCK)[None, :]
    masks = []
    for chunk in (GDN_CHUNK, HG_CHUNK):
        same = (r // chunk) == (c // chunk)
        masks += [same & (r >= c), same & (r <= c)]
    return jnp.asarray(np.stack(masks).astype(np.float32), dtype=BF16)


def _gdn_units(d, incl_blk, q_ref, k_ref, v_ref, kt_ref, ab_ref, gc_ref, o_ref):
    c = GDN_CHUNK
    ri = lax.broadcasted_iota(jnp.int32, (c, c), 0)
    ci = lax.broadcasted_iota(jnp.int32, (c, c), 1)
    incl = (ri >= ci) if d == 0 else (ri <= ci)
    strict = (ri > ci) if d == 0 else (ri < ci)
    last = c - 1 if d == 0 else 0
    ab = ab_ref[0]
    g_cols = gc_ref[0:1, :] * _softplus(ab + gc_ref[1:2, :])
    beta = _sigmoid(ab)
    gcs_blk = _mask_dot(incl_blk, g_cols)
    gcr_blk = gcs_blk.T
    n_chunks = SCAN_BLOCK // c
    units = []
    for rank, cc in enumerate(range(n_chunks) if d == 0 else range(n_chunks - 1, -1, -1)):
        rows = slice(cc * c, (cc + 1) * c)
        for h in range(HEADS):
            j = d * HEADS + h
            hs = slice(HD * h, HD * (h + 1))
            gcol = gcs_blk[rows, j:j + 1]
            grow = gcr_blk[j:j + 1, rows]
            bcol = beta[rows, 8 + j:9 + j]
            gl = gcol[last:last + 1, :]
            eg = jnp.exp(gcol)
            k = k_ref[0, rows, hs]
            units.append(dict(
                rank=rank, j=j, rows=rows, hs=hs, o_ref=o_ref, strict=strict, gl=gl, bcol=bcol, eg=eg,
                q=q_ref[0, rows, hs], k=k,
                k_tail=(kt_ref[0, hs, rows] * jnp.exp(gl - grow)).astype(BF16),
                decay=jnp.where(incl, jnp.exp(jnp.where(incl, gcol - grow, 0.0)), 0.0),
                x=jnp.concatenate([v_ref[0, rows, hs] * bcol, k * (bcol * eg)], axis=1)))
    return units


def _gdn_stages(units, s_ref):
    c = GDN_CHUNK
    ri = lax.broadcasted_iota(jnp.int32, (c, c), 0)
    ci = lax.broadcasted_iota(jnp.int32, (c, c), 1)
    diag_blk = jnp.right_shift(ri, 4) == jnp.right_shift(ci, 4)
    eye = (ri == ci).astype(F32)

    def gram():
        for u in units:
            kb16 = u["k"].astype(BF16)
            u["kk"] = _dot_t((u["k"] * u["bcol"]).astype(BF16), kb16)
            u["qk"] = _dot_t(u["q"].astype(BF16), kb16)

    def split():
        for u in units:
            p = jnp.where(u["strict"], u["kk"] * u["decay"], 0.0)
            pd = jnp.where(diag_blk, p, 0.0)
            u["att"] = (u["qk"] * u["decay"]).astype(BF16)
            u["poff"] = (p - pd).astype(BF16)
            u["td"] = eye - pd
            u["a"] = pd.astype(BF16)

    def square():
        for u in units:
            u["a"] = _dot(u["a"], u["a"]).astype(BF16)

    def extend():
        for u in units:
            u["td"] = u["td"] + _dot(u["td"].astype(BF16), u["a"])

    def apply_diag():
        for u in units:
            yn = _dot(u["td"].astype(BF16), jnp.concatenate([u["x"].astype(BF16), u["poff"]], axis=1))
            u["y"] = yn[:, 0:2 * HD]
            u["n"] = yn[:, 2 * HD:3 * HD].astype(BF16)
            u["x"] = u["y"]

    def substitute():
        for u in units:
            u["x"] = u["y"] - _dot(u["n"], u["x"].astype(BF16))

    def new_values(rank):
        for u in units:
            if u["rank"] == rank:
                u["s"] = s_ref[0, u["j"]]
                u["sb"] = u["s"].astype(BF16)
                u["vb"] = (u["x"][:, 0:HD] - _dot(u["x"][:, HD:2 * HD].astype(BF16), u["sb"])).astype(BF16)

    def emit(rank):
        for u in units:
            if u["rank"] == rank:
                u["o_ref"][0, u["rows"], u["hs"]] = (_dot((u["q"] * u["eg"]).astype(BF16), u["sb"])
                                                     + _dot(u["att"], u["vb"]))
                s_ref[0, u["j"]] = u["s"] * jnp.exp(u["gl"]) + _dot(u["k_tail"], u["vb"])

    stages = [gram, split] + [square, extend] * 3 + [apply_diag] + [substitute] * 3
    for rank in range(SCAN_BLOCK // c):
        stages += [functools.partial(new_values, rank), functools.partial(emit, rank)]
    return stages


def _hgrn_lower_bound(layer, lb_ref):
    raw = lb_ref[0:DEPTH, :]
    e = jnp.exp(raw - jnp.max(raw, axis=0, keepdims=True))
    gamma = e / jnp.sum(e, axis=0, keepdims=True)
    lb = jnp.zeros((1, LIN_W), F32)
    for i in range(1, layer + 1):
        lb = lb + gamma[i:i + 1, :]
    return lb


def _hgrn_block(d, incl01, lb, q_ref, v_ref, f_ref):
    c = HG_CHUNK
    oml = 1.0 - lb
    q = q_ref[0]
    f = f_ref[0]
    logf = jnp.log(jnp.maximum(lb + oml * _sigmoid(f), GATE_FLOOR))
    bc = _mask_dot(incl01, logf)
    bc2 = bc * LOG2E
    src2 = bc2 - (jnp.log2(oml) - _softplus(f) * LOG2E)
    return dict(d=d, q=q, v=v_ref[0], kk=oml * _sigmoid(-f), bc=bc, bc2=bc2, src2=src2,
                qe=(q * jnp.exp(bc)).astype(BF16), last=c - 1 if d == 0 else 0)


def _hgrn_live_halves(d, s):
    half = HG_CHUNK // 2
    if d == 0:
        return (1,) if s >= half else (0, 1)
    return (0,) if s < half else (0, 1)


def _hgrn_chunk_build(blk, cc):
    c = HG_CHUNK
    half = c // 2
    d = blk["d"]
    base = cc * c
    rows = slice(base, base + c)
    kk, bc = blk["kk"][rows], blk["bc"][rows]
    bl = bc[blk["last"]:blk["last"] + 1, :]
    row_half = lax.broadcasted_iota(jnp.int32, (half, LIN_W), 0)
    zero_half = jnp.zeros((half, LIN_W), F32)
    parts = []
    for s in range(c):
        bs = blk["src2"][base + s:base + s + 1]
        for hf in (0, 1):
            if hf not in _hgrn_live_halves(d, s):
                parts.append(zero_half)
                continue
            tr = slice(base + hf * half, base + (hf + 1) * half)
            col = blk["q"][tr] * jnp.exp2(blk["bc2"][tr] - bs)
            if hf == s // half:
                rid = row_half + hf * half
                col = jnp.where((rid >= s) if d == 0 else (rid <= s), col, 0.0)
            parts.append(col)
    return dict(cols=jnp.concatenate(parts, axis=0).astype(BF16), k_tail=(kk * jnp.exp(bl - bc)).astype(BF16),
                decay=jnp.exp(bl))


def _hgrn_chunk_launch(blk, cc, built, vt_ref, ones, st):
    rows = slice(cc * HG_CHUNK, (cc + 1) * HG_CHUNK)
    return dict(
        r=_dot(built["cols"], ones),
        o_inter=_dot_t(blk["qe"][rows], st.astype(BF16)),
        upd=_dot(vt_ref[0, :, rows].astype(BF16), built["k_tail"]),
        decay=built["decay"])


def _hgrn_chunk_finish(blk, cc, pend, seg_mask, st, o_ref):
    c = HG_CHUNK
    half = c // 2
    base = cc * c
    r = pend["r"]
    o_half = [pend["o_inter"][0:half], pend["o_inter"][half:c]]
    for s in range(c):
        vs = blk["v"][base + s:base + s + 1]
        for hf in _hgrn_live_halves(blk["d"], s):
            o_half[hf] = o_half[hf] + r[s * c + hf * half:s * c + (hf + 1) * half, :] * vs
    o_ref[0, base:base + half, :] = o_half[0]
    o_ref[0, base + half:base + c, :] = o_half[1]
    return st * pend["decay"] + pend["upd"] * seg_mask


def _hgrn_steps(lb, masks_ref, fwd_refs, bwd_refs, ones, st_ref):
    seg_mask = ones.astype(F32)
    (qf, vf, ff, vtf, of_ref), (qb, vb, fb, vtb, ob_ref) = fwd_refs, bwd_refs
    fwd = _hgrn_block(0, masks_ref[2], lb, qf, vf, ff)
    bwd = _hgrn_block(1, masks_ref[3], lb, qb, vb, fb)
    carry = {0: st_ref[0, 0], 1: st_ref[0, 1]}
    built, pending = {}, {}
    n_chunks = SCAN_BLOCK // HG_CHUNK
    chunk_of = {0: lambda step: step, 1: lambda step: n_chunks - 1 - step}
    side = {0: (fwd, vtf, of_ref), 1: (bwd, vtb, ob_ref)}

    def build(d, step):
        built[(d, step)] = _hgrn_chunk_build(side[d][0], chunk_of[d](step))

    def launch(d, step):
        blk, vt_ref, _ = side[d]
        pending[(d, step)] = _hgrn_chunk_launch(blk, chunk_of[d](step), built.pop((d, step)), vt_ref, ones, carry[d])

    def finish(d, step):
        blk, _, o_ref = side[d]
        carry[d] = _hgrn_chunk_finish(blk, chunk_of[d](step), pending.pop((d, step)), seg_mask, carry[d], o_ref)

    def flush():
        st_ref[0, 0] = carry[0]
        st_ref[0, 1] = carry[1]

    groups = []
    for slot in range(n_chunks + 2):
        group = []
        for phase, step in ((finish, slot - 2), (launch, slot - 1), (build, slot)):
            if 0 <= step < n_chunks:
                group += [functools.partial(phase, 0, step), functools.partial(phase, 1, step)]
        groups.append(group)
    return groups, flush


def _scan_kernel(layer, masks_ref,
                 gqf, gkf, gvkf, gktf, gabf, gqb, gkb, gvkb, gktb, gabb, gc_ref, gs0_ref,
                 hqf, hvf, hff, hvtf, hqb, hvb, hfb, hvtb, lb_ref, ones_ref, hs0_ref,
                 gof_ref, gob_ref, gs_ref, hof_ref, hob_ref, hst_ref):
    @pl.when(pl.program_id(1) == 0)
    def _():
        gs_ref[...] = gs0_ref[...]
        hst_ref[...] = hs0_ref[...]

    units = (_gdn_units(0, masks_ref[0], gqf, gkf, gvkf, gktf, gabf, gc_ref, gof_ref)
             + _gdn_units(1, masks_ref[1], gqb, gkb, gvkb, gktb, gabb, gc_ref, gob_ref))
    stages = _gdn_stages(units, gs_ref)
    groups, flush = _hgrn_steps(_hgrn_lower_bound(layer, lb_ref), masks_ref, (hqf, hvf, hff, hvtf, hof_ref),
                                (hqb, hvb, hfb, hvtb, hob_ref), ones_ref[...], hst_ref)
    per_stage = -(-len(groups) // len(stages))
    for stage in stages:
        stage()
        for group in groups[:per_stage]:
            for step in group:
                step()
        groups = groups[per_stage:]
    for group in groups:
        for step in group:
            step()
    flush()


def _lin_scan(layer, gq, gk, gvk, gkt, oa, gconst, gs0, oh, vt, lb, ones, hs0):
    b, t, _ = gq.shape
    blk = SCAN_BLOCK
    nb = t // blk

    def tok(cb, rev):
        return (lambda bi, i: (bi, nb - 1 - i, cb)) if rev else (lambda bi, i: (bi, i, cb))

    def lanes(rev):
        return (lambda bi, i: (bi, 0, nb - 1 - i)) if rev else (lambda bi, i: (bi, 0, i))

    def gdn_specs(rev):
        return [pl.BlockSpec((1, blk, LIN_W), tok(0, rev)),
                pl.BlockSpec((1, blk, LIN_W), tok(0, rev)),
                pl.BlockSpec((1, blk, LIN_W), tok(0, rev)),
                pl.BlockSpec((1, LIN_W, blk), lanes(rev)),
                pl.BlockSpec((1, blk, 128), tok(A_AB // 128, rev))]

    def hgrn_specs(rev):
        f_col = 3 if rev else 2
        return [pl.BlockSpec((1, blk, LIN_W), tok(0, rev)),
                pl.BlockSpec((1, blk, LIN_W), tok(1, rev)),
                pl.BlockSpec((1, blk, LIN_W), tok(f_col, rev)),
                pl.BlockSpec((1, LIN_W, blk), lanes(rev))]

    const = lambda r, c: pl.BlockSpec((r, c), lambda bi, i: (0, 0))
    g_state = pl.BlockSpec((1, 2 * HEADS, HD, HD), lambda bi, i: (bi, 0, 0, 0))
    h_state = pl.BlockSpec((1, 2, LIN_W, LIN_W), lambda bi, i: (bi, 0, 0, 0))
    out_tok = lambda rev: pl.BlockSpec((1, blk, LIN_W), tok(0, rev))
    seq = jax.ShapeDtypeStruct((b, t, LIN_W), F32)
    return pl.pallas_call(
        functools.partial(_scan_kernel, layer),
        grid=(b, nb),
        in_specs=[pl.BlockSpec((4, blk, blk), lambda bi, i: (0, 0, 0))]
        + gdn_specs(False) + gdn_specs(True) + [const(8, 128), g_state]
        + hgrn_specs(False) + hgrn_specs(True) + [const(8, LIN_W), const(LIN_W, LIN_W), h_state],
        out_specs=[out_tok(False), out_tok(True), g_state, out_tok(False), out_tok(True), h_state],
        out_shape=[seq, seq, jax.ShapeDtypeStruct((b, 2 * HEADS, HD, HD), F32),
                   seq, seq, jax.ShapeDtypeStruct((b, 2, LIN_W, LIN_W), F32)],
        compiler_params=pltpu.CompilerParams(dimension_semantics=("arbitrary", "arbitrary"),
                                             vmem_limit_bytes=VMEM_LIMIT),
        name="lin_scan",
    )(_cumsum_masks(), gq, gk, gvk, gkt, oa, gq, gk, gvk, gkt, oa, gconst, gs0,
      oh, oh, oh, vt, oh, oh, oh, vt, lb, ones, hs0)


def _post_kernel(x_ref, ogf_ref, ogb_ref, z_ref, gnw_ref, ohf_ref, ohb_ref, hg_ref, hnw_ref, om_ref, seg_ref,
                 wout_ref, gpm_ref, gt1_ref, gpf_ref, sc2_ref, sh2_ref, gt2_ref, gff_ref,
                 wf_ref, wo_ref, out_ref):
    seg = seg_ref[...]
    inv = 1.0 / HD
    og = ogf_ref[...] + ogb_ref[...]
    og = og * lax.rsqrt(_seg_sum(og * og, seg) * inv + EPS) * gnw_ref[...] * _silu(z_ref[...])
    oh = ohf_ref[...] + ohb_ref[...]
    oh = oh * lax.rsqrt(_seg_sum(oh * oh, seg) * inv + EPS) * hnw_ref[...] * _sigmoid(hg_ref[...])
    mix = jnp.concatenate([og.astype(BF16), oh.astype(BF16), om_ref[...].astype(BF16)], axis=-1)
    x1 = x_ref[...] + gt1_ref[0] * _rms(_dot(mix, wout_ref[...]), gpm_ref[...])
    hb = (_rms(x1, gpf_ref[...]) * (1.0 + sc2_ref[0]) + sh2_ref[0]).astype(BF16)
    fw = D_FF // FF_SPLIT
    y = None
    for part in range(FF_SPLIT):
        cs = slice(part * fw, (part + 1) * fw)
        gs = slice(D_FF + part * fw, D_FF + (part + 1) * fw)
        act = (_silu(_dot(hb, wf_ref[:, cs])) * _dot(hb, wf_ref[:, gs])).astype(BF16)
        contrib = _dot(act, wo_ref[cs, :])
        y = contrib if y is None else y + contrib
    out_ref[...] = x1 + gt2_ref[0] * _rms(y, gff_ref[...])


def _post(x, ogf, ogb, og, gnw, ohf, ohb, oh, hnw, om, seg, wout, gpm, gt1, gpf, sc2, sh2, gt2, gff, wf, wo, tm):
    m = x.shape[0]
    tiles = m // tm
    per = tiles // gt1.shape[0]
    row = lambda i: (i // per, 0, 0)
    tok = lambda w: pl.BlockSpec((tm, w), lambda i: (i, 0))
    vec = lambda w: pl.BlockSpec((1, w), lambda i: (0, 0))
    mod = pl.BlockSpec((1, 1, D_MODEL), row)
    once = lambda r, c: pl.BlockSpec((r, c), lambda i: (0, 0), pipeline_mode=pl.Buffered(1))
    return pl.pallas_call(
        _post_kernel,
        grid=(tiles,),
        in_specs=[tok(D_MODEL), tok(LIN_W), tok(LIN_W),
                  pl.BlockSpec((tm, LIN_W), lambda i: (i, G_W // LIN_W - 1)), vec(LIN_W),
                  tok(LIN_W), tok(LIN_W),
                  pl.BlockSpec((tm, LIN_W), lambda i: (i, H_W // LIN_W - 1)), vec(LIN_W),
                  tok(HEADS * MLA_DV), once(LIN_W, LIN_W),
                  once(D_MODEL, D_MODEL), vec(D_MODEL), mod, vec(D_MODEL), mod, mod, mod, vec(D_MODEL),
                  once(D_MODEL, 2 * D_FF), once(D_FF, D_MODEL)],
        out_specs=tok(D_MODEL),
        out_shape=jax.ShapeDtypeStruct((m, D_MODEL), F32),
        compiler_params=pltpu.CompilerParams(dimension_semantics=("arbitrary",), vmem_limit_bytes=VMEM_LIMIT),
        name="post",
    )(x, ogf, ogb, og, gnw, ohf, ohb, oh, hnw, om, seg, wout, gpm, gt1, gpf, sc2, sh2, gt2, gff, wf, wo)


def _rope_tables(t_len, use_rope):
    scale = MLA_QK ** -0.5 * LOG2E
    if use_rope:
        rows = t_len // GRID_W
        row = np.repeat(np.arange(rows, dtype=np.float64), GRID_W)
        col = np.tile(np.arange(GRID_W, dtype=np.float64), rows)
        nf = MLA_ROPE // 4
        inv = ROPE_BASE ** (-np.arange(nf, dtype=np.float64) / nf)
        ar, ac = row[:, None] * inv, col[:, None] * inv
        cos = np.concatenate([np.cos(ar), np.cos(ar), np.cos(ac), np.cos(ac)], axis=-1)
        sin = np.concatenate([-np.sin(ar), np.sin(ar), -np.sin(ac), np.sin(ac)], axis=-1)
    else:
        cos = np.ones((t_len, MLA_ROPE))
        sin = np.zeros((t_len, MLA_ROPE))
    z64 = np.zeros((t_len, MLA_ROPE))
    qa = scale * np.concatenate([np.ones((t_len, MLA_NOPE)), cos, z64], axis=-1)
    qb = scale * np.concatenate([np.zeros((t_len, MLA_NOPE)), sin, z64], axis=-1)
    ck = np.concatenate([cos, z64], axis=-1)
    sk = np.concatenate([sin, z64], axis=-1)
    return tuple(jnp.asarray(a, dtype=F32) for a in (qa, qb, ck, sk))


def _layer_weights(l, w_in, w_out, gdn_conv_w, gdn_a_log, gdn_dt_bias, gdn_norm_w, lower_bounds, hgrn_norm_w,
                   mla_q_norm_w, mla_w_uq, mla_kv_norm_w, mla_w_ukv, w_ffn_in, w_ffn_out,
                   g_pre_mix, g_post_mix, g_pre_ffn, g_post_ffn):
    w_cat = _w_in_layout(w_in[l].astype(BF16))
    convw = jnp.concatenate([gdn_conv_w[l].T, jnp.zeros((3, G_CONV), F32)], axis=0)
    neg_a = -jnp.exp(gdn_a_log[l].astype(F32)).reshape(1, 8)
    dt = gdn_dt_bias[l].astype(F32).reshape(1, 8)
    pad = lambda r: jnp.pad(r, ((0, 0), (0, 128 - r.shape[1])))
    gconst = jnp.concatenate([pad(neg_a), pad(dt), jnp.zeros((6, 128), F32)], axis=0)
    vec = lambda a: a.reshape(1, -1).astype(F32)
    return dict(
        w_cat=w_cat, convw=convw, gconst=gconst,
        gnw=vec(jnp.tile(gdn_norm_w[l], HEADS)), hnw=vec(jnp.tile(hgrn_norm_w[l], HEADS)),
        layer=l, lb=lower_bounds,
        qnw=vec(mla_q_norm_w[l]), kvnw=vec(mla_kv_norm_w[l]),
        wuq=_w_uq_layout(mla_w_uq[l].astype(BF16)), wukv=_w_ukv_layout(mla_w_ukv[l].astype(BF16)),
        wout=w_out[l].astype(BF16), wf=w_ffn_in[l].astype(BF16),
        wo=w_ffn_out[l].astype(BF16),
        g_pre_mix=vec(g_pre_mix[l]), g_post_mix=vec(g_post_mix[l]),
        g_pre_ffn=vec(g_pre_ffn[l]), g_post_ffn=vec(g_post_ffn[l]))


def _trunk_layer(x, mod, w, gdn_s0, hg_s0, tables, ctx_cache, seg512, seg256):
    b, t, _ = x.shape
    m = b * t
    tm = min(TOKEN_TILE, t)
    sh1, sc1, gt1, sh2, sc2, gt2 = mod
    xf = x.reshape(m, D_MODEL)
    og, oh, oa, vt = _premix(xf, w["g_pre_mix"], sc1, sh1, w["w_cat"], tm, t)

    qa, qb, ck, sk = tables
    qcat, kcat, vcat, ckv = _mla_prep(oa, w["qnw"], w["kvnw"], w["wuq"], w["wukv"], qa, qb, ck, sk, tm, t, ctx_cache)
    o_mla = _attn(qcat.reshape(b, t, HEADS * MLA_HEAD_PAD), kcat, vcat, min(ATTN_Q_TILE, t)).reshape(m, HEADS * MLA_DV)

    gq, gk, gvk, gkt = _gdn_prep(og, w["convw"], seg512, tm, t)
    ogf, ogb, gdn_state, ohf, ohb, hg_state = _lin_scan(
        w["layer"], gq.reshape(b, t, LIN_W), gk.reshape(b, t, LIN_W), gvk.reshape(b, t, LIN_W), gkt,
        oa.reshape(b, t, A_W), w["gconst"], gdn_s0,
        oh.reshape(b, t, H_W), vt, w["lb"], seg256, hg_s0)

    x_new = _post(xf, ogf.reshape(m, LIN_W), ogb.reshape(m, LIN_W), og, w["gnw"],
                  ohf.reshape(m, LIN_W), ohb.reshape(m, LIN_W), oh, w["hnw"], o_mla, seg256,
                  w["wout"], w["g_post_mix"], gt1, w["g_pre_ffn"], sc2, sh2, gt2, w["g_post_ffn"],
                  w["wf"], w["wo"], tm)
    mkr = oa[:, A_KR:A_KR + MLA_ROPE]
    return x_new.reshape(b, t, D_MODEL), gdn_state, hg_state, ckv.reshape(b, t, MLA_KV_RANK), mkr.reshape(b, t, MLA_ROPE)


def _hg_state_to_block(s):
    st = jnp.swapaxes(s, -1, -2)
    eye = jnp.eye(HEADS, dtype=s.dtype)
    big = st[:, :, :, :, None, :] * eye[None, None, :, None, :, None]
    return big.reshape(s.shape[0], 2, LIN_W, LIN_W)


def _hg_block_to_state(big):
    b = big.shape[0]
    r = big.reshape(b, 2, HEADS, HD, HEADS, HD)
    diag = jnp.stack([r[:, :, h, :, h, :] for h in range(HEADS)], axis=2)
    return jnp.swapaxes(diag, -1, -2)


def kernel(x_prompt, x_sample, cache_mla_ckv, cache_mla_krope, state_gdn, state_hgrn, c, c_ctx, w_ada, b_ada,
           g_pre_mix, g_post_mix, g_pre_ffn, g_post_ffn, w_in, w_out, gdn_conv_w, gdn_a_log, gdn_dt_bias,
           gdn_norm_w, hgrn_lb, hgrn_norm_w, mla_q_norm_w, mla_w_uq, mla_kv_norm_w, mla_w_ukv, w_ffn_in, w_ffn_out):
    bp, tp, _ = x_prompt.shape
    bd, td, _ = x_sample.shape
    past = cache_mla_ckv.shape[2]
    assert 1 + bd <= MOD_ROWS and tp % SCAN_BLOCK == 0 and td % SCAN_BLOCK == 0

    lower_bounds = jnp.pad(hgrn_lb.astype(F32), ((0, 8 - DEPTH), (0, 0)))

    cond = jnp.concatenate([c_ctx[None, :], c, jnp.zeros((MOD_ROWS - 1 - bd, D_MODEL), F32)], axis=0)
    mod_all = _ada(cond, w_ada, b_ada)

    seg512 = _seg_ones(2 * LIN_W)
    seg256 = _seg_ones(LIN_W)
    tab_ctx = _rope_tables(tp, False)
    tab_lat = _rope_tables(td, True)

    weights = [_layer_weights(l, w_in, w_out, gdn_conv_w, gdn_a_log, gdn_dt_bias, gdn_norm_w, lower_bounds,
                              hgrn_norm_w, mla_q_norm_w, mla_w_uq, mla_kv_norm_w, mla_w_ukv, w_ffn_in, w_ffn_out,
                              g_pre_mix, g_post_mix, g_pre_ffn, g_post_ffn) for l in range(DEPTH)]

    def mods(l, lo, n):
        rows = mod_all[l, lo:lo + n].reshape(n, 1, 6 * D_MODEL)
        return tuple(rows[:, :, i * D_MODEL:(i + 1) * D_MODEL] for i in range(6))

    xp = x_prompt
    zero_g = jnp.zeros((bp, 2 * HEADS, HD, HD), F32)
    zero_h = jnp.zeros((bp, 2, LIN_W, LIN_W), F32)
    ckv_l, kr_l, gs_l, hs_l = [], [], [], []
    for l in range(DEPTH):
        xp, gs, hs, ckv, mkr = _trunk_layer(xp, mods(l, 0, 1), weights[l], zero_g, zero_h, tab_ctx, None,
                                            seg512, seg256)
        ckv_l.append(ckv)
        kr_l.append(mkr)
        gs_l.append(gs.reshape(bp, 2, HEADS, HD, HD))
        hs_l.append(_hg_block_to_state(hs))

    xs = x_sample
    for l in range(DEPTH):
        ctx_cache = (cache_mla_ckv[:, l], jnp.pad(cache_mla_krope[:, l], ((0, 0), (0, 0), (0, 128 - MLA_ROPE))))
        xs, _, _, _, _ = _trunk_layer(xs, mods(l, 1, bd), weights[l],
                                      state_gdn[:, l].reshape(bd, 2 * HEADS, HD, HD),
                                      _hg_state_to_block(state_hgrn[:, l]), tab_lat, ctx_cache, seg512, seg256)

    return (xp, xs, jnp.stack(ckv_l, axis=1), jnp.stack(kr_l, axis=1),
            jnp.stack(gs_l, axis=1), jnp.stack(hs_l, axis=1))
```

```python
import functools
import math

import numpy as np
import jax
import jax.numpy as jnp
from jax import lax
from jax.experimental import pallas as pl
from jax.experimental.pallas import tpu as pltpu

F32 = jnp.float32
BF16 = jnp.bfloat16

D_MODEL = 1024
DEPTH = 2
GRID_W = 64
EPS = 1e-6
GATE_FLOOR = 1e-30
HEADS = 4
HD = 64
LIN_W = HEADS * HD
GDN_CHUNK = 64
HG_CHUNK = 16
SCAN_BLOCK = 256
LOG2E = 1.4426950408889634
MLA_Q_RANK = 384
MLA_KV_RANK = 256
MLA_NOPE = 128
MLA_ROPE = 64
MLA_DV = 128
MLA_QK = MLA_NOPE + MLA_ROPE
MLA_HEAD_PAD = 256
ROPE_BASE = 10000.0
D_FF = -(-8 * D_MODEL // (3 * 256)) * 256
FF_SPLIT = 11
TOKEN_TILE = 512
ATTN_Q_TILE = 256
MOD_ROWS = 16
VMEM_LIMIT = 56 * 1024 * 1024

_OFF = {}
_o = 0
for _n, _s in (("gq", 256), ("gk", 256), ("gv", 256), ("gz", 256), ("ga", 8), ("gb", 8),
               ("hq", 256), ("hi", 256), ("hf", 512), ("hg", 256),
               ("mcq", MLA_Q_RANK), ("mckv", MLA_KV_RANK), ("mkr", MLA_ROPE)):
    _OFF[_n] = _o
    _o += _s
IN_DIM = _o

G_W = 1024
G_CONV = 768
H_W = 1280
A_W = 896
A_KR = MLA_Q_RANK + MLA_KV_RANK
A_AB = A_KR + 2 * MLA_ROPE


def _swap_rope_halves(w):
    s = w.shape
    return w.reshape(s[:-1] + (2, 2, MLA_ROPE // 4))[..., ::-1, :].reshape(s)


def _w_in_layout(wi):
    return jnp.concatenate([wi[:, _OFF["gq"]:_OFF["ga"]], wi[:, _OFF["hq"]:_OFF["mcq"]], wi[:, _OFF["mcq"]:IN_DIM],
                            _swap_rope_halves(wi[:, _OFF["mkr"]:IN_DIM]), wi[:, _OFF["ga"]:_OFF["hq"]],
                            jnp.zeros((wi.shape[0], G_W + H_W + A_W - IN_DIM - MLA_ROPE), wi.dtype)], axis=1)


def _w_uq_layout(w):
    w = w.reshape(w.shape[0], HEADS, MLA_QK)
    rope = w[:, :, MLA_NOPE:]
    return jnp.concatenate([w[:, :, :MLA_NOPE], rope, _swap_rope_halves(rope)], axis=-1).reshape(w.shape[0], -1)


def _w_ukv_layout(w):
    w = w.reshape(w.shape[0], HEADS, MLA_NOPE + MLA_DV)
    return jnp.concatenate([w[:, :, :MLA_NOPE].reshape(w.shape[0], -1), w[:, :, MLA_NOPE:].reshape(w.shape[0], -1)],
                           axis=1)


def _seg_ones(n):
    i = np.arange(n) // HD
    return jnp.asarray((i[:, None] == i[None, :]).astype(np.float32), dtype=BF16)


def _rms(x, w):
    return x * lax.rsqrt(jnp.mean(x * x, axis=-1, keepdims=True) + EPS) * w


def _dot(a, b):
    return jnp.dot(a, b, preferred_element_type=F32)


def _dot_t(a, b):
    return lax.dot_general(a, b, (((1,), (1,)), ((), ())), preferred_element_type=F32)


def _split3(x):
    hi = x.astype(BF16)
    r1 = x - hi.astype(F32)
    mid = r1.astype(BF16)
    lo = (r1 - mid.astype(F32)).astype(BF16)
    return hi, mid, lo


def _split2(x):
    hi = x.astype(BF16)
    return hi, (x - hi.astype(F32)).astype(BF16)


def _seg_sum(x, seg):
    hi, lo = _split2(x)
    return _dot(hi, seg) + _dot(lo, seg)


def _mask_dot(mask01, x):
    hi, mid, lo = _split3(x)
    return _dot(mask01, hi) + _dot(mask01, mid) + _dot(mask01, lo)


def _dot_mask_t(x, mask01):
    hi, mid, lo = _split3(x)
    return _dot_t(hi, mask01) + _dot_t(mid, mask01) + _dot_t(lo, mask01)


def _softplus(x):
    return jnp.maximum(x, 0.0) + jnp.log1p(jnp.exp(-jnp.abs(x)))


def _sigmoid(x):
    return jax.nn.sigmoid(x)


def _silu(x):
    return x * jax.nn.sigmoid(x)


def _ada_kernel(c_ref, w_ref, b_ref, o_ref):
    s = _silu(c_ref[...]).astype(BF16)
    o_ref[0] = _dot(s, w_ref[0].astype(BF16)) + b_ref[0]


def _ada(cond, w_ada, b_ada):
    n = w_ada.shape[-1]
    tn = 1536
    return pl.pallas_call(
        _ada_kernel,
        grid=(DEPTH, n // tn),
        in_specs=[pl.BlockSpec((MOD_ROWS, D_MODEL), lambda l, j: (0, 0)),
                  pl.BlockSpec((1, D_MODEL, tn), lambda l, j: (l, 0, j)),
                  pl.BlockSpec((1, 1, tn), lambda l, j: (l, 0, j))],
        out_specs=pl.BlockSpec((1, MOD_ROWS, tn), lambda l, j: (l, 0, j)),
        out_shape=jax.ShapeDtypeStruct((DEPTH, MOD_ROWS, n), F32),
        compiler_params=pltpu.CompilerParams(dimension_semantics=("arbitrary", "arbitrary"),
                                             vmem_limit_bytes=VMEM_LIMIT),
        name="ada",
    )(cond, w_ada, b_ada.reshape(DEPTH, 1, n))


def _premix_kernel(x_ref, g_ref, sc_ref, sh_ref, w_ref, og_ref, oh_ref, oa_ref, vt_ref):
    h = _rms(x_ref[...], g_ref[...]) * (1.0 + sc_ref[0]) + sh_ref[0]
    hb = h.astype(BF16)
    og_ref[...] = _dot(hb, w_ref[:, 0:G_W])
    oh = _dot(hb, w_ref[:, G_W:G_W + H_W])
    oh_ref[...] = oh
    oa_ref[...] = _dot(hb, w_ref[:, G_W + H_W:G_W + H_W + A_W])
    vt_ref[0] = oh[:, LIN_W:2 * LIN_W].T


def _premix(x, g, sc, sh, w, tm, t_len):
    m = x.shape[0]
    tiles = m // tm
    per = tiles // sc.shape[0]
    row = lambda i: (i // per, 0, 0)
    tps = t_len // tm
    seq = lambda i: (i // tps, 0, i % tps)
    wt = G_W + H_W + A_W
    return pl.pallas_call(
        _premix_kernel,
        grid=(tiles,),
        in_specs=[pl.BlockSpec((tm, D_MODEL), lambda i: (i, 0)),
                  pl.BlockSpec((1, D_MODEL), lambda i: (0, 0)),
                  pl.BlockSpec((1, 1, D_MODEL), row),
                  pl.BlockSpec((1, 1, D_MODEL), row),
                  pl.BlockSpec((D_MODEL, wt), lambda i: (0, 0))],
        out_specs=[pl.BlockSpec((tm, G_W), lambda i: (i, 0)),
                   pl.BlockSpec((tm, H_W), lambda i: (i, 0)),
                   pl.BlockSpec((tm, A_W), lambda i: (i, 0)),
                   pl.BlockSpec((1, LIN_W, tm), seq)],
        out_shape=[jax.ShapeDtypeStruct((m, G_W), F32),
                   jax.ShapeDtypeStruct((m, H_W), F32),
                   jax.ShapeDtypeStruct((m, A_W), F32),
                   jax.ShapeDtypeStruct((m // t_len, LIN_W, t_len), F32)],
        compiler_params=pltpu.CompilerParams(dimension_semantics=("arbitrary",), vmem_limit_bytes=VMEM_LIMIT),
        name="premix",
    )(x, g, sc, sh, w)


def _keys_from_latent(ckv, krot, wukv_ref, k_out, v_out):
    kv = _dot(ckv.astype(BF16), wukv_ref[...])
    for h in range(HEADS):
        k_out[:, MLA_HEAD_PAD * h:MLA_HEAD_PAD * h + MLA_NOPE] = kv[:, MLA_NOPE * h:MLA_NOPE * (h + 1)].astype(BF16)
        k_out[:, MLA_HEAD_PAD * h + MLA_NOPE:MLA_HEAD_PAD * (h + 1)] = krot
    v_out[...] = kv[:, HEADS * MLA_NOPE:].astype(BF16)


def _mla_prep_kernel(tps, past, a_ref, qnw_ref, kvnw_ref, wuq_ref, wukv_ref, qa_ref, qb_ref, ck_ref, sk_ref, *refs):
    if past:
        cache_ckv_ref, cache_kr_ref, q_out, k_out, v_out, ckv_out = refs
    else:
        q_out, k_out, v_out, ckv_out = refs
    j = pl.program_id(1)

    @pl.when(j < tps)
    def _():
        qn = _rms(a_ref[:, 0:MLA_Q_RANK], qnw_ref[...])
        y = _dot(qn.astype(BF16), wuq_ref[...])
        z = pltpu.roll(y, HEADS * MLA_HEAD_PAD - MLA_ROPE, 1)
        qa = qa_ref[...]
        qb = qb_ref[...]
        for h in range(HEADS):
            sl = slice(MLA_HEAD_PAD * h, MLA_HEAD_PAD * (h + 1))
            q_out[:, sl] = (y[:, sl] * qa + z[:, sl] * qb).astype(BF16)
        ckv = _rms(a_ref[:, MLA_Q_RANK:A_KR], kvnw_ref[...])
        ckv_out[...] = ckv
        kr = a_ref[:, A_KR:A_AB]
        krot = (kr * ck_ref[...] + pltpu.roll(kr, 64, 1) * sk_ref[...]).astype(BF16)
        _keys_from_latent(ckv, krot, wukv_ref, k_out.at[0], v_out.at[0])

    if past:
        @pl.when(j == tps)
        def _():
            _keys_from_latent(cache_ckv_ref[0], cache_kr_ref[0].astype(BF16), wukv_ref,
                              k_out.at[0, 0:past], v_out.at[0, 0:past])


def _mla_prep(oa, qnw, kvnw, wuq, wukv, qa, qb, ck, sk, tm, t_len, cache=None):
    m = oa.shape[0]
    b = m // t_len
    tps = t_len // tm
    past = 0 if cache is None else cache[0].shape[1]
    tok = lambda bi, j: (bi * tps + jnp.minimum(j, tps - 1), 0)
    pos = lambda bi, j: (jnp.minimum(j, tps - 1), 0)
    full = lambda bi, j: (0, 0)
    seq = lambda bi, j: (bi, j, 0)
    kw = HEADS * MLA_HEAD_PAD
    in_specs = [pl.BlockSpec((tm, A_W), tok),
                pl.BlockSpec((1, MLA_Q_RANK), full),
                pl.BlockSpec((1, MLA_KV_RANK), full),
                pl.BlockSpec((MLA_Q_RANK, kw), full),
                pl.BlockSpec((MLA_KV_RANK, HEADS * (MLA_NOPE + MLA_DV)), full),
                pl.BlockSpec((tm, MLA_HEAD_PAD), pos),
                pl.BlockSpec((tm, MLA_HEAD_PAD), pos),
                pl.BlockSpec((tm, 128), pos),
                pl.BlockSpec((tm, 128), pos)]
    operands = [oa, qnw, kvnw, wuq, wukv, qa, qb, ck, sk]
    if past:
        in_specs += [pl.BlockSpec((1, past, MLA_KV_RANK), lambda bi, j: (bi, 0, 0)),
                     pl.BlockSpec((1, past, 128), lambda bi, j: (bi, 0, 0))]
        operands += list(cache)
    return pl.pallas_call(
        functools.partial(_mla_prep_kernel, tps, past),
        grid=(b, tps + (1 if past else 0)),
        in_specs=in_specs,
        out_specs=[pl.BlockSpec((tm, kw), tok),
                   pl.BlockSpec((1, tm, kw), seq),
                   pl.BlockSpec((1, tm, HEADS * MLA_DV), seq),
                   pl.BlockSpec((tm, MLA_KV_RANK), tok)],
        out_shape=[jax.ShapeDtypeStruct((m, kw), BF16),
                   jax.ShapeDtypeStruct((b, t_len + past, kw), BF16),
                   jax.ShapeDtypeStruct((b, t_len + past, HEADS * MLA_DV), BF16),
                   jax.ShapeDtypeStruct((m, MLA_KV_RANK), F32)],
        compiler_params=pltpu.CompilerParams(dimension_semantics=("arbitrary", "arbitrary"),
                                             vmem_limit_bytes=VMEM_LIMIT),
        name="mla_prep",
    )(*operands)


def _attn_kernel(q_ref, k_ref, v_ref, o_ref):
    for h in range(HEADS):
        sl = slice(MLA_HEAD_PAD * h, MLA_HEAD_PAD * (h + 1))
        s = _dot_t(q_ref[0, :, sl], k_ref[0, :, sl])
        p = jnp.exp2(s - jnp.max(s, axis=-1, keepdims=True))
        l = jnp.sum(p, axis=-1, keepdims=True)
        o = _dot(p.astype(BF16), v_ref[0, :, MLA_DV * h:MLA_DV * (h + 1)])
        o_ref[0, :, MLA_DV * h:MLA_DV * (h + 1)] = o / l


def _attn(q, k, v, tq):
    b, t, kw = q.shape
    s = k.shape[1]
    return pl.pallas_call(
        _attn_kernel,
        grid=(b, t // tq),
        in_specs=[pl.BlockSpec((1, tq, kw), lambda bi, i: (bi, i, 0)),
                  pl.BlockSpec((1, s, kw), lambda bi, i: (bi, 0, 0)),
                  pl.BlockSpec((1, s, HEADS * MLA_DV), lambda bi, i: (bi, 0, 0))],
        out_specs=pl.BlockSpec((1, tq, HEADS * MLA_DV), lambda bi, i: (bi, i, 0)),
        out_shape=jax.ShapeDtypeStruct((b, t, HEADS * MLA_DV), F32),
        compiler_params=pltpu.CompilerParams(dimension_semantics=("arbitrary", "arbitrary"),
                                             vmem_limit_bytes=VMEM_LIMIT),
        name="attn",
    )(q, k, v)


def _gdn_prep_kernel(tps, cur_ref, prev_ref, next_ref, w_ref, seg_ref, q_out, k_out, v_out, kt_out):
    i = pl.program_id(0)
    tm = cur_ref.shape[0]
    has_prev = ((i % tps) != 0).astype(F32)
    has_next = ((i % tps) != (tps - 1)).astype(F32)
    xc = jnp.concatenate([prev_ref[...] * has_prev, cur_ref[...], next_ref[...] * has_next], axis=0)
    n = tm + 16
    y = xc[8:8 + tm] * w_ref[2:3, :]
    for j in (0, 1, 3, 4):
        d = j - 2
        y = y + pltpu.roll(xc, (n - d) % n, 0)[8:8 + tm] * w_ref[j:j + 1, :]
    y = _silu(y)
    seg = seg_ref[...]
    qk = y[:, 0:2 * LIN_W]
    qk = qk * lax.rsqrt(_seg_sum(qk * qk, seg) + EPS)
    q_out[...] = qk[:, 0:LIN_W] * (HD ** -0.5)
    k_out[...] = qk[:, LIN_W:2 * LIN_W]
    kt_out[0] = qk[:, LIN_W:2 * LIN_W].T
    v_out[...] = y[:, 2 * LIN_W:]


def _gdn_prep(og, convw, seg, tm, t_len):
    m = og.shape[0]
    tiles = m // tm
    tps = t_len // tm
    r8 = tm // 8
    last8 = m // 8 - 1
    return pl.pallas_call(
        functools.partial(_gdn_prep_kernel, tps),
        grid=(tiles,),
        in_specs=[pl.BlockSpec((tm, G_CONV), lambda i: (i, 0)),
                  pl.BlockSpec((8, G_CONV), lambda i: (jnp.maximum(i * r8 - 1, 0), 0)),
                  pl.BlockSpec((8, G_CONV), lambda i: (jnp.minimum((i + 1) * r8, last8), 0)),
                  pl.BlockSpec((8, G_CONV), lambda i: (0, 0)),
                  pl.BlockSpec((2 * LIN_W, 2 * LIN_W), lambda i: (0, 0))],
        out_specs=[pl.BlockSpec((tm, LIN_W), lambda i: (i, 0)),
                   pl.BlockSpec((tm, LIN_W), lambda i: (i, 0)),
                   pl.BlockSpec((tm, LIN_W), lambda i: (i, 0)),
                   pl.BlockSpec((1, LIN_W, tm), lambda i: (i // tps, 0, i % tps))],
        out_shape=[jax.ShapeDtypeStruct((m, LIN_W), F32),
                   jax.ShapeDtypeStruct((m, LIN_W), F32),
                   jax.ShapeDtypeStruct((m, LIN_W), F32),
                   jax.ShapeDtypeStruct((m // t_len, LIN_W, t_len), F32)],
        compiler_params=pltpu.CompilerParams(dimension_semantics=("arbitrary",), vmem_limit_bytes=VMEM_LIMIT),
        name="gdn_prep",
    )(og, og, og, convw, seg)


def _cumsum_masks():
    r = np.arange(SCAN_BLOCK)[:, None]
    c = np.arange(SCAN_BLOCK)[None, :]
    masks = []
    for chunk in (GDN_CHUNK, HG_CHUNK):
        same = (r // chunk) == (c // chunk)
        masks += [same & (r >= c), same & (r <= c)]
    return jnp.asarray(np.stack(masks).astype(np.float32), dtype=BF16)


def _gdn_units(d, incl_blk, q_ref, k_ref, v_ref, kt_ref, ab_ref, gc_ref, o_ref):
    c = GDN_CHUNK
    ri = lax.broadcasted_iota(jnp.int32, (c, c), 0)
    ci = lax.broadcasted_iota(jnp.int32, (c, c), 1)
    incl = (ri >= ci) if d == 0 else (ri <= ci)
    strict = (ri > ci) if d == 0 else (ri < ci)
    last = c - 1 if d == 0 else 0
    ab = ab_ref[0]
    g_cols = gc_ref[0:1, :] * _softplus(ab + gc_ref[1:2, :])
    beta = _sigmoid(ab)
    gcs_blk = _mask_dot(incl_blk, g_cols)
    gcr_blk = gcs_blk.T
    n_chunks = SCAN_BLOCK // c
    units = []
    for rank, cc in enumerate(range(n_chunks) if d == 0 else range(n_chunks - 1, -1, -1)):
        rows = slice(cc * c, (cc + 1) * c)
        for h in range(HEADS):
            j = d * HEADS + h
            hs = slice(HD * h, HD * (h + 1))
            gcol = gcs_blk[rows, j:j + 1]
            grow = gcr_blk[j:j + 1, rows]
            bcol = beta[rows, 8 + j:9 + j]
            gl = gcol[last:last + 1, :]
            eg = jnp.exp(gcol)
            k = k_ref[0, rows, hs]
            units.append(dict(
                rank=rank, j=j, rows=rows, hs=hs, o_ref=o_ref, strict=strict, gl=gl, bcol=bcol, eg=eg,
                q=q_ref[0, rows, hs], k=k,
                k_tail=(kt_ref[0, hs, rows] * jnp.exp(gl - grow)).astype(BF16),
                decay=jnp.where(incl, jnp.exp(jnp.where(incl, gcol - grow, 0.0)), 0.0),
                x=jnp.concatenate([v_ref[0, rows, hs] * bcol, k * (bcol * eg)], axis=1)))
    return units


def _gdn_stages(units, s_ref):
    c = GDN_CHUNK
    ri = lax.broadcasted_iota(jnp.int32, (c, c), 0)
    ci = lax.broadcasted_iota(jnp.int32, (c, c), 1)
    diag_blk = jnp.right_shift(ri, 4) == jnp.right_shift(ci, 4)
    eye = (ri == ci).astype(F32)

    def gram():
        for u in units:
            kb16 = u["k"].astype(BF16)
            u["kk"] = _dot_t((u["k"] * u["bcol"]).astype(BF16), kb16)
            u["qk"] = _dot_t(u["q"].astype(BF16), kb16)

    def split():
        for u in units:
            p = jnp.where(u["strict"], u["kk"] * u["decay"], 0.0)
            pd = jnp.where(diag_blk, p, 0.0)
            u["att"] = (u["qk"] * u["decay"]).astype(BF16)
            u["poff"] = (p - pd).astype(BF16)
            u["td"] = eye - pd
            u["a"] = pd.astype(BF16)

    def square():
        for u in units:
            u["a"] = _dot(u["a"], u["a"]).astype(BF16)

    def extend():
        for u in units:
            u["td"] = u["td"] + _dot(u["td"].astype(BF16), u["a"])

    def apply_diag():
        for u in units:
            yn = _dot(u["td"].astype(BF16), jnp.concatenate([u["x"].astype(BF16), u["poff"]], axis=1))
            u["y"] = yn[:, 0:2 * HD]
            u["n"] = yn[:, 2 * HD:3 * HD].astype(BF16)
            u["x"] = u["y"]

    def substitute():
        for u in units:
            u["x"] = u["y"] - _dot(u["n"], u["x"].astype(BF16))

    def new_values(rank):
        for u in units:
            if u["rank"] == rank:
                u["s"] = s_ref[0, u["j"]]
                u["sb"] = u["s"].astype(BF16)
                u["vb"] = (u["x"][:, 0:HD] - _dot(u["x"][:, HD:2 * HD].astype(BF16), u["sb"])).astype(BF16)

    def emit(rank):
        for u in units:
            if u["rank"] == rank:
                u["o_ref"][0, u["rows"], u["hs"]] = (_dot((u["q"] * u["eg"]).astype(BF16), u["sb"])
                                                     + _dot(u["att"], u["vb"]))
                s_ref[0, u["j"]] = u["s"] * jnp.exp(u["gl"]) + _dot(u["k_tail"], u["vb"])

    stages = [gram, split] + [square, extend] * 3 + [apply_diag] + [substitute] * 3
    for rank in range(SCAN_BLOCK // c):
        stages += [functools.partial(new_values, rank), functools.partial(emit, rank)]
    return stages


def _hgrn_lower_bound(layer, lb_ref):
    raw = lb_ref[0:DEPTH, :]
    e = jnp.exp(raw - jnp.max(raw, axis=0, keepdims=True))
    gamma = e / jnp.sum(e, axis=0, keepdims=True)
    lb = jnp.zeros((1, LIN_W), F32)
    for i in range(1, layer + 1):
        lb = lb + gamma[i:i + 1, :]
    return lb


def _hgrn_block(d, incl01, lb, q_ref, v_ref, f_ref):
    c = HG_CHUNK
    oml = 1.0 - lb
    q = q_ref[0]
    f = f_ref[0]
    logf = jnp.log(jnp.maximum(lb + oml * _sigmoid(f), GATE_FLOOR))
    bc = _mask_dot(incl01, logf)
    bc2 = bc * LOG2E
    src2 = bc2 - (jnp.log2(oml) - _softplus(f) * LOG2E)
    return dict(d=d, q=q, v=v_ref[0], kk=oml * _sigmoid(-f), bc=bc, bc2=bc2, src2=src2,
                qe=(q * jnp.exp(bc)).astype(BF16), last=c - 1 if d == 0 else 0)


def _hgrn_live_halves(d, s):
    half = HG_CHUNK // 2
    if d == 0:
        return (1,) if s >= half else (0, 1)
    return (0,) if s < half else (0, 1)


def _hgrn_chunk_build(blk, cc):
    c = HG_CHUNK
    half = c // 2
    d = blk["d"]
    base = cc * c
    rows = slice(base, base + c)
    kk, bc = blk["kk"][rows], blk["bc"][rows]
    bl = bc[blk["last"]:blk["last"] + 1, :]
    row_half = lax.broadcasted_iota(jnp.int32, (half, LIN_W), 0)
    zero_half = jnp.zeros((half, LIN_W), F32)
    parts = []
    for s in range(c):
        bs = blk["src2"][base + s:base + s + 1]
        for hf in (0, 1):
            if hf not in _hgrn_live_halves(d, s):
                parts.append(zero_half)
                continue
            tr = slice(base + hf * half, base + (hf + 1) * half)
            col = blk["q"][tr] * jnp.exp2(blk["bc2"][tr] - bs)
            if hf == s // half:
                rid = row_half + hf * half
                col = jnp.where((rid >= s) if d == 0 else (rid <= s), col, 0.0)
            parts.append(col)
    return dict(cols=jnp.concatenate(parts, axis=0).astype(BF16), k_tail=(kk * jnp.exp(bl - bc)).astype(BF16),
                decay=jnp.exp(bl))


def _hgrn_chunk_launch(blk, cc, built, vt_ref, ones, st):
    rows = slice(cc * HG_CHUNK, (cc + 1) * HG_CHUNK)
    return dict(
        r=_dot(built["cols"], ones),
        o_inter=_dot_t(blk["qe"][rows], st.astype(BF16)),
        upd=_dot(vt_ref[0, :, rows].astype(BF16), built["k_tail"]),
        decay=built["decay"])


def _hgrn_chunk_finish(blk, cc, pend, seg_mask, st, o_ref):
    c = HG_CHUNK
    half = c // 2
    base = cc * c
    r = pend["r"]
    o_half = [pend["o_inter"][0:half], pend["o_inter"][half:c]]
    for s in range(c):
        vs = blk["v"][base + s:base + s + 1]
        for hf in _hgrn_live_halves(blk["d"], s):
            o_half[hf] = o_half[hf] + r[s * c + hf * half:s * c + (hf + 1) * half, :] * vs
    o_ref[0, base:base + half, :] = o_half[0]
    o_ref[0, base + half:base + c, :] = o_half[1]
    return st * pend["decay"] + pend["upd"] * seg_mask


def _hgrn_steps(lb, masks_ref, fwd_refs, bwd_refs, ones, st_ref):
    seg_mask = ones.astype(F32)
    (qf, vf, ff, vtf, of_ref), (qb, vb, fb, vtb, ob_ref) = fwd_refs, bwd_refs
    fwd = _hgrn_block(0, masks_ref[2], lb, qf, vf, ff)
    bwd = _hgrn_block(1, masks_ref[3], lb, qb, vb, fb)
    carry = {0: st_ref[0, 0], 1: st_ref[0, 1]}
    built, pending = {}, {}
    n_chunks = SCAN_BLOCK // HG_CHUNK
    chunk_of = {0: lambda step: step, 1: lambda step: n_chunks - 1 - step}
    side = {0: (fwd, vtf, of_ref), 1: (bwd, vtb, ob_ref)}

    def build(d, step):
        built[(d, step)] = _hgrn_chunk_build(side[d][0], chunk_of[d](step))

    def launch(d, step):
        blk, vt_ref, _ = side[d]
        pending[(d, step)] = _hgrn_chunk_launch(blk, chunk_of[d](step), built.pop((d, step)), vt_ref, ones, carry[d])

    def finish(d, step):
        blk, _, o_ref = side[d]
        carry[d] = _hgrn_chunk_finish(blk, chunk_of[d](step), pending.pop((d, step)), seg_mask, carry[d], o_ref)

    def flush():
        st_ref[0, 0] = carry[0]
        st_ref[0, 1] = carry[1]

    groups = []
    for slot in range(n_chunks + 2):
        group = []
        for phase, step in ((finish, slot - 2), (launch, slot - 1), (build, slot)):
            if 0 <= step < n_chunks:
                group += [functools.partial(phase, 0, step), functools.partial(phase, 1, step)]
        groups.append(group)
    return groups, flush


def _scan_kernel(layer, masks_ref,
                 gqf, gkf, gvkf, gktf, gabf, gqb, gkb, gvkb, gktb, gabb, gc_ref, gs0_ref,
                 hqf, hvf, hff, hvtf, hqb, hvb, hfb, hvtb, lb_ref, ones_ref, hs0_ref,
                 gof_ref, gob_ref, gs_ref, hof_ref, hob_ref, hst_ref):
    @pl.when(pl.program_id(1) == 0)
    def _():
        gs_ref[...] = gs0_ref[...]
        hst_ref[...] = hs0_ref[...]

    units = (_gdn_units(0, masks_ref[0], gqf, gkf, gvkf, gktf, gabf, gc_ref, gof_ref)
             + _gdn_units(1, masks_ref[1], gqb, gkb, gvkb, gktb, gabb, gc_ref, gob_ref))
    stages = _gdn_stages(units, gs_ref)
    groups, flush = _hgrn_steps(_hgrn_lower_bound(layer, lb_ref), masks_ref, (hqf, hvf, hff, hvtf, hof_ref),
                                (hqb, hvb, hfb, hvtb, hob_ref), ones_ref[...], hst_ref)
    per_stage = -(-len(groups) // len(stages))
    for stage in stages:
        stage()
        for group in groups[:per_stage]:
            for step in group:
                step()
        groups = groups[per_stage:]
    for group in groups:
        for step in group:
            step()
    flush()


def _lin_scan(layer, gq, gk, gvk, gkt, oa, gconst, gs0, oh, vt, lb, ones, hs0):
    b, t, _ = gq.shape
    blk = SCAN_BLOCK
    nb = t // blk

    def tok(cb, rev):
        return (lambda bi, i: (bi, nb - 1 - i, cb)) if rev else (lambda bi, i: (bi, i, cb))

    def lanes(rev):
        return (lambda bi, i: (bi, 0, nb - 1 - i)) if rev else (lambda bi, i: (bi, 0, i))

    def gdn_specs(rev):
        return [pl.BlockSpec((1, blk, LIN_W), tok(0, rev)),
                pl.BlockSpec((1, blk, LIN_W), tok(0, rev)),
                pl.BlockSpec((1, blk, LIN_W), tok(0, rev)),
                pl.BlockSpec((1, LIN_W, blk), lanes(rev)),
                pl.BlockSpec((1, blk, 128), tok(A_AB // 128, rev))]

    def hgrn_specs(rev):
        f_col = 3 if rev else 2
        return [pl.BlockSpec((1, blk, LIN_W), tok(0, rev)),
                pl.BlockSpec((1, blk, LIN_W), tok(1, rev)),
                pl.BlockSpec((1, blk, LIN_W), tok(f_col, rev)),
                pl.BlockSpec((1, LIN_W, blk), lanes(rev))]

    const = lambda r, c: pl.BlockSpec((r, c), lambda bi, i: (0, 0))
    g_state = pl.BlockSpec((1, 2 * HEADS, HD, HD), lambda bi, i: (bi, 0, 0, 0))
    h_state = pl.BlockSpec((1, 2, LIN_W, LIN_W), lambda bi, i: (bi, 0, 0, 0))
    out_tok = lambda rev: pl.BlockSpec((1, blk, LIN_W), tok(0, rev))
    seq = jax.ShapeDtypeStruct((b, t, LIN_W), F32)
    return pl.pallas_call(
        functools.partial(_scan_kernel, layer),
        grid=(b, nb),
        in_specs=[pl.BlockSpec((4, blk, blk), lambda bi, i: (0, 0, 0))]
        + gdn_specs(False) + gdn_specs(True) + [const(8, 128), g_state]
        + hgrn_specs(False) + hgrn_specs(True) + [const(8, LIN_W), const(LIN_W, LIN_W), h_state],
        out_specs=[out_tok(False), out_tok(True), g_state, out_tok(False), out_tok(True), h_state],
        out_shape=[seq, seq, jax.ShapeDtypeStruct((b, 2 * HEADS, HD, HD), F32),
                   seq, seq, jax.ShapeDtypeStruct((b, 2, LIN_W, LIN_W), F32)],
        compiler_params=pltpu.CompilerParams(dimension_semantics=("arbitrary", "arbitrary"),
                                             vmem_limit_bytes=VMEM_LIMIT),
        name="lin_scan",
    )(_cumsum_masks(), gq, gk, gvk, gkt, oa, gq, gk, gvk, gkt, oa, gconst, gs0,
      oh, oh, oh, vt, oh, oh, oh, vt, lb, ones, hs0)


def _post_kernel(x_ref, ogf_ref, ogb_ref, z_ref, gnw_ref, ohf_ref, ohb_ref, hg_ref, hnw_ref, om_ref, seg_ref,
                 wout_ref, gpm_ref, gt1_ref, gpf_ref, sc2_ref, sh2_ref, gt2_ref, gff_ref,
                 wf_ref, wo_ref, out_ref):
    seg = seg_ref[...]
    inv = 1.0 / HD
    og = ogf_ref[...] + ogb_ref[...]
    og = og * lax.rsqrt(_seg_sum(og * og, seg) * inv + EPS) * gnw_ref[...] * _silu(z_ref[...])
    oh = ohf_ref[...] + ohb_ref[...]
    oh = oh * lax.rsqrt(_seg_sum(oh * oh, seg) * inv + EPS) * hnw_ref[...] * _sigmoid(hg_ref[...])
    mix = jnp.concatenate([og.astype(BF16), oh.astype(BF16), om_ref[...].astype(BF16)], axis=-1)
    x1 = x_ref[...] + gt1_ref[0] * _rms(_dot(mix, wout_ref[...]), gpm_ref[...])
    hb = (_rms(x1, gpf_ref[...]) * (1.0 + sc2_ref[0]) + sh2_ref[0]).astype(BF16)
    fw = D_FF // FF_SPLIT
    y = None
    for part in range(FF_SPLIT):
        cs = slice(part * fw, (part + 1) * fw)
        gs = slice(D_FF + part * fw, D_FF + (part + 1) * fw)
        act = (_silu(_dot(hb, wf_ref[:, cs])) * _dot(hb, wf_ref[:, gs])).astype(BF16)
        contrib = _dot(act, wo_ref[cs, :])
        y = contrib if y is None else y + contrib
    out_ref[...] = x1 + gt2_ref[0] * _rms(y, gff_ref[...])


def _post(x, ogf, ogb, og, gnw, ohf, ohb, oh, hnw, om, seg, wout, gpm, gt1, gpf, sc2, sh2, gt2, gff, wf, wo, tm):
    m = x.shape[0]
    tiles = m // tm
    per = tiles // gt1.shape[0]
    row = lambda i: (i // per, 0, 0)
    tok = lambda w: pl.BlockSpec((tm, w), lambda i: (i, 0))
    vec = lambda w: pl.BlockSpec((1, w), lambda i: (0, 0))
    mod = pl.BlockSpec((1, 1, D_MODEL), row)
    once = lambda r, c: pl.BlockSpec((r, c), lambda i: (0, 0), pipeline_mode=pl.Buffered(1))
    return pl.pallas_call(
        _post_kernel,
        grid=(tiles,),
        in_specs=[tok(D_MODEL), tok(LIN_W), tok(LIN_W),
                  pl.BlockSpec((tm, LIN_W), lambda i: (i, G_W // LIN_W - 1)), vec(LIN_W),
                  tok(LIN_W), tok(LIN_W),
                  pl.BlockSpec((tm, LIN_W), lambda i: (i, H_W // LIN_W - 1)), vec(LIN_W),
                  tok(HEADS * MLA_DV), once(LIN_W, LIN_W),
                  once(D_MODEL, D_MODEL), vec(D_MODEL), mod, vec(D_MODEL), mod, mod, mod, vec(D_MODEL),
                  once(D_MODEL, 2 * D_FF), once(D_FF, D_MODEL)],
        out_specs=tok(D_MODEL),
        out_shape=jax.ShapeDtypeStruct((m, D_MODEL), F32),
        compiler_params=pltpu.CompilerParams(dimension_semantics=("arbitrary",), vmem_limit_bytes=VMEM_LIMIT),
        name="post",
    )(x, ogf, ogb, og, gnw, ohf, ohb, oh, hnw, om, seg, wout, gpm, gt1, gpf, sc2, sh2, gt2, gff, wf, wo)


def _rope_tables(t_len, use_rope):
    scale = MLA_QK ** -0.5 * LOG2E
    if use_rope:
        rows = t_len // GRID_W
        row = np.repeat(np.arange(rows, dtype=np.float64), GRID_W)
        col = np.tile(np.arange(GRID_W, dtype=np.float64), rows)
        nf = MLA_ROPE // 4
        inv = ROPE_BASE ** (-np.arange(nf, dtype=np.float64) / nf)
        ar, ac = row[:, None] * inv, col[:, None] * inv
        cos = np.concatenate([np.cos(ar), np.cos(ar), np.cos(ac), np.cos(ac)], axis=-1)
        sin = np.concatenate([-np.sin(ar), np.sin(ar), -np.sin(ac), np.sin(ac)], axis=-1)
    else:
        cos = np.ones((t_len, MLA_ROPE))
        sin = np.zeros((t_len, MLA_ROPE))
    z64 = np.zeros((t_len, MLA_ROPE))
    qa = scale * np.concatenate([np.ones((t_len, MLA_NOPE)), cos, z64], axis=-1)
    qb = scale * np.concatenate([np.zeros((t_len, MLA_NOPE)), sin, z64], axis=-1)
    ck = np.concatenate([cos, z64], axis=-1)
    sk = np.concatenate([sin, z64], axis=-1)
    return tuple(jnp.asarray(a, dtype=F32) for a in (qa, qb, ck, sk))


def _layer_weights(l, w_in, w_out, gdn_conv_w, gdn_a_log, gdn_dt_bias, gdn_norm_w, lower_bounds, hgrn_norm_w,
                   mla_q_norm_w, mla_w_uq, mla_kv_norm_w, mla_w_ukv, w_ffn_in, w_ffn_out,
                   g_pre_mix, g_post_mix, g_pre_ffn, g_post_ffn):
    w_cat = _w_in_layout(w_in[l].astype(BF16))
    convw = jnp.concatenate([gdn_conv_w[l].T, jnp.zeros((3, G_CONV), F32)], axis=0)
    neg_a = -jnp.exp(gdn_a_log[l].astype(F32)).reshape(1, 8)
    dt = gdn_dt_bias[l].astype(F32).reshape(1, 8)
    pad = lambda r: jnp.pad(r, ((0, 0), (0, 128 - r.shape[1])))
    gconst = jnp.concatenate([pad(neg_a), pad(dt), jnp.zeros((6, 128), F32)], axis=0)
    vec = lambda a: a.reshape(1, -1).astype(F32)
    return dict(
        w_cat=w_cat, convw=convw, gconst=gconst,
        gnw=vec(jnp.tile(gdn_norm_w[l], HEADS)), hnw=vec(jnp.tile(hgrn_norm_w[l], HEADS)),
        layer=l, lb=lower_bounds,
        qnw=vec(mla_q_norm_w[l]), kvnw=vec(mla_kv_norm_w[l]),
        wuq=_w_uq_layout(mla_w_uq[l].astype(BF16)), wukv=_w_ukv_layout(mla_w_ukv[l].astype(BF16)),
        wout=w_out[l].astype(BF16), wf=w_ffn_in[l].astype(BF16),
        wo=w_ffn_out[l].astype(BF16),
        g_pre_mix=vec(g_pre_mix[l]), g_post_mix=vec(g_post_mix[l]),
        g_pre_ffn=vec(g_pre_ffn[l]), g_post_ffn=vec(g_post_ffn[l]))


def _trunk_layer(x, mod, w, gdn_s0, hg_s0, tables, ctx_cache, seg512, seg256):
    b, t, _ = x.shape
    m = b * t
    tm = min(TOKEN_TILE, t)
    sh1, sc1, gt1, sh2, sc2, gt2 = mod
    xf = x.reshape(m, D_MODEL)
    og, oh, oa, vt = _premix(xf, w["g_pre_mix"], sc1, sh1, w["w_cat"], tm, t)

    qa, qb, ck, sk = tables
    qcat, kcat, vcat, ckv = _mla_prep(oa, w["qnw"], w["kvnw"], w["wuq"], w["wukv"], qa, qb, ck, sk, tm, t, ctx_cache)
    o_mla = _attn(qcat.reshape(b, t, HEADS * MLA_HEAD_PAD), kcat, vcat, min(ATTN_Q_TILE, t)).reshape(m, HEADS * MLA_DV)

    gq, gk, gvk, gkt = _gdn_prep(og, w["convw"], seg512, tm, t)
    ogf, ogb, gdn_state, ohf, ohb, hg_state = _lin_scan(
        w["layer"], gq.reshape(b, t, LIN_W), gk.reshape(b, t, LIN_W), gvk.reshape(b, t, LIN_W), gkt,
        oa.reshape(b, t, A_W), w["gconst"], gdn_s0,
        oh.reshape(b, t, H_W), vt, w["lb"], seg256, hg_s0)

    x_new = _post(xf, ogf.reshape(m, LIN_W), ogb.reshape(m, LIN_W), og, w["gnw"],
                  ohf.reshape(m, LIN_W), ohb.reshape(m, LIN_W), oh, w["hnw"], o_mla, seg256,
                  w["wout"], w["g_post_mix"], gt1, w["g_pre_ffn"], sc2, sh2, gt2, w["g_post_ffn"],
                  w["wf"], w["wo"], tm)
    mkr = oa[:, A_KR:A_KR + MLA_ROPE]
    return x_new.reshape(b, t, D_MODEL), gdn_state, hg_state, ckv.reshape(b, t, MLA_KV_RANK), mkr.reshape(b, t, MLA_ROPE)


def _hg_state_to_block(s):
    st = jnp.swapaxes(s, -1, -2)
    eye = jnp.eye(HEADS, dtype=s.dtype)
    big = st[:, :, :, :, None, :] * eye[None, None, :, None, :, None]
    return big.reshape(s.shape[0], 2, LIN_W, LIN_W)


def _hg_block_to_state(big):
    b = big.shape[0]
    r = big.reshape(b, 2, HEADS, HD, HEADS, HD)
    diag = jnp.stack([r[:, :, h, :, h, :] for h in range(HEADS)], axis=2)
    return jnp.swapaxes(diag, -1, -2)


def kernel(x_prompt, x_sample, cache_mla_ckv, cache_mla_krope, state_gdn, state_hgrn, c, c_ctx, w_ada, b_ada,
           g_pre_mix, g_post_mix, g_pre_ffn, g_post_ffn, w_in, w_out, gdn_conv_w, gdn_a_log, gdn_dt_bias,
           gdn_norm_w, hgrn_lb, hgrn_norm_w, mla_q_norm_w, mla_w_uq, mla_kv_norm_w, mla_w_ukv, w_ffn_in, w_ffn_out):
    bp, tp, _ = x_prompt.shape
    bd, td, _ = x_sample.shape
    past = cache_mla_ckv.shape[2]
    assert 1 + bd <= MOD_ROWS and tp % SCAN_BLOCK == 0 and td % SCAN_BLOCK == 0

    lower_bounds = jnp.pad(hgrn_lb.astype(F32), ((0, 8 - DEPTH), (0, 0)))

    cond = jnp.concatenate([c_ctx[None, :], c, jnp.zeros((MOD_ROWS - 1 - bd, D_MODEL), F32)], axis=0)
    mod_all = _ada(cond, w_ada, b_ada)

    seg512 = _seg_ones(2 * LIN_W)
    seg256 = _seg_ones(LIN_W)
    tab_ctx = _rope_tables(tp, False)
    tab_lat = _rope_tables(td, True)

    weights = [_layer_weights(l, w_in, w_out, gdn_conv_w, gdn_a_log, gdn_dt_bias, gdn_norm_w, lower_bounds,
                              hgrn_norm_w, mla_q_norm_w, mla_w_uq, mla_kv_norm_w, mla_w_ukv, w_ffn_in, w_ffn_out,
                              g_pre_mix, g_post_mix, g_pre_ffn, g_post_ffn) for l in range(DEPTH)]

    def mods(l, lo, n):
        rows = mod_all[l, lo:lo + n].reshape(n, 1, 6 * D_MODEL)
        return tuple(rows[:, :, i * D_MODEL:(i + 1) * D_MODEL] for i in range(6))

    xp = x_prompt
    zero_g = jnp.zeros((bp, 2 * HEADS, HD, HD), F32)
    zero_h = jnp.zeros((bp, 2, LIN_W, LIN_W), F32)
    ckv_l, kr_l, gs_l, hs_l = [], [], [], []
    for l in range(DEPTH):
        xp, gs, hs, ckv, mkr = _trunk_layer(xp, mods(l, 0, 1), weights[l], zero_g, zero_h, tab_ctx, None,
                                            seg512, seg256)
        ckv_l.append(ckv)
        kr_l.append(mkr)
        gs_l.append(gs.reshape(bp, 2, HEADS, HD, HD))
        hs_l.append(_hg_block_to_state(hs))

    xs = x_sample
    for l in range(DEPTH):
        ctx_cache = (cache_mla_ckv[:, l], jnp.pad(cache_mla_krope[:, l], ((0, 0), (0, 0), (0, 128 - MLA_ROPE))))
        xs, _, _, _, _ = _trunk_layer(xs, mods(l, 1, bd), weights[l],
                                      state_gdn[:, l].reshape(bd, 2 * HEADS, HD, HD),
                                      _hg_state_to_block(state_hgrn[:, l]), tab_lat, ctx_cache, seg512, seg256)

    return (xp, xs, jnp.stack(ckv_l, axis=1), jnp.stack(kr_l, axis=1),
            jnp.stack(gs_l, axis=1), jnp.stack(hs_l, axis=1))
```

```python
import functools

import numpy as np
import jax
import jax.numpy as jnp
from jax import lax
from jax.experimental import pallas as pl
from jax.experimental.pallas import tpu as pltpu

F32 = jnp.float32
BF16 = jnp.bfloat16

D_MODEL = 1024
DEPTH = 2
GRID_W = 64
EPS = 1e-6
GATE_FLOOR = 1e-30
HEADS = 4
HD = 64
LIN_W = HEADS * HD
GDN_CHUNK = 64
HG_CHUNK = 16
SCAN_BLOCK = 256
LOG2E = 1.4426950408889634
MLA_Q_RANK = 384
MLA_KV_RANK = 256
MLA_NOPE = 128
MLA_ROPE = 64
MLA_DV = 128
MLA_QK = MLA_NOPE + MLA_ROPE
MLA_HEAD_PAD = 256
ROPE_BASE = 10000.0
D_FF = -(-8 * D_MODEL // (3 * 256)) * 256
FF_SPLIT = 11
TOKEN_TILE = 512
ATTN_Q_TILE = 256
ADA_COL_TILE = 1536
MOD_ROWS = 16
VMEM_LIMIT = 56 * 1024 * 1024

_OFF = {}
_o = 0
for _n, _s in (("gq", 256), ("gk", 256), ("gv", 256), ("gz", 256), ("ga", 8), ("gb", 8),
               ("hq", 256), ("hi", 256), ("hf", 512), ("hg", 256),
               ("mcq", MLA_Q_RANK), ("mckv", MLA_KV_RANK), ("mkr", MLA_ROPE)):
    _OFF[_n] = _o
    _o += _s
IN_DIM = _o

G_W = 1024
G_CONV = 768
H_W = 1280
A_W = 896
A_KR = MLA_Q_RANK + MLA_KV_RANK
A_AB = A_KR + 2 * MLA_ROPE


def _swap_rope_halves(w):
    s = w.shape
    return w.reshape(s[:-1] + (2, 2, MLA_ROPE // 4))[..., ::-1, :].reshape(s)


def _w_in_layout(wi):
    return jnp.concatenate([wi[:, _OFF["gq"]:_OFF["ga"]], wi[:, _OFF["hq"]:_OFF["mcq"]], wi[:, _OFF["mcq"]:IN_DIM],
                            _swap_rope_halves(wi[:, _OFF["mkr"]:IN_DIM]), wi[:, _OFF["ga"]:_OFF["hq"]],
                            jnp.zeros((wi.shape[0], G_W + H_W + A_W - IN_DIM - MLA_ROPE), wi.dtype)], axis=1)


def _w_uq_layout(w):
    w = w.reshape(w.shape[0], HEADS, MLA_QK)
    rope = w[:, :, MLA_NOPE:]
    return jnp.concatenate([w[:, :, :MLA_NOPE], rope, _swap_rope_halves(rope)], axis=-1).reshape(w.shape[0], -1)


def _w_ukv_layout(w):
    w = w.reshape(w.shape[0], HEADS, MLA_NOPE + MLA_DV)
    return jnp.concatenate([w[:, :, :MLA_NOPE].reshape(w.shape[0], -1), w[:, :, MLA_NOPE:].reshape(w.shape[0], -1)],
                           axis=1)


def _seg_ones(n):
    i = np.arange(n) // HD
    return jnp.asarray((i[:, None] == i[None, :]).astype(np.float32), dtype=BF16)


def _rms(x, w):
    return x * lax.rsqrt(jnp.mean(x * x, axis=-1, keepdims=True) + EPS) * w


def _dot(a, b):
    return jnp.dot(a, b, preferred_element_type=F32)


def _dot_t(a, b):
    return lax.dot_general(a, b, (((1,), (1,)), ((), ())), preferred_element_type=F32)


def _split3(x):
    hi = x.astype(BF16)
    r1 = x - hi.astype(F32)
    mid = r1.astype(BF16)
    lo = (r1 - mid.astype(F32)).astype(BF16)
    return hi, mid, lo


def _split2(x):
    hi = x.astype(BF16)
    return hi, (x - hi.astype(F32)).astype(BF16)


def _seg_sum(x, seg):
    hi, lo = _split2(x)
    return _dot(hi, seg) + _dot(lo, seg)


def _mask_dot(mask01, x):
    hi, mid, lo = _split3(x)
    return _dot(mask01, hi) + _dot(mask01, mid) + _dot(mask01, lo)


def _softplus(x):
    return jnp.maximum(x, 0.0) + jnp.log1p(jnp.exp(-jnp.abs(x)))


def _sigmoid(x):
    return jax.nn.sigmoid(x)


def _silu(x):
    return x * jax.nn.sigmoid(x)


def _ada_kernel(c_ref, w_ref, b_ref, o_ref):
    s = _silu(c_ref[...]).astype(BF16)
    o_ref[0] = _dot(s, w_ref[0].astype(BF16)) + b_ref[0]


def _ada(cond, w_ada, b_ada):
    n = w_ada.shape[-1]
    tn = ADA_COL_TILE
    return pl.pallas_call(
        _ada_kernel,
        grid=(DEPTH, n // tn),
        in_specs=[pl.BlockSpec((MOD_ROWS, D_MODEL), lambda l, j: (0, 0)),
                  pl.BlockSpec((1, D_MODEL, tn), lambda l, j: (l, 0, j)),
                  pl.BlockSpec((1, 1, tn), lambda l, j: (l, 0, j))],
        out_specs=pl.BlockSpec((1, MOD_ROWS, tn), lambda l, j: (l, 0, j)),
        out_shape=jax.ShapeDtypeStruct((DEPTH, MOD_ROWS, n), F32),
        compiler_params=pltpu.CompilerParams(dimension_semantics=("arbitrary", "arbitrary"),
                                             vmem_limit_bytes=VMEM_LIMIT),
        name="ada",
    )(cond, w_ada, b_ada.reshape(DEPTH, 1, n))


def _premix_kernel(x_ref, g_ref, sc_ref, sh_ref, w_ref, og_ref, oh_ref, oa_ref, vt_ref):
    h = _rms(x_ref[...], g_ref[...]) * (1.0 + sc_ref[0]) + sh_ref[0]
    hb = h.astype(BF16)
    og_ref[...] = _dot(hb, w_ref[:, 0:G_W])
    oh = _dot(hb, w_ref[:, G_W:G_W + H_W])
    oh_ref[...] = oh
    oa_ref[...] = _dot(hb, w_ref[:, G_W + H_W:G_W + H_W + A_W])
    vt_ref[0] = oh[:, LIN_W:2 * LIN_W].T


def _premix(x, g, sc, sh, w, tm, t_len):
    m = x.shape[0]
    tiles = m // tm
    per = tiles // sc.shape[0]
    row = lambda i: (i // per, 0, 0)
    tps = t_len // tm
    seq = lambda i: (i // tps, 0, i % tps)
    wt = G_W + H_W + A_W
    return pl.pallas_call(
        _premix_kernel,
        grid=(tiles,),
        in_specs=[pl.BlockSpec((tm, D_MODEL), lambda i: (i, 0)),
                  pl.BlockSpec((1, D_MODEL), lambda i: (0, 0)),
                  pl.BlockSpec((1, 1, D_MODEL), row),
                  pl.BlockSpec((1, 1, D_MODEL), row),
                  pl.BlockSpec((D_MODEL, wt), lambda i: (0, 0))],
        out_specs=[pl.BlockSpec((tm, G_W), lambda i: (i, 0)),
                   pl.BlockSpec((tm, H_W), lambda i: (i, 0)),
                   pl.BlockSpec((tm, A_W), lambda i: (i, 0)),
                   pl.BlockSpec((1, LIN_W, tm), seq)],
        out_shape=[jax.ShapeDtypeStruct((m, G_W), F32),
                   jax.ShapeDtypeStruct((m, H_W), F32),
                   jax.ShapeDtypeStruct((m, A_W), F32),
                   jax.ShapeDtypeStruct((m // t_len, LIN_W, t_len), F32)],
        compiler_params=pltpu.CompilerParams(dimension_semantics=("arbitrary",), vmem_limit_bytes=VMEM_LIMIT),
        name="premix",
    )(x, g, sc, sh, w)


def _keys_from_latent(ckv, krot, wukv_ref, k_out, v_out):
    kv = _dot(ckv.astype(BF16), wukv_ref[...])
    for h in range(HEADS):
        k_out[:, MLA_HEAD_PAD * h:MLA_HEAD_PAD * h + MLA_NOPE] = kv[:, MLA_NOPE * h:MLA_NOPE * (h + 1)].astype(BF16)
        k_out[:, MLA_HEAD_PAD * h + MLA_NOPE:MLA_HEAD_PAD * (h + 1)] = krot
    v_out[...] = kv[:, HEADS * MLA_NOPE:].astype(BF16)


def _mla_prep_kernel(tps, past, a_ref, qnw_ref, kvnw_ref, wuq_ref, wukv_ref, qa_ref, qb_ref, ck_ref, sk_ref, *refs):
    if past:
        cache_ckv_ref, cache_kr_ref, q_out, k_out, v_out, ckv_out = refs
    else:
        q_out, k_out, v_out, ckv_out = refs
    j = pl.program_id(1)

    @pl.when(j < tps)
    def _():
        qn = _rms(a_ref[:, 0:MLA_Q_RANK], qnw_ref[...])
        y = _dot(qn.astype(BF16), wuq_ref[...])
        z = pltpu.roll(y, HEADS * MLA_HEAD_PAD - MLA_ROPE, 1)
        qa = qa_ref[...]
        qb = qb_ref[...]
        for h in range(HEADS):
            sl = slice(MLA_HEAD_PAD * h, MLA_HEAD_PAD * (h + 1))
            q_out[:, sl] = (y[:, sl] * qa + z[:, sl] * qb).astype(BF16)
        ckv = _rms(a_ref[:, MLA_Q_RANK:A_KR], kvnw_ref[...])
        ckv_out[...] = ckv
        kr = a_ref[:, A_KR:A_AB]
        krot = (kr * ck_ref[...] + pltpu.roll(kr, 64, 1) * sk_ref[...]).astype(BF16)
        _keys_from_latent(ckv, krot, wukv_ref, k_out.at[0], v_out.at[0])

    if past:
        @pl.when(j == tps)
        def _():
            _keys_from_latent(cache_ckv_ref[0], cache_kr_ref[0].astype(BF16), wukv_ref,
                              k_out.at[0, 0:past], v_out.at[0, 0:past])


def _mla_prep(oa, qnw, kvnw, wuq, wukv, qa, qb, ck, sk, tm, t_len, cache=None):
    m = oa.shape[0]
    b = m // t_len
    tps = t_len // tm
    past = 0 if cache is None else cache[0].shape[1]
    tok = lambda bi, j: (bi * tps + jnp.minimum(j, tps - 1), 0)
    pos = lambda bi, j: (jnp.minimum(j, tps - 1), 0)
    full = lambda bi, j: (0, 0)
    seq = lambda bi, j: (bi, j, 0)
    kw = HEADS * MLA_HEAD_PAD
    in_specs = [pl.BlockSpec((tm, A_W), tok),
                pl.BlockSpec((1, MLA_Q_RANK), full),
                pl.BlockSpec((1, MLA_KV_RANK), full),
                pl.BlockSpec((MLA_Q_RANK, kw), full),
                pl.BlockSpec((MLA_KV_RANK, HEADS * (MLA_NOPE + MLA_DV)), full),
                pl.BlockSpec((tm, MLA_HEAD_PAD), pos),
                pl.BlockSpec((tm, MLA_HEAD_PAD), pos),
                pl.BlockSpec((tm, 128), pos),
                pl.BlockSpec((tm, 128), pos)]
    operands = [oa, qnw, kvnw, wuq, wukv, qa, qb, ck, sk]
    if past:
        in_specs += [pl.BlockSpec((1, past, MLA_KV_RANK), lambda bi, j: (bi, 0, 0)),
                     pl.BlockSpec((1, past, 128), lambda bi, j: (bi, 0, 0))]
        operands += list(cache)
    return pl.pallas_call(
        functools.partial(_mla_prep_kernel, tps, past),
        grid=(b, tps + (1 if past else 0)),
        in_specs=in_specs,
        out_specs=[pl.BlockSpec((tm, kw), tok),
                   pl.BlockSpec((1, tm, kw), seq),
                   pl.BlockSpec((1, tm, HEADS * MLA_DV), seq),
                   pl.BlockSpec((tm, MLA_KV_RANK), tok)],
        out_shape=[jax.ShapeDtypeStruct((m, kw), BF16),
                   jax.ShapeDtypeStruct((b, t_len + past, kw), BF16),
                   jax.ShapeDtypeStruct((b, t_len + past, HEADS * MLA_DV), BF16),
                   jax.ShapeDtypeStruct((m, MLA_KV_RANK), F32)],
        compiler_params=pltpu.CompilerParams(dimension_semantics=("arbitrary", "arbitrary"),
                                             vmem_limit_bytes=VMEM_LIMIT),
        name="mla_prep",
    )(*operands)


def _attn_kernel(q_ref, k_ref, v_ref, o_ref):
    for h in range(HEADS):
        sl = slice(MLA_HEAD_PAD * h, MLA_HEAD_PAD * (h + 1))
        s = _dot_t(q_ref[0, :, sl], k_ref[0, :, sl])
        p = jnp.exp2(s - jnp.max(s, axis=-1, keepdims=True))
        l = jnp.sum(p, axis=-1, keepdims=True)
        o = _dot(p.astype(BF16), v_ref[0, :, MLA_DV * h:MLA_DV * (h + 1)])
        o_ref[0, :, MLA_DV * h:MLA_DV * (h + 1)] = o / l


def _attn(q, k, v, tq):
    b, t, kw = q.shape
    s = k.shape[1]
    return pl.pallas_call(
        _attn_kernel,
        grid=(b, t // tq),
        in_specs=[pl.BlockSpec((1, tq, kw), lambda bi, i: (bi, i, 0)),
                  pl.BlockSpec((1, s, kw), lambda bi, i: (bi, 0, 0)),
                  pl.BlockSpec((1, s, HEADS * MLA_DV), lambda bi, i: (bi, 0, 0))],
        out_specs=pl.BlockSpec((1, tq, HEADS * MLA_DV), lambda bi, i: (bi, i, 0)),
        out_shape=jax.ShapeDtypeStruct((b, t, HEADS * MLA_DV), F32),
        compiler_params=pltpu.CompilerParams(dimension_semantics=("arbitrary", "arbitrary"),
                                             vmem_limit_bytes=VMEM_LIMIT),
        name="attn",
    )(q, k, v)


def _gdn_prep_kernel(tps, cur_ref, prev_ref, next_ref, w_ref, seg_ref, q_out, k_out, v_out, kt_out):
    i = pl.program_id(0)
    tm = cur_ref.shape[0]
    has_prev = ((i % tps) != 0).astype(F32)
    has_next = ((i % tps) != (tps - 1)).astype(F32)
    xc = jnp.concatenate([prev_ref[...] * has_prev, cur_ref[...], next_ref[...] * has_next], axis=0)
    n = tm + 16
    y = xc[8:8 + tm] * w_ref[2:3, :]
    for j in (0, 1, 3, 4):
        d = j - 2
        y = y + pltpu.roll(xc, (n - d) % n, 0)[8:8 + tm] * w_ref[j:j + 1, :]
    y = _silu(y)
    seg = seg_ref[...]
    qk = y[:, 0:2 * LIN_W]
    qk = qk * lax.rsqrt(_seg_sum(qk * qk, seg) + EPS)
    q_out[...] = qk[:, 0:LIN_W] * (HD ** -0.5)
    k_out[...] = qk[:, LIN_W:2 * LIN_W]
    kt_out[0] = qk[:, LIN_W:2 * LIN_W].T
    v_out[...] = y[:, 2 * LIN_W:]


def _gdn_prep(og, convw, seg, tm, t_len):
    m = og.shape[0]
    tiles = m // tm
    tps = t_len // tm
    r8 = tm // 8
    last8 = m // 8 - 1
    return pl.pallas_call(
        functools.partial(_gdn_prep_kernel, tps),
        grid=(tiles,),
        in_specs=[pl.BlockSpec((tm, G_CONV), lambda i: (i, 0)),
                  pl.BlockSpec((8, G_CONV), lambda i: (jnp.maximum(i * r8 - 1, 0), 0)),
                  pl.BlockSpec((8, G_CONV), lambda i: (jnp.minimum((i + 1) * r8, last8), 0)),
                  pl.BlockSpec((8, G_CONV), lambda i: (0, 0)),
                  pl.BlockSpec((2 * LIN_W, 2 * LIN_W), lambda i: (0, 0))],
        out_specs=[pl.BlockSpec((tm, LIN_W), lambda i: (i, 0)),
                   pl.BlockSpec((tm, LIN_W), lambda i: (i, 0)),
                   pl.BlockSpec((tm, LIN_W), lambda i: (i, 0)),
                   pl.BlockSpec((1, LIN_W, tm), lambda i: (i // tps, 0, i % tps))],
        out_shape=[jax.ShapeDtypeStruct((m, LIN_W), F32),
                   jax.ShapeDtypeStruct((m, LIN_W), F32),
                   jax.ShapeDtypeStruct((m, LIN_W), F32),
                   jax.ShapeDtypeStruct((m // t_len, LIN_W, t_len), F32)],
        compiler_params=pltpu.CompilerParams(dimension_semantics=("arbitrary",), vmem_limit_bytes=VMEM_LIMIT),
        name="gdn_prep",
    )(og, og, og, convw, seg)


def _cumsum_masks():
    r = np.arange(SCAN_BLOCK)[:, None]
    c = np.arange(SCAN_BLOCK)[None, :]
    masks = []
    for chunk in (GDN_CHUNK, HG_CHUNK):
        same = (r // chunk) == (c // chunk)
        masks += [same & (r >= c), same & (r <= c)]
    return jnp.asarray(np.stack(masks).astype(np.float32), dtype=BF16)


def _gdn_units(d, incl_blk, q_ref, k_ref, v_ref, kt_ref, ab_ref, gc_ref, o_ref):
    c = GDN_CHUNK
    ri = lax.broadcasted_iota(jnp.int32, (c, c), 0)
    ci = lax.broadcasted_iota(jnp.int32, (c, c), 1)
    incl = (ri >= ci) if d == 0 else (ri <= ci)
    strict = (ri > ci) if d == 0 else (ri < ci)
    last = c - 1 if d == 0 else 0
    ab = ab_ref[0]
    g_cols = gc_ref[0:1, :] * _softplus(ab + gc_ref[1:2, :])
    beta = _sigmoid(ab)
    gcs_blk = _mask_dot(incl_blk, g_cols)
    gcr_blk = gcs_blk.T
    n_chunks = SCAN_BLOCK // c
    units = []
    for rank, cc in enumerate(range(n_chunks) if d == 0 else range(n_chunks - 1, -1, -1)):
        rows = slice(cc * c, (cc + 1) * c)
        for h in range(HEADS):
            j = d * HEADS + h
            hs = slice(HD * h, HD * (h + 1))
            gcol = gcs_blk[rows, j:j + 1]
            grow = gcr_blk[j:j + 1, rows]
            bcol = beta[rows, 8 + j:9 + j]
            gl = gcol[last:last + 1, :]
            eg = jnp.exp(gcol)
            k = k_ref[0, rows, hs]
            units.append(dict(
                rank=rank, j=j, rows=rows, hs=hs, o_ref=o_ref, strict=strict, gl=gl, bcol=bcol, eg=eg,
                q=q_ref[0, rows, hs], k=k,
                k_tail=(kt_ref[0, hs, rows] * jnp.exp(gl - grow)).astype(BF16),
                decay=jnp.where(incl, jnp.exp(jnp.where(incl, gcol - grow, 0.0)), 0.0),
                x=jnp.concatenate([v_ref[0, rows, hs] * bcol, k * (bcol * eg)], axis=1)))
    return units


def _gdn_stages(units, s_ref):
    c = GDN_CHUNK
    ri = lax.broadcasted_iota(jnp.int32, (c, c), 0)
    ci = lax.broadcasted_iota(jnp.int32, (c, c), 1)
    diag_blk = jnp.right_shift(ri, 4) == jnp.right_shift(ci, 4)
    eye = (ri == ci).astype(F32)

    def gram():
        for u in units:
            kb16 = u["k"].astype(BF16)
            u["kk"] = _dot_t((u["k"] * u["bcol"]).astype(BF16), kb16)
            u["qk"] = _dot_t(u["q"].astype(BF16), kb16)

    def split():
        for u in units:
            p = jnp.where(u["strict"], u["kk"] * u["decay"], 0.0)
            pd = jnp.where(diag_blk, p, 0.0)
            u["att"] = (u["qk"] * u["decay"]).astype(BF16)
            u["poff"] = (p - pd).astype(BF16)
            u["td"] = eye - pd
            u["a"] = pd.astype(BF16)

    def square():
        for u in units:
            u["a"] = _dot(u["a"], u["a"]).astype(BF16)

    def extend():
        for u in units:
            u["td"] = u["td"] + _dot(u["td"].astype(BF16), u["a"])

    def apply_diag():
        for u in units:
            yn = _dot(u["td"].astype(BF16), jnp.concatenate([u["x"].astype(BF16), u["poff"]], axis=1))
            u["y"] = yn[:, 0:2 * HD]
            u["n"] = yn[:, 2 * HD:3 * HD].astype(BF16)
            u["x"] = u["y"]

    def substitute():
        for u in units:
            u["x"] = u["y"] - _dot(u["n"], u["x"].astype(BF16))

    def new_values(rank):
        for u in units:
            if u["rank"] == rank:
                u["s"] = s_ref[0, u["j"]]
                u["sb"] = u["s"].astype(BF16)
                u["vb"] = (u["x"][:, 0:HD] - _dot(u["x"][:, HD:2 * HD].astype(BF16), u["sb"])).astype(BF16)

    def emit(rank):
        for u in units:
            if u["rank"] == rank:
                u["o_ref"][0, u["rows"], u["hs"]] = (_dot((u["q"] * u["eg"]).astype(BF16), u["sb"])
                                                     + _dot(u["att"], u["vb"]))
                s_ref[0, u["j"]] = u["s"] * jnp.exp(u["gl"]) + _dot(u["k_tail"], u["vb"])

    stages = [gram, split] + [square, extend] * 3 + [apply_diag] + [substitute] * 3
    for rank in range(SCAN_BLOCK // c):
        stages += [functools.partial(new_values, rank), functools.partial(emit, rank)]
    return stages


def _hgrn_lower_bound(layer, lb_ref):
    raw = lb_ref[0:DEPTH, :]
    e = jnp.exp(raw - jnp.max(raw, axis=0, keepdims=True))
    gamma = e / jnp.sum(e, axis=0, keepdims=True)
    lb = jnp.zeros((1, LIN_W), F32)
    for i in range(1, layer + 1):
        lb = lb + gamma[i:i + 1, :]
    return lb


def _hgrn_block(d, incl01, lb, q_ref, v_ref, f_ref):
    c = HG_CHUNK
    oml = 1.0 - lb
    q = q_ref[0]
    f = f_ref[0]
    logf = jnp.log(jnp.maximum(lb + oml * _sigmoid(f), GATE_FLOOR))
    bc = _mask_dot(incl01, logf)
    bc2 = bc * LOG2E
    src2 = bc2 - (jnp.log2(oml) - _softplus(f) * LOG2E)
    return dict(d=d, q=q, v=v_ref[0], kk=oml * _sigmoid(-f), bc=bc, bc2=bc2, src2=src2,
                qe=(q * jnp.exp(bc)).astype(BF16), last=c - 1 if d == 0 else 0)


def _hgrn_live_halves(d, s):
    half = HG_CHUNK // 2
    if d == 0:
        return (1,) if s >= half else (0, 1)
    return (0,) if s < half else (0, 1)


def _hgrn_chunk_build(blk, cc):
    c = HG_CHUNK
    half = c // 2
    d = blk["d"]
    base = cc * c
    rows = slice(base, base + c)
    kk, bc = blk["kk"][rows], blk["bc"][rows]
    bl = bc[blk["last"]:blk["last"] + 1, :]
    row_half = lax.broadcasted_iota(jnp.int32, (half, LIN_W), 0)
    zero_half = jnp.zeros((half, LIN_W), F32)
    parts = []
    for s in range(c):
        bs = blk["src2"][base + s:base + s + 1]
        for hf in (0, 1):
            if hf not in _hgrn_live_halves(d, s):
                parts.append(zero_half)
                continue
            tr = slice(base + hf * half, base + (hf + 1) * half)
            col = blk["q"][tr] * jnp.exp2(blk["bc2"][tr] - bs)
            if hf == s // half:
                rid = row_half + hf * half
                col = jnp.where((rid >= s) if d == 0 else (rid <= s), col, 0.0)
            parts.append(col)
    return dict(cols=jnp.concatenate(parts, axis=0).astype(BF16), k_tail=(kk * jnp.exp(bl - bc)).astype(BF16),
                decay=jnp.exp(bl))


def _hgrn_chunk_launch(blk, cc, built, vt_ref, ones, st):
    rows = slice(cc * HG_CHUNK, (cc + 1) * HG_CHUNK)
    return dict(
        r=_dot(built["cols"], ones),
        o_inter=_dot_t(blk["qe"][rows], st.astype(BF16)),
        upd=_dot(vt_ref[0, :, rows].astype(BF16), built["k_tail"]),
        decay=built["decay"])


def _hgrn_chunk_finish(blk, cc, pend, seg_mask, st, o_ref):
    c = HG_CHUNK
    half = c // 2
    base = cc * c
    r = pend["r"]
    o_half = [pend["o_inter"][0:half], pend["o_inter"][half:c]]
    for s in range(c):
        vs = blk["v"][base + s:base + s + 1]
        for hf in _hgrn_live_halves(blk["d"], s):
            o_half[hf] = o_half[hf] + r[s * c + hf * half:s * c + (hf + 1) * half, :] * vs
    o_ref[0, base:base + half, :] = o_half[0]
    o_ref[0, base + half:base + c, :] = o_half[1]
    return st * pend["decay"] + pend["upd"] * seg_mask


def _hgrn_steps(lb, masks_ref, fwd_refs, bwd_refs, ones, st_ref):
    seg_mask = ones.astype(F32)
    (qf, vf, ff, vtf, of_ref), (qb, vb, fb, vtb, ob_ref) = fwd_refs, bwd_refs
    fwd = _hgrn_block(0, masks_ref[2], lb, qf, vf, ff)
    bwd = _hgrn_block(1, masks_ref[3], lb, qb, vb, fb)
    carry = {0: st_ref[0, 0], 1: st_ref[0, 1]}
    built, pending = {}, {}
    n_chunks = SCAN_BLOCK // HG_CHUNK
    chunk_of = {0: lambda step: step, 1: lambda step: n_chunks - 1 - step}
    side = {0: (fwd, vtf, of_ref), 1: (bwd, vtb, ob_ref)}

    def build(d, step):
        built[(d, step)] = _hgrn_chunk_build(side[d][0], chunk_of[d](step))

    def launch(d, step):
        blk, vt_ref, _ = side[d]
        pending[(d, step)] = _hgrn_chunk_launch(blk, chunk_of[d](step), built.pop((d, step)), vt_ref, ones, carry[d])

    def finish(d, step):
        blk, _, o_ref = side[d]
        carry[d] = _hgrn_chunk_finish(blk, chunk_of[d](step), pending.pop((d, step)), seg_mask, carry[d], o_ref)

    def flush():
        st_ref[0, 0] = carry[0]
        st_ref[0, 1] = carry[1]

    groups = []
    for slot in range(n_chunks + 2):
        group = []
        for phase, step in ((finish, slot - 2), (launch, slot - 1), (build, slot)):
            if 0 <= step < n_chunks:
                group += [functools.partial(phase, 0, step), functools.partial(phase, 1, step)]
        groups.append(group)
    return groups, flush


def _scan_kernel(layer, masks_ref,
                 gqf, gkf, gvkf, gktf, gabf, gqb, gkb, gvkb, gktb, gabb, gc_ref, gs0_ref,
                 hqf, hvf, hff, hvtf, hqb, hvb, hfb, hvtb, lb_ref, ones_ref, hs0_ref,
                 gof_ref, gob_ref, gs_ref, hof_ref, hob_ref, hst_ref):
    @pl.when(pl.program_id(1) == 0)
    def _():
        gs_ref[...] = gs0_ref[...]
        hst_ref[...] = hs0_ref[...]

    units = (_gdn_units(0, masks_ref[0], gqf, gkf, gvkf, gktf, gabf, gc_ref, gof_ref)
             + _gdn_units(1, masks_ref[1], gqb, gkb, gvkb, gktb, gabb, gc_ref, gob_ref))
    stages = _gdn_stages(units, gs_ref)
    groups, flush = _hgrn_steps(_hgrn_lower_bound(layer, lb_ref), masks_ref, (hqf, hvf, hff, hvtf, hof_ref),
                                (hqb, hvb, hfb, hvtb, hob_ref), ones_ref[...], hst_ref)
    per_stage = -(-len(groups) // len(stages))
    for stage in stages:
        stage()
        for group in groups[:per_stage]:
            for step in group:
                step()
        groups = groups[per_stage:]
    for group in groups:
        for step in group:
            step()
    flush()


def _lin_scan(layer, gq, gk, gvk, gkt, oa, gconst, gs0, oh, vt, lb, ones, hs0):
    b, t, _ = gq.shape
    blk = SCAN_BLOCK
    nb = t // blk

    def tok(cb, rev):
        return (lambda bi, i: (bi, nb - 1 - i, cb)) if rev else (lambda bi, i: (bi, i, cb))

    def lanes(rev):
        return (lambda bi, i: (bi, 0, nb - 1 - i)) if rev else (lambda bi, i: (bi, 0, i))

    def gdn_specs(rev):
        return [pl.BlockSpec((1, blk, LIN_W), tok(0, rev)),
                pl.BlockSpec((1, blk, LIN_W), tok(0, rev)),
                pl.BlockSpec((1, blk, LIN_W), tok(0, rev)),
                pl.BlockSpec((1, LIN_W, blk), lanes(rev)),
                pl.BlockSpec((1, blk, 128), tok(A_AB // 128, rev))]

    def hgrn_specs(rev):
        f_col = 3 if rev else 2
        return [pl.BlockSpec((1, blk, LIN_W), tok(0, rev)),
                pl.BlockSpec((1, blk, LIN_W), tok(1, rev)),
                pl.BlockSpec((1, blk, LIN_W), tok(f_col, rev)),
                pl.BlockSpec((1, LIN_W, blk), lanes(rev))]

    const = lambda r, c: pl.BlockSpec((r, c), lambda bi, i: (0, 0))
    g_state = pl.BlockSpec((1, 2 * HEADS, HD, HD), lambda bi, i: (bi, 0, 0, 0))
    h_state = pl.BlockSpec((1, 2, LIN_W, LIN_W), lambda bi, i: (bi, 0, 0, 0))
    out_tok = lambda rev: pl.BlockSpec((1, blk, LIN_W), tok(0, rev))
    seq = jax.ShapeDtypeStruct((b, t, LIN_W), F32)
    return pl.pallas_call(
        functools.partial(_scan_kernel, layer),
        grid=(b, nb),
        in_specs=[pl.BlockSpec((4, blk, blk), lambda bi, i: (0, 0, 0))]
        + gdn_specs(False) + gdn_specs(True) + [const(8, 128), g_state]
        + hgrn_specs(False) + hgrn_specs(True) + [const(8, LIN_W), const(LIN_W, LIN_W), h_state],
        out_specs=[out_tok(False), out_tok(True), g_state, out_tok(False), out_tok(True), h_state],
        out_shape=[seq, seq, jax.ShapeDtypeStruct((b, 2 * HEADS, HD, HD), F32),
                   seq, seq, jax.ShapeDtypeStruct((b, 2, LIN_W, LIN_W), F32)],
        compiler_params=pltpu.CompilerParams(dimension_semantics=("arbitrary", "arbitrary"),
                                             vmem_limit_bytes=VMEM_LIMIT),
        name="lin_scan",
    )(_cumsum_masks(), gq, gk, gvk, gkt, oa, gq, gk, gvk, gkt, oa, gconst, gs0,
      oh, oh, oh, vt, oh, oh, oh, vt, lb, ones, hs0)


def _post_kernel(x_ref, ogf_ref, ogb_ref, z_ref, gnw_ref, ohf_ref, ohb_ref, hg_ref, hnw_ref, om_ref, seg_ref,
                 wout_ref, gpm_ref, gt1_ref, gpf_ref, sc2_ref, sh2_ref, gt2_ref, gff_ref,
                 wf_ref, wo_ref, out_ref):
    seg = seg_ref[...]
    inv = 1.0 / HD
    og = ogf_ref[...] + ogb_ref[...]
    og = og * lax.rsqrt(_seg_sum(og * og, seg) * inv + EPS) * gnw_ref[...] * _silu(z_ref[...])
    oh = ohf_ref[...] + ohb_ref[...]
    oh = oh * lax.rsqrt(_seg_sum(oh * oh, seg) * inv + EPS) * hnw_ref[...] * _sigmoid(hg_ref[...])
    mix = jnp.concatenate([og.astype(BF16), oh.astype(BF16), om_ref[...].astype(BF16)], axis=-1)
    x1 = x_ref[...] + gt1_ref[0] * _rms(_dot(mix, wout_ref[...]), gpm_ref[...])
    hb = (_rms(x1, gpf_ref[...]) * (1.0 + sc2_ref[0]) + sh2_ref[0]).astype(BF16)
    fw = D_FF // FF_SPLIT
    y = None
    for part in range(FF_SPLIT):
        cs = slice(part * fw, (part + 1) * fw)
        gs = slice(D_FF + part * fw, D_FF + (part + 1) * fw)
        act = (_silu(_dot(hb, wf_ref[:, cs])) * _dot(hb, wf_ref[:, gs])).astype(BF16)
        contrib = _dot(act, wo_ref[cs, :])
        y = contrib if y is None else y + contrib
    out_ref[...] = x1 + gt2_ref[0] * _rms(y, gff_ref[...])


def _post(x, ogf, ogb, og, gnw, ohf, ohb, oh, hnw, om, seg, wout, gpm, gt1, gpf, sc2, sh2, gt2, gff, wf, wo, tm):
    m = x.shape[0]
    tiles = m // tm
    per = tiles // gt1.shape[0]
    row = lambda i: (i // per, 0, 0)
    tok = lambda w: pl.BlockSpec((tm, w), lambda i: (i, 0))
    vec = lambda w: pl.BlockSpec((1, w), lambda i: (0, 0))
    mod = pl.BlockSpec((1, 1, D_MODEL), row)
    once = lambda r, c: pl.BlockSpec((r, c), lambda i: (0, 0), pipeline_mode=pl.Buffered(1))
    return pl.pallas_call(
        _post_kernel,
        grid=(tiles,),
        in_specs=[tok(D_MODEL), tok(LIN_W), tok(LIN_W),
                  pl.BlockSpec((tm, LIN_W), lambda i: (i, G_W // LIN_W - 1)), vec(LIN_W),
                  tok(LIN_W), tok(LIN_W),
                  pl.BlockSpec((tm, LIN_W), lambda i: (i, H_W // LIN_W - 1)), vec(LIN_W),
                  tok(HEADS * MLA_DV), once(LIN_W, LIN_W),
                  once(D_MODEL, D_MODEL), vec(D_MODEL), mod, vec(D_MODEL), mod, mod, mod, vec(D_MODEL),
                  once(D_MODEL, 2 * D_FF), once(D_FF, D_MODEL)],
        out_specs=tok(D_MODEL),
        out_shape=jax.ShapeDtypeStruct((m, D_MODEL), F32),
        compiler_params=pltpu.CompilerParams(dimension_semantics=("arbitrary",), vmem_limit_bytes=VMEM_LIMIT),
        name="post",
    )(x, ogf, ogb, og, gnw, ohf, ohb, oh, hnw, om, seg, wout, gpm, gt1, gpf, sc2, sh2, gt2, gff, wf, wo)


def _rope_tables(t_len, use_rope):
    scale = MLA_QK ** -0.5 * LOG2E
    if use_rope:
        rows = t_len // GRID_W
        row = np.repeat(np.arange(rows, dtype=np.float64), GRID_W)
        col = np.tile(np.arange(GRID_W, dtype=np.float64), rows)
        nf = MLA_ROPE // 4
        inv = ROPE_BASE ** (-np.arange(nf, dtype=np.float64) / nf)
        ar, ac = row[:, None] * inv, col[:, None] * inv
        cos = np.concatenate([np.cos(ar), np.cos(ar), np.cos(ac), np.cos(ac)], axis=-1)
        sin = np.concatenate([-np.sin(ar), np.sin(ar), -np.sin(ac), np.sin(ac)], axis=-1)
    else:
        cos = np.ones((t_len, MLA_ROPE))
        sin = np.zeros((t_len, MLA_ROPE))
    z64 = np.zeros((t_len, MLA_ROPE))
    qa = scale * np.concatenate([np.ones((t_len, MLA_NOPE)), cos, z64], axis=-1)
    qb = scale * np.concatenate([np.zeros((t_len, MLA_NOPE)), sin, z64], axis=-1)
    ck = np.concatenate([cos, z64], axis=-1)
    sk = np.concatenate([sin, z64], axis=-1)
    return tuple(jnp.asarray(a, dtype=F32) for a in (qa, qb, ck, sk))


def _layer_weights(l, w_in, w_out, gdn_conv_w, gdn_a_log, gdn_dt_bias, gdn_norm_w, lower_bounds, hgrn_norm_w,
                   mla_q_norm_w, mla_w_uq, mla_kv_norm_w, mla_w_ukv, w_ffn_in, w_ffn_out,
                   g_pre_mix, g_post_mix, g_pre_ffn, g_post_ffn):
    w_cat = _w_in_layout(w_in[l].astype(BF16))
    convw = jnp.concatenate([gdn_conv_w[l].T, jnp.zeros((3, G_CONV), F32)], axis=0)
    neg_a = -jnp.exp(gdn_a_log[l].astype(F32)).reshape(1, 8)
    dt = gdn_dt_bias[l].astype(F32).reshape(1, 8)
    pad = lambda r: jnp.pad(r, ((0, 0), (0, 128 - r.shape[1])))
    gconst = jnp.concatenate([pad(neg_a), pad(dt), jnp.zeros((6, 128), F32)], axis=0)
    vec = lambda a: a.reshape(1, -1).astype(F32)
    return dict(
        w_cat=w_cat, convw=convw, gconst=gconst,
        gnw=vec(jnp.tile(gdn_norm_w[l], HEADS)), hnw=vec(jnp.tile(hgrn_norm_w[l], HEADS)),
        layer=l, lb=lower_bounds,
        qnw=vec(mla_q_norm_w[l]), kvnw=vec(mla_kv_norm_w[l]),
        wuq=_w_uq_layout(mla_w_uq[l].astype(BF16)), wukv=_w_ukv_layout(mla_w_ukv[l].astype(BF16)),
        wout=w_out[l].astype(BF16), wf=w_ffn_in[l].astype(BF16),
        wo=w_ffn_out[l].astype(BF16),
        g_pre_mix=vec(g_pre_mix[l]), g_post_mix=vec(g_post_mix[l]),
        g_pre_ffn=vec(g_pre_ffn[l]), g_post_ffn=vec(g_post_ffn[l]))


def _trunk_layer(x, mod, w, gdn_s0, hg_s0, tables, ctx_cache, seg512, seg256):
    b, t, _ = x.shape
    m = b * t
    tm = min(TOKEN_TILE, t)
    sh1, sc1, gt1, sh2, sc2, gt2 = mod
    xf = x.reshape(m, D_MODEL)
    og, oh, oa, vt = _premix(xf, w["g_pre_mix"], sc1, sh1, w["w_cat"], tm, t)

    qa, qb, ck, sk = tables
    qcat, kcat, vcat, ckv = _mla_prep(oa, w["qnw"], w["kvnw"], w["wuq"], w["wukv"], qa, qb, ck, sk, tm, t, ctx_cache)
    o_mla = _attn(qcat.reshape(b, t, HEADS * MLA_HEAD_PAD), kcat, vcat, min(ATTN_Q_TILE, t)).reshape(m, HEADS * MLA_DV)

    gq, gk, gvk, gkt = _gdn_prep(og, w["convw"], seg512, tm, t)
    ogf, ogb, gdn_state, ohf, ohb, hg_state = _lin_scan(
        w["layer"], gq.reshape(b, t, LIN_W), gk.reshape(b, t, LIN_W), gvk.reshape(b, t, LIN_W), gkt,
        oa.reshape(b, t, A_W), w["gconst"], gdn_s0,
        oh.reshape(b, t, H_W), vt, w["lb"], seg256, hg_s0)

    x_new = _post(xf, ogf.reshape(m, LIN_W), ogb.reshape(m, LIN_W), og, w["gnw"],
                  ohf.reshape(m, LIN_W), ohb.reshape(m, LIN_W), oh, w["hnw"], o_mla, seg256,
                  w["wout"], w["g_post_mix"], gt1, w["g_pre_ffn"], sc2, sh2, gt2, w["g_post_ffn"],
                  w["wf"], w["wo"], tm)
    mkr = oa[:, A_KR:A_KR + MLA_ROPE]
    return x_new.reshape(b, t, D_MODEL), gdn_state, hg_state, ckv.reshape(b, t, MLA_KV_RANK), mkr.reshape(b, t, MLA_ROPE)


def _hg_state_to_block(s):
    st = jnp.swapaxes(s, -1, -2)
    eye = jnp.eye(HEADS, dtype=s.dtype)
    big = st[:, :, :, :, None, :] * eye[None, None, :, None, :, None]
    return big.reshape(s.shape[0], 2, LIN_W, LIN_W)


def _hg_block_to_state(big):
    b = big.shape[0]
    r = big.reshape(b, 2, HEADS, HD, HEADS, HD)
    diag = jnp.stack([r[:, :, h, :, h, :] for h in range(HEADS)], axis=2)
    return jnp.swapaxes(diag, -1, -2)


def kernel(x_prompt, x_sample, cache_mla_ckv, cache_mla_krope, state_gdn, state_hgrn, c, c_ctx, w_ada, b_ada,
           g_pre_mix, g_post_mix, g_pre_ffn, g_post_ffn, w_in, w_out, gdn_conv_w, gdn_a_log, gdn_dt_bias,
           gdn_norm_w, hgrn_lb, hgrn_norm_w, mla_q_norm_w, mla_w_uq, mla_kv_norm_w, mla_w_ukv, w_ffn_in, w_ffn_out):
    bp, tp, _ = x_prompt.shape
    bd, td, _ = x_sample.shape
    past = cache_mla_ckv.shape[2]
    assert 1 + bd <= MOD_ROWS and tp % SCAN_BLOCK == 0 and td % SCAN_BLOCK == 0

    lower_bounds = jnp.pad(hgrn_lb.astype(F32), ((0, 8 - DEPTH), (0, 0)))

    cond = jnp.concatenate([c_ctx[None, :], c, jnp.zeros((MOD_ROWS - 1 - bd, D_MODEL), F32)], axis=0)
    mod_all = _ada(cond, w_ada, b_ada)

    seg512 = _seg_ones(2 * LIN_W)
    seg256 = _seg_ones(LIN_W)
    tab_ctx = _rope_tables(tp, False)
    tab_lat = _rope_tables(td, True)

    weights = [_layer_weights(l, w_in, w_out, gdn_conv_w, gdn_a_log, gdn_dt_bias, gdn_norm_w, lower_bounds,
                              hgrn_norm_w, mla_q_norm_w, mla_w_uq, mla_kv_norm_w, mla_w_ukv, w_ffn_in, w_ffn_out,
                              g_pre_mix, g_post_mix, g_pre_ffn, g_post_ffn) for l in range(DEPTH)]

    def mods(l, lo, n):
        rows = mod_all[l, lo:lo + n].reshape(n, 1, 6 * D_MODEL)
        return tuple(rows[:, :, i * D_MODEL:(i + 1) * D_MODEL] for i in range(6))

    xp = x_prompt
    zero_g = jnp.zeros((bp, 2 * HEADS, HD, HD), F32)
    zero_h = jnp.zeros((bp, 2, LIN_W, LIN_W), F32)
    ckv_l, kr_l, gs_l, hs_l = [], [], [], []
    for l in range(DEPTH):
        xp, gs, hs, ckv, mkr = _trunk_layer(xp, mods(l, 0, 1), weights[l], zero_g, zero_h, tab_ctx, None,
                                            seg512, seg256)
        ckv_l.append(ckv)
        kr_l.append(mkr)
        gs_l.append(gs.reshape(bp, 2, HEADS, HD, HD))
        hs_l.append(_hg_block_to_state(hs))

    xs = x_sample
    for l in range(DEPTH):
        ctx_cache = (cache_mla_ckv[:, l], jnp.pad(cache_mla_krope[:, l], ((0, 0), (0, 0), (0, 128 - MLA_ROPE))))
        xs, _, _, _, _ = _trunk_layer(xs, mods(l, 1, bd), weights[l],
                                      state_gdn[:, l].reshape(bd, 2 * HEADS, HD, HD),
                                      _hg_state_to_block(state_hgrn[:, l]), tab_lat, ctx_cache, seg512, seg256)

    return (xp, xs, jnp.stack(ckv_l, axis=1), jnp.stack(kr_l, axis=1),
            jnp.stack(gs_l, axis=1), jnp.stack(hs_l, axis=1))
```

```python
import functools

import numpy as np
import jax
import jax.numpy as jnp
from jax import lax
from jax.experimental import pallas as pl
from jax.experimental.pallas import tpu as pltpu

F32 = jnp.float32
BF16 = jnp.bfloat16

D_MODEL = 1024
DEPTH = 2
GRID_W = 64
EPS = 1e-6
GATE_FLOOR = 1e-30
HEADS = 4
HD = 64
LIN_W = HEADS * HD
GDN_CHUNK = 64
HG_CHUNK = 16
SCAN_BLOCK = 256
LOG2E = 1.4426950408889634
MLA_Q_RANK = 384
MLA_KV_RANK = 256
MLA_NOPE = 128
MLA_ROPE = 64
MLA_DV = 128
MLA_QK = MLA_NOPE + MLA_ROPE
MLA_HEAD_PAD = 256
ROPE_BASE = 10000.0
D_FF = -(-8 * D_MODEL // (3 * 256)) * 256
FF_SPLIT = 11
TOKEN_TILE = 512
ATTN_Q_TILE = 256
ADA_COL_TILE = 1536
MOD_ROWS = 16
VMEM_LIMIT = 56 * 1024 * 1024

_OFF = {}
_o = 0
for _n, _s in (("gq", 256), ("gk", 256), ("gv", 256), ("gz", 256), ("ga", 8), ("gb", 8),
               ("hq", 256), ("hi", 256), ("hf", 512), ("hg", 256),
               ("mcq", MLA_Q_RANK), ("mckv", MLA_KV_RANK), ("mkr", MLA_ROPE)):
    _OFF[_n] = _o
    _o += _s
IN_DIM = _o

G_W = 1024
G_CONV = 768
H_W = 1280
A_W = 896
A_KR = MLA_Q_RANK + MLA_KV_RANK
A_AB = A_KR + 2 * MLA_ROPE


def _swap_rope_halves(w):
    s = w.shape
    return w.reshape(s[:-1] + (2, 2, MLA_ROPE // 4))[..., ::-1, :].reshape(s)


def _w_in_layout(wi):
    return jnp.concatenate([wi[:, _OFF["gq"]:_OFF["ga"]], wi[:, _OFF["hq"]:_OFF["mcq"]], wi[:, _OFF["mcq"]:IN_DIM],
                            _swap_rope_halves(wi[:, _OFF["mkr"]:IN_DIM]), wi[:, _OFF["ga"]:_OFF["hq"]],
                            jnp.zeros((wi.shape[0], G_W + H_W + A_W - IN_DIM - MLA_ROPE), wi.dtype)], axis=1)


def _w_uq_layout(w):
    w = w.reshape(w.shape[0], HEADS, MLA_QK)
    rope = w[:, :, MLA_NOPE:]
    return jnp.concatenate([w[:, :, :MLA_NOPE], rope, _swap_rope_halves(rope)], axis=-1).reshape(w.shape[0], -1)


def _w_ukv_layout(w):
    w = w.reshape(w.shape[0], HEADS, MLA_NOPE + MLA_DV)
    return jnp.concatenate([w[:, :, :MLA_NOPE].reshape(w.shape[0], -1), w[:, :, MLA_NOPE:].reshape(w.shape[0], -1)],
                           axis=1)


def _seg_ones(n):
    i = np.arange(n) // HD
    return jnp.asarray((i[:, None] == i[None, :]).astype(np.float32), dtype=BF16)


def _rms(x, w):
    return x * lax.rsqrt(jnp.mean(x * x, axis=-1, keepdims=True) + EPS) * w


def _dot(a, b):
    return jnp.dot(a, b, preferred_element_type=F32)


def _dot_t(a, b):
    return lax.dot_general(a, b, (((1,), (1,)), ((), ())), preferred_element_type=F32)


def _split3(x):
    hi = x.astype(BF16)
    r1 = x - hi.astype(F32)
    mid = r1.astype(BF16)
    lo = (r1 - mid.astype(F32)).astype(BF16)
    return hi, mid, lo


def _split2(x):
    hi = x.astype(BF16)
    return hi, (x - hi.astype(F32)).astype(BF16)


def _seg_sum(x, seg):
    hi, lo = _split2(x)
    return _dot(hi, seg) + _dot(lo, seg)


def _mask_dot(mask01, x):
    hi, mid, lo = _split3(x)
    return _dot(mask01, hi) + _dot(mask01, mid) + _dot(mask01, lo)


def _softplus(x):
    return jnp.maximum(x, 0.0) + jnp.log1p(jnp.exp(-jnp.abs(x)))


def _sigmoid(x):
    return jax.nn.sigmoid(x)


def _silu(x):
    return x * jax.nn.sigmoid(x)


def _ada_kernel(c_ref, w_ref, b_ref, o_ref):
    s = _silu(c_ref[...]).astype(BF16)
    o_ref[0] = _dot(s, w_ref[0].astype(BF16)) + b_ref[0]


def _ada(cond, w_ada, b_ada):
    n = w_ada.shape[-1]
    tn = ADA_COL_TILE
    return pl.pallas_call(
        _ada_kernel,
        grid=(DEPTH, n // tn),
        in_specs=[pl.BlockSpec((MOD_ROWS, D_MODEL), lambda l, j: (0, 0)),
                  pl.BlockSpec((1, D_MODEL, tn), lambda l, j: (l, 0, j)),
                  pl.BlockSpec((1, 1, tn), lambda l, j: (l, 0, j))],
        out_specs=pl.BlockSpec((1, MOD_ROWS, tn), lambda l, j: (l, 0, j)),
        out_shape=jax.ShapeDtypeStruct((DEPTH, MOD_ROWS, n), F32),
        compiler_params=pltpu.CompilerParams(dimension_semantics=("arbitrary", "arbitrary"),
                                             vmem_limit_bytes=VMEM_LIMIT),
        name="ada",
    )(cond, w_ada, b_ada.reshape(DEPTH, 1, n))


def _premix_kernel(x_ref, g_ref, sc_ref, sh_ref, w_ref, og_ref, oh_ref, oa_ref, vt_ref):
    h = _rms(x_ref[...], g_ref[...]) * (1.0 + sc_ref[0]) + sh_ref[0]
    hb = h.astype(BF16)
    og_ref[...] = _dot(hb, w_ref[:, 0:G_W])
    oh = _dot(hb, w_ref[:, G_W:G_W + H_W])
    oh_ref[...] = oh
    oa_ref[...] = _dot(hb, w_ref[:, G_W + H_W:G_W + H_W + A_W])
    vt_ref[0] = oh[:, LIN_W:2 * LIN_W].T


def _premix(x, g, sc, sh, w, tm, t_len):
    m = x.shape[0]
    tiles = m // tm
    per = tiles // sc.shape[0]
    row = lambda i: (i // per, 0, 0)
    tps = t_len // tm
    seq = lambda i: (i // tps, 0, i % tps)
    wt = G_W + H_W + A_W
    return pl.pallas_call(
        _premix_kernel,
        grid=(tiles,),
        in_specs=[pl.BlockSpec((tm, D_MODEL), lambda i: (i, 0)),
                  pl.BlockSpec((1, D_MODEL), lambda i: (0, 0)),
                  pl.BlockSpec((1, 1, D_MODEL), row),
                  pl.BlockSpec((1, 1, D_MODEL), row),
                  pl.BlockSpec((D_MODEL, wt), lambda i: (0, 0))],
        out_specs=[pl.BlockSpec((tm, G_W), lambda i: (i, 0)),
                   pl.BlockSpec((tm, H_W), lambda i: (i, 0)),
                   pl.BlockSpec((tm, A_W), lambda i: (i, 0)),
                   pl.BlockSpec((1, LIN_W, tm), seq)],
        out_shape=[jax.ShapeDtypeStruct((m, G_W), F32),
                   jax.ShapeDtypeStruct((m, H_W), F32),
                   jax.ShapeDtypeStruct((m, A_W), F32),
                   jax.ShapeDtypeStruct((m // t_len, LIN_W, t_len), F32)],
        compiler_params=pltpu.CompilerParams(dimension_semantics=("arbitrary",), vmem_limit_bytes=VMEM_LIMIT),
        name="premix",
    )(x, g, sc, sh, w)


def _keys_from_latent(ckv, krot, wukv_ref, k_out, v_out):
    kv = _dot(ckv.astype(BF16), wukv_ref[...])
    for h in range(HEADS):
        k_out[:, MLA_HEAD_PAD * h:MLA_HEAD_PAD * h + MLA_NOPE] = kv[:, MLA_NOPE * h:MLA_NOPE * (h + 1)].astype(BF16)
        k_out[:, MLA_HEAD_PAD * h + MLA_NOPE:MLA_HEAD_PAD * (h + 1)] = krot
    v_out[...] = kv[:, HEADS * MLA_NOPE:].astype(BF16)


def _mla_prep_kernel(tps, past, a_ref, qnw_ref, kvnw_ref, wuq_ref, wukv_ref, qa_ref, qb_ref, ck_ref, sk_ref, *refs):
    if past:
        cache_ckv_ref, cache_kr_ref, q_out, k_out, v_out, ckv_out = refs
    else:
        q_out, k_out, v_out, ckv_out = refs
    j = pl.program_id(1)

    @pl.when(j < tps)
    def _():
        qn = _rms(a_ref[:, 0:MLA_Q_RANK], qnw_ref[...])
        y = _dot(qn.astype(BF16), wuq_ref[...])
        z = pltpu.roll(y, HEADS * MLA_HEAD_PAD - MLA_ROPE, 1)
        qa = qa_ref[...]
        qb = qb_ref[...]
        for h in range(HEADS):
            sl = slice(MLA_HEAD_PAD * h, MLA_HEAD_PAD * (h + 1))
            q_out[:, sl] = (y[:, sl] * qa + z[:, sl] * qb).astype(BF16)
        ckv = _rms(a_ref[:, MLA_Q_RANK:A_KR], kvnw_ref[...])
        ckv_out[...] = ckv
        kr = a_ref[:, A_KR:A_AB]
        krot = (kr * ck_ref[...] + pltpu.roll(kr, 64, 1) * sk_ref[...]).astype(BF16)
        _keys_from_latent(ckv, krot, wukv_ref, k_out.at[0], v_out.at[0])

    if past:
        @pl.when(j == tps)
        def _():
            _keys_from_latent(cache_ckv_ref[0], cache_kr_ref[0].astype(BF16), wukv_ref,
                              k_out.at[0, 0:past], v_out.at[0, 0:past])


def _mla_prep(oa, qnw, kvnw, wuq, wukv, qa, qb, ck, sk, tm, t_len, cache=None):
    m = oa.shape[0]
    b = m // t_len
    tps = t_len // tm
    past = 0 if cache is None else cache[0].shape[1]
    tok = lambda bi, j: (bi * tps + jnp.minimum(j, tps - 1), 0)
    pos = lambda bi, j: (jnp.minimum(j, tps - 1), 0)
    full = lambda bi, j: (0, 0)
    seq = lambda bi, j: (bi, j, 0)
    kw = HEADS * MLA_HEAD_PAD
    in_specs = [pl.BlockSpec((tm, A_W), tok),
                pl.BlockSpec((1, MLA_Q_RANK), full),
                pl.BlockSpec((1, MLA_KV_RANK), full),
                pl.BlockSpec((MLA_Q_RANK, kw), full),
                pl.BlockSpec((MLA_KV_RANK, HEADS * (MLA_NOPE + MLA_DV)), full),
                pl.BlockSpec((tm, MLA_HEAD_PAD), pos),
                pl.BlockSpec((tm, MLA_HEAD_PAD), pos),
                pl.BlockSpec((tm, 128), pos),
                pl.BlockSpec((tm, 128), pos)]
    operands = [oa, qnw, kvnw, wuq, wukv, qa, qb, ck, sk]
    if past:
        in_specs += [pl.BlockSpec((1, past, MLA_KV_RANK), lambda bi, j: (bi, 0, 0)),
                     pl.BlockSpec((1, past, 128), lambda bi, j: (bi, 0, 0))]
        operands += list(cache)
    return pl.pallas_call(
        functools.partial(_mla_prep_kernel, tps, past),
        grid=(b, tps + (1 if past else 0)),
        in_specs=in_specs,
        out_specs=[pl.BlockSpec((tm, kw), tok),
                   pl.BlockSpec((1, tm, kw), seq),
                   pl.BlockSpec((1, tm, HEADS * MLA_DV), seq),
                   pl.BlockSpec((tm, MLA_KV_RANK), tok)],
        out_shape=[jax.ShapeDtypeStruct((m, kw), BF16),
                   jax.ShapeDtypeStruct((b, t_len + past, kw), BF16),
                   jax.ShapeDtypeStruct((b, t_len + past, HEADS * MLA_DV), BF16),
                   jax.ShapeDtypeStruct((m, MLA_KV_RANK), F32)],
        compiler_params=pltpu.CompilerParams(dimension_semantics=("arbitrary", "arbitrary"),
                                             vmem_limit_bytes=VMEM_LIMIT),
        name="mla_prep",
    )(*operands)


def _attn_kernel(q_ref, k_ref, v_ref, o_ref):
    for h in range(HEADS):
        sl = slice(MLA_HEAD_PAD * h, MLA_HEAD_PAD * (h + 1))
        s = _dot_t(q_ref[0, :, sl], k_ref[0, :, sl])
        p = jnp.exp2(s - jnp.max(s, axis=-1, keepdims=True))
        l = jnp.sum(p, axis=-1, keepdims=True)
        o = _dot(p.astype(BF16), v_ref[0, :, MLA_DV * h:MLA_DV * (h + 1)])
        o_ref[0, :, MLA_DV * h:MLA_DV * (h + 1)] = (o / l).astype(o_ref.dtype)


def _attn(q, k, v, tq):
    b, t, kw = q.shape
    s = k.shape[1]
    return pl.pallas_call(
        _attn_kernel,
        grid=(b, t // tq),
        in_specs=[pl.BlockSpec((1, tq, kw), lambda bi, i: (bi, i, 0)),
                  pl.BlockSpec((1, s, kw), lambda bi, i: (bi, 0, 0)),
                  pl.BlockSpec((1, s, HEADS * MLA_DV), lambda bi, i: (bi, 0, 0))],
        out_specs=pl.BlockSpec((1, tq, HEADS * MLA_DV), lambda bi, i: (bi, i, 0)),
        out_shape=jax.ShapeDtypeStruct((b, t, HEADS * MLA_DV), BF16),
        compiler_params=pltpu.CompilerParams(dimension_semantics=("arbitrary", "arbitrary"),
                                             vmem_limit_bytes=VMEM_LIMIT),
        name="attn",
    )(q, k, v)


def _gdn_prep_kernel(tps, cur_ref, prev_ref, next_ref, w_ref, seg_ref, q_out, k_out, v_out, kt_out):
    i = pl.program_id(0)
    tm = cur_ref.shape[0]
    has_prev = ((i % tps) != 0).astype(F32)
    has_next = ((i % tps) != (tps - 1)).astype(F32)
    xc = jnp.concatenate([prev_ref[...] * has_prev, cur_ref[...], next_ref[...] * has_next], axis=0)
    n = tm + 16
    y = xc[8:8 + tm] * w_ref[2:3, :]
    for j in (0, 1, 3, 4):
        d = j - 2
        y = y + pltpu.roll(xc, (n - d) % n, 0)[8:8 + tm] * w_ref[j:j + 1, :]
    y = _silu(y)
    seg = seg_ref[...]
    qk = y[:, 0:2 * LIN_W]
    qk = qk * lax.rsqrt(_seg_sum(qk * qk, seg) + EPS)
    q_out[...] = qk[:, 0:LIN_W] * (HD ** -0.5)
    k_out[...] = qk[:, LIN_W:2 * LIN_W]
    kt_out[0] = qk[:, LIN_W:2 * LIN_W].T
    v_out[...] = y[:, 2 * LIN_W:]


def _gdn_prep(og, convw, seg, tm, t_len):
    m = og.shape[0]
    tiles = m // tm
    tps = t_len // tm
    r8 = tm // 8
    last8 = m // 8 - 1
    return pl.pallas_call(
        functools.partial(_gdn_prep_kernel, tps),
        grid=(tiles,),
        in_specs=[pl.BlockSpec((tm, G_CONV), lambda i: (i, 0)),
                  pl.BlockSpec((8, G_CONV), lambda i: (jnp.maximum(i * r8 - 1, 0), 0)),
                  pl.BlockSpec((8, G_CONV), lambda i: (jnp.minimum((i + 1) * r8, last8), 0)),
                  pl.BlockSpec((8, G_CONV), lambda i: (0, 0)),
                  pl.BlockSpec((2 * LIN_W, 2 * LIN_W), lambda i: (0, 0))],
        out_specs=[pl.BlockSpec((tm, LIN_W), lambda i: (i, 0)),
                   pl.BlockSpec((tm, LIN_W), lambda i: (i, 0)),
                   pl.BlockSpec((tm, LIN_W), lambda i: (i, 0)),
                   pl.BlockSpec((1, LIN_W, tm), lambda i: (i // tps, 0, i % tps))],
        out_shape=[jax.ShapeDtypeStruct((m, LIN_W), F32),
                   jax.ShapeDtypeStruct((m, LIN_W), F32),
                   jax.ShapeDtypeStruct((m, LIN_W), F32),
                   jax.ShapeDtypeStruct((m // t_len, LIN_W, t_len), F32)],
        compiler_params=pltpu.CompilerParams(dimension_semantics=("arbitrary",), vmem_limit_bytes=VMEM_LIMIT),
        name="gdn_prep",
    )(og, og, og, convw, seg)


def _cumsum_masks():
    r = np.arange(SCAN_BLOCK)[:, None]
    c = np.arange(SCAN_BLOCK)[None, :]
    masks = []
    for chunk in (GDN_CHUNK, HG_CHUNK):
        same = (r // chunk) == (c // chunk)
        masks += [same & (r >= c), same & (r <= c)]
    return jnp.asarray(np.stack(masks).astype(np.float32), dtype=BF16)


def _gdn_units(d, incl_blk, q_ref, k_ref, v_ref, kt_ref, ab_ref, gc_ref, o_ref):
    c = GDN_CHUNK
    ri = lax.broadcasted_iota(jnp.int32, (c, c), 0)
    ci = lax.broadcasted_iota(jnp.int32, (c, c), 1)
    incl = (ri >= ci) if d == 0 else (ri <= ci)
    strict = (ri > ci) if d == 0 else (ri < ci)
    last = c - 1 if d == 0 else 0
    ab = ab_ref[0]
    g_cols = gc_ref[0:1, :] * _softplus(ab + gc_ref[1:2, :])
    beta = _sigmoid(ab)
    gcs_blk = _mask_dot(incl_blk, g_cols)
    gcr_blk = gcs_blk.T
    n_chunks = SCAN_BLOCK // c
    units = []
    for rank, cc in enumerate(range(n_chunks) if d == 0 else range(n_chunks - 1, -1, -1)):
        rows = slice(cc * c, (cc + 1) * c)
        for h in range(HEADS):
            j = d * HEADS + h
            hs = slice(HD * h, HD * (h + 1))
            gcol = gcs_blk[rows, j:j + 1]
            grow = gcr_blk[j:j + 1, rows]
            bcol = beta[rows, 8 + j:9 + j]
            gl = gcol[last:last + 1, :]
            eg = jnp.exp(gcol)
            k = k_ref[0, rows, hs]
            units.append(dict(
                rank=rank, j=j, rows=rows, hs=hs, o_ref=o_ref, strict=strict, gl=gl, bcol=bcol, eg=eg,
                q=q_ref[0, rows, hs], k=k,
                k_tail=(kt_ref[0, hs, rows] * jnp.exp(gl - grow)).astype(BF16),
                decay=jnp.where(incl, jnp.exp(jnp.where(incl, gcol - grow, 0.0)), 0.0),
                x=jnp.concatenate([v_ref[0, rows, hs] * bcol, k * (bcol * eg)], axis=1)))
    return units


def _gdn_stages(units, s_ref):
    c = GDN_CHUNK
    ri = lax.broadcasted_iota(jnp.int32, (c, c), 0)
    ci = lax.broadcasted_iota(jnp.int32, (c, c), 1)
    diag_blk = jnp.right_shift(ri, 4) == jnp.right_shift(ci, 4)
    eye = (ri == ci).astype(F32)

    def gram():
        for u in units:
            kb16 = u["k"].astype(BF16)
            u["kk"] = _dot_t((u["k"] * u["bcol"]).astype(BF16), kb16)
            u["qk"] = _dot_t(u["q"].astype(BF16), kb16)

    def split():
        for u in units:
            p = jnp.where(u["strict"], u["kk"] * u["decay"], 0.0)
            pd = jnp.where(diag_blk, p, 0.0)
            u["att"] = (u["qk"] * u["decay"]).astype(BF16)
            u["poff"] = (p - pd).astype(BF16)
            u["td"] = eye - pd
            u["a"] = pd.astype(BF16)

    def square():
        for u in units:
            u["a"] = _dot(u["a"], u["a"]).astype(BF16)

    def extend():
        for u in units:
            u["td"] = u["td"] + _dot(u["td"].astype(BF16), u["a"])

    def apply_diag():
        for u in units:
            yn = _dot(u["td"].astype(BF16), jnp.concatenate([u["x"].astype(BF16), u["poff"]], axis=1))
            u["y"] = yn[:, 0:2 * HD]
            u["n"] = yn[:, 2 * HD:3 * HD].astype(BF16)
            u["x"] = u["y"]

    def substitute():
        for u in units:
            u["x"] = u["y"] - _dot(u["n"], u["x"].astype(BF16))

    def new_values(rank):
        for u in units:
            if u["rank"] == rank:
                u["s"] = s_ref[0, u["j"]]
                u["sb"] = u["s"].astype(BF16)
                u["vb"] = (u["x"][:, 0:HD] - _dot(u["x"][:, HD:2 * HD].astype(BF16), u["sb"])).astype(BF16)

    def emit(rank):
        for u in units:
            if u["rank"] == rank:
                u["o_ref"][0, u["rows"], u["hs"]] = (_dot((u["q"] * u["eg"]).astype(BF16), u["sb"])
                                                     + _dot(u["att"], u["vb"]))
                s_ref[0, u["j"]] = u["s"] * jnp.exp(u["gl"]) + _dot(u["k_tail"], u["vb"])

    stages = [gram, split] + [square, extend] * 3 + [apply_diag] + [substitute] * 3
    for rank in range(SCAN_BLOCK // c):
        stages += [functools.partial(new_values, rank), functools.partial(emit, rank)]
    return stages


def _hgrn_lower_bound(layer, lb_ref):
    raw = lb_ref[0:DEPTH, :]
    e = jnp.exp(raw - jnp.max(raw, axis=0, keepdims=True))
    gamma = e / jnp.sum(e, axis=0, keepdims=True)
    lb = jnp.zeros((1, LIN_W), F32)
    for i in range(1, layer + 1):
        lb = lb + gamma[i:i + 1, :]
    return lb


def _hgrn_block(d, incl01, lb, q_ref, v_ref, f_ref):
    c = HG_CHUNK
    oml = 1.0 - lb
    q = q_ref[0]
    f = f_ref[0]
    logf = jnp.log(jnp.maximum(lb + oml * _sigmoid(f), GATE_FLOOR))
    bc = _mask_dot(incl01, logf)
    bc2 = bc * LOG2E
    src2 = bc2 - (jnp.log2(oml) - _softplus(f) * LOG2E)
    return dict(d=d, q=q, v=v_ref[0], kk=oml * _sigmoid(-f), bc=bc, bc2=bc2, src2=src2,
                qe=(q * jnp.exp(bc)).astype(BF16), last=c - 1 if d == 0 else 0)


def _hgrn_live_halves(d, s):
    half = HG_CHUNK // 2
    if d == 0:
        return (1,) if s >= half else (0, 1)
    return (0,) if s < half else (0, 1)


def _hgrn_chunk_build(blk, cc):
    c = HG_CHUNK
    half = c // 2
    d = blk["d"]
    base = cc * c
    rows = slice(base, base + c)
    kk, bc = blk["kk"][rows], blk["bc"][rows]
    bl = bc[blk["last"]:blk["last"] + 1, :]
    row_half = lax.broadcasted_iota(jnp.int32, (half, LIN_W), 0)
    zero_half = jnp.zeros((half, LIN_W), F32)
    parts = []
    for s in range(c):
        bs = blk["src2"][base + s:base + s + 1]
        for hf in (0, 1):
            if hf not in _hgrn_live_halves(d, s):
                parts.append(zero_half)
                continue
            tr = slice(base + hf * half, base + (hf + 1) * half)
            col = blk["q"][tr] * jnp.exp2(blk["bc2"][tr] - bs)
            if hf == s // half:
                rid = row_half + hf * half
                col = jnp.where((rid >= s) if d == 0 else (rid <= s), col, 0.0)
            parts.append(col)
    return dict(cols=jnp.concatenate(parts, axis=0).astype(BF16), k_tail=(kk * jnp.exp(bl - bc)).astype(BF16),
                decay=jnp.exp(bl))


def _hgrn_chunk_launch(blk, cc, built, vt_ref, ones, st):
    rows = slice(cc * HG_CHUNK, (cc + 1) * HG_CHUNK)
    return dict(
        r=_dot(built["cols"], ones),
        o_inter=_dot_t(blk["qe"][rows], st.astype(BF16)),
        upd=_dot(vt_ref[0, :, rows].astype(BF16), built["k_tail"]),
        decay=built["decay"])


def _hgrn_chunk_finish(blk, cc, pend, seg_mask, st, o_ref):
    c = HG_CHUNK
    half = c // 2
    base = cc * c
    r = pend["r"]
    o_half = [pend["o_inter"][0:half], pend["o_inter"][half:c]]
    for s in range(c):
        vs = blk["v"][base + s:base + s + 1]
        for hf in _hgrn_live_halves(blk["d"], s):
            o_half[hf] = o_half[hf] + r[s * c + hf * half:s * c + (hf + 1) * half, :] * vs
    o_ref[0, base:base + half, :] = o_half[0]
    o_ref[0, base + half:base + c, :] = o_half[1]
    return st * pend["decay"] + pend["upd"] * seg_mask


def _hgrn_steps(lb, masks_ref, fwd_refs, bwd_refs, ones, st_ref):
    seg_mask = ones.astype(F32)
    (qf, vf, ff, vtf, of_ref), (qb, vb, fb, vtb, ob_ref) = fwd_refs, bwd_refs
    fwd = _hgrn_block(0, masks_ref[2], lb, qf, vf, ff)
    bwd = _hgrn_block(1, masks_ref[3], lb, qb, vb, fb)
    carry = {0: st_ref[0, 0], 1: st_ref[0, 1]}
    built, pending = {}, {}
    n_chunks = SCAN_BLOCK // HG_CHUNK
    chunk_of = {0: lambda step: step, 1: lambda step: n_chunks - 1 - step}
    side = {0: (fwd, vtf, of_ref), 1: (bwd, vtb, ob_ref)}

    def build(d, step):
        built[(d, step)] = _hgrn_chunk_build(side[d][0], chunk_of[d](step))

    def launch(d, step):
        blk, vt_ref, _ = side[d]
        pending[(d, step)] = _hgrn_chunk_launch(blk, chunk_of[d](step), built.pop((d, step)), vt_ref, ones, carry[d])

    def finish(d, step):
        blk, _, o_ref = side[d]
        carry[d] = _hgrn_chunk_finish(blk, chunk_of[d](step), pending.pop((d, step)), seg_mask, carry[d], o_ref)

    def flush():
        st_ref[0, 0] = carry[0]
        st_ref[0, 1] = carry[1]

    groups = []
    for slot in range(n_chunks + 2):
        group = []
        for phase, step in ((finish, slot - 2), (launch, slot - 1), (build, slot)):
            if 0 <= step < n_chunks:
                group += [functools.partial(phase, 0, step), functools.partial(phase, 1, step)]
        groups.append(group)
    return groups, flush


def _scan_kernel(layer, masks_ref,
                 gqf, gkf, gvkf, gktf, gabf, gqb, gkb, gvkb, gktb, gabb, gc_ref, gs0_ref,
                 hqf, hvf, hff, hvtf, hqb, hvb, hfb, hvtb, lb_ref, ones_ref, hs0_ref,
                 gof_ref, gob_ref, gs_ref, hof_ref, hob_ref, hst_ref):
    @pl.when(pl.program_id(1) == 0)
    def _():
        gs_ref[...] = gs0_ref[...]
        hst_ref[...] = hs0_ref[...]

    units = (_gdn_units(0, masks_ref[0], gqf, gkf, gvkf, gktf, gabf, gc_ref, gof_ref)
             + _gdn_units(1, masks_ref[1], gqb, gkb, gvkb, gktb, gabb, gc_ref, gob_ref))
    stages = _gdn_stages(units, gs_ref)
    groups, flush = _hgrn_steps(_hgrn_lower_bound(layer, lb_ref), masks_ref, (hqf, hvf, hff, hvtf, hof_ref),
                                (hqb, hvb, hfb, hvtb, hob_ref), ones_ref[...], hst_ref)
    per_stage = -(-len(groups) // len(stages))
    for stage in stages:
        stage()
        for group in groups[:per_stage]:
            for step in group:
                step()
        groups = groups[per_stage:]
    for group in groups:
        for step in group:
            step()
    flush()


def _lin_scan(layer, gq, gk, gvk, gkt, oa, gconst, gs0, oh, vt, lb, ones, hs0):
    b, t, _ = gq.shape
    blk = SCAN_BLOCK
    nb = t // blk

    def tok(cb, rev):
        return (lambda bi, i: (bi, nb - 1 - i, cb)) if rev else (lambda bi, i: (bi, i, cb))

    def lanes(rev):
        return (lambda bi, i: (bi, 0, nb - 1 - i)) if rev else (lambda bi, i: (bi, 0, i))

    def gdn_specs(rev):
        return [pl.BlockSpec((1, blk, LIN_W), tok(0, rev)),
                pl.BlockSpec((1, blk, LIN_W), tok(0, rev)),
                pl.BlockSpec((1, blk, LIN_W), tok(0, rev)),
                pl.BlockSpec((1, LIN_W, blk), lanes(rev)),
                pl.BlockSpec((1, blk, 128), tok(A_AB // 128, rev))]

    def hgrn_specs(rev):
        f_col = 3 if rev else 2
        return [pl.BlockSpec((1, blk, LIN_W), tok(0, rev)),
                pl.BlockSpec((1, blk, LIN_W), tok(1, rev)),
                pl.BlockSpec((1, blk, LIN_W), tok(f_col, rev)),
                pl.BlockSpec((1, LIN_W, blk), lanes(rev))]

    const = lambda r, c: pl.BlockSpec((r, c), lambda bi, i: (0, 0))
    g_state = pl.BlockSpec((1, 2 * HEADS, HD, HD), lambda bi, i: (bi, 0, 0, 0))
    h_state = pl.BlockSpec((1, 2, LIN_W, LIN_W), lambda bi, i: (bi, 0, 0, 0))
    out_tok = lambda rev: pl.BlockSpec((1, blk, LIN_W), tok(0, rev))
    seq = jax.ShapeDtypeStruct((b, t, LIN_W), F32)
    return pl.pallas_call(
        functools.partial(_scan_kernel, layer),
        grid=(b, nb),
        in_specs=[pl.BlockSpec((4, blk, blk), lambda bi, i: (0, 0, 0))]
        + gdn_specs(False) + gdn_specs(True) + [const(8, 128), g_state]
        + hgrn_specs(False) + hgrn_specs(True) + [const(8, LIN_W), const(LIN_W, LIN_W), h_state],
        out_specs=[out_tok(False), out_tok(True), g_state, out_tok(False), out_tok(True), h_state],
        out_shape=[seq, seq, jax.ShapeDtypeStruct((b, 2 * HEADS, HD, HD), F32),
                   seq, seq, jax.ShapeDtypeStruct((b, 2, LIN_W, LIN_W), F32)],
        compiler_params=pltpu.CompilerParams(dimension_semantics=("arbitrary", "arbitrary"),
                                             vmem_limit_bytes=VMEM_LIMIT),
        name="lin_scan",
    )(_cumsum_masks(), gq, gk, gvk, gkt, oa, gq, gk, gvk, gkt, oa, gconst, gs0,
      oh, oh, oh, vt, oh, oh, oh, vt, lb, ones, hs0)


def _post_kernel(x_ref, ogf_ref, ogb_ref, z_ref, gnw_ref, ohf_ref, ohb_ref, hg_ref, hnw_ref, om_ref, seg_ref,
                 wout_ref, gpm_ref, gt1_ref, gpf_ref, sc2_ref, sh2_ref, gt2_ref, gff_ref,
                 wf_ref, wo_ref, out_ref):
    seg = seg_ref[...]
    inv = 1.0 / HD
    og = ogf_ref[...] + ogb_ref[...]
    og = og * lax.rsqrt(_seg_sum(og * og, seg) * inv + EPS) * gnw_ref[...] * _silu(z_ref[...])
    oh = ohf_ref[...] + ohb_ref[...]
    oh = oh * lax.rsqrt(_seg_sum(oh * oh, seg) * inv + EPS) * hnw_ref[...] * _sigmoid(hg_ref[...])
    mix = jnp.concatenate([og.astype(BF16), oh.astype(BF16), om_ref[...].astype(BF16)], axis=-1)
    x1 = x_ref[...] + gt1_ref[0] * _rms(_dot(mix, wout_ref[...]), gpm_ref[...])
    hb = (_rms(x1, gpf_ref[...]) * (1.0 + sc2_ref[0]) + sh2_ref[0]).astype(BF16)
    fw = D_FF // FF_SPLIT
    y = None
    for part in range(FF_SPLIT):
        cs = slice(part * fw, (part + 1) * fw)
        gs = slice(D_FF + part * fw, D_FF + (part + 1) * fw)
        act = (_silu(_dot(hb, wf_ref[:, cs])) * _dot(hb, wf_ref[:, gs])).astype(BF16)
        contrib = _dot(act, wo_ref[cs, :])
        y = contrib if y is None else y + contrib
    out_ref[...] = x1 + gt2_ref[0] * _rms(y, gff_ref[...])


def _post(x, ogf, ogb, og, gnw, ohf, ohb, oh, hnw, om, seg, wout, gpm, gt1, gpf, sc2, sh2, gt2, gff, wf, wo, tm):
    m = x.shape[0]
    tiles = m // tm
    per = tiles // gt1.shape[0]
    row = lambda i: (i // per, 0, 0)
    tok = lambda w: pl.BlockSpec((tm, w), lambda i: (i, 0))
    vec = lambda w: pl.BlockSpec((1, w), lambda i: (0, 0))
    mod = pl.BlockSpec((1, 1, D_MODEL), row)
    once = lambda r, c: pl.BlockSpec((r, c), lambda i: (0, 0), pipeline_mode=pl.Buffered(1))
    return pl.pallas_call(
        _post_kernel,
        grid=(tiles,),
        in_specs=[tok(D_MODEL), tok(LIN_W), tok(LIN_W),
                  pl.BlockSpec((tm, LIN_W), lambda i: (i, G_W // LIN_W - 1)), vec(LIN_W),
                  tok(LIN_W), tok(LIN_W),
                  pl.BlockSpec((tm, LIN_W), lambda i: (i, H_W // LIN_W - 1)), vec(LIN_W),
                  tok(HEADS * MLA_DV), once(LIN_W, LIN_W),
                  once(D_MODEL, D_MODEL), vec(D_MODEL), mod, vec(D_MODEL), mod, mod, mod, vec(D_MODEL),
                  once(D_MODEL, 2 * D_FF), once(D_FF, D_MODEL)],
        out_specs=tok(D_MODEL),
        out_shape=jax.ShapeDtypeStruct((m, D_MODEL), F32),
        compiler_params=pltpu.CompilerParams(dimension_semantics=("arbitrary",), vmem_limit_bytes=VMEM_LIMIT),
        name="post",
    )(x, ogf, ogb, og, gnw, ohf, ohb, oh, hnw, om, seg, wout, gpm, gt1, gpf, sc2, sh2, gt2, gff, wf, wo)


def _rope_tables(t_len, use_rope):
    scale = MLA_QK ** -0.5 * LOG2E
    if use_rope:
        rows = t_len // GRID_W
        row = np.repeat(np.arange(rows, dtype=np.float64), GRID_W)
        col = np.tile(np.arange(GRID_W, dtype=np.float64), rows)
        nf = MLA_ROPE // 4
        inv = ROPE_BASE ** (-np.arange(nf, dtype=np.float64) / nf)
        ar, ac = row[:, None] * inv, col[:, None] * inv
        cos = np.concatenate([np.cos(ar), np.cos(ar), np.cos(ac), np.cos(ac)], axis=-1)
        sin = np.concatenate([-np.sin(ar), np.sin(ar), -np.sin(ac), np.sin(ac)], axis=-1)
    else:
        cos = np.ones((t_len, MLA_ROPE))
        sin = np.zeros((t_len, MLA_ROPE))
    z64 = np.zeros((t_len, MLA_ROPE))
    qa = scale * np.concatenate([np.ones((t_len, MLA_NOPE)), cos, z64], axis=-1)
    qb = scale * np.concatenate([np.zeros((t_len, MLA_NOPE)), sin, z64], axis=-1)
    ck = np.concatenate([cos, z64], axis=-1)
    sk = np.concatenate([sin, z64], axis=-1)
    return tuple(jnp.asarray(a, dtype=F32) for a in (qa, qb, ck, sk))


def _layer_weights(l, w_in, w_out, gdn_conv_w, gdn_a_log, gdn_dt_bias, gdn_norm_w, lower_bounds, hgrn_norm_w,
                   mla_q_norm_w, mla_w_uq, mla_kv_norm_w, mla_w_ukv, w_ffn_in, w_ffn_out,
                   g_pre_mix, g_post_mix, g_pre_ffn, g_post_ffn):
    w_cat = _w_in_layout(w_in[l].astype(BF16))
    convw = jnp.concatenate([gdn_conv_w[l].T, jnp.zeros((3, G_CONV), F32)], axis=0)
    neg_a = -jnp.exp(gdn_a_log[l].astype(F32)).reshape(1, 8)
    dt = gdn_dt_bias[l].astype(F32).reshape(1, 8)
    pad = lambda r: jnp.pad(r, ((0, 0), (0, 128 - r.shape[1])))
    gconst = jnp.concatenate([pad(neg_a), pad(dt), jnp.zeros((6, 128), F32)], axis=0)
    vec = lambda a: a.reshape(1, -1).astype(F32)
    return dict(
        w_cat=w_cat, convw=convw, gconst=gconst,
        gnw=vec(jnp.tile(gdn_norm_w[l], HEADS)), hnw=vec(jnp.tile(hgrn_norm_w[l], HEADS)),
        layer=l, lb=lower_bounds,
        qnw=vec(mla_q_norm_w[l]), kvnw=vec(mla_kv_norm_w[l]),
        wuq=_w_uq_layout(mla_w_uq[l].astype(BF16)), wukv=_w_ukv_layout(mla_w_ukv[l].astype(BF16)),
        wout=w_out[l].astype(BF16), wf=w_ffn_in[l].astype(BF16),
        wo=w_ffn_out[l].astype(BF16),
        g_pre_mix=vec(g_pre_mix[l]), g_post_mix=vec(g_post_mix[l]),
        g_pre_ffn=vec(g_pre_ffn[l]), g_post_ffn=vec(g_post_ffn[l]))


def _trunk_layer(x, mod, w, gdn_s0, hg_s0, tables, ctx_cache, seg512, seg256):
    b, t, _ = x.shape
    m = b * t
    tm = min(TOKEN_TILE, t)
    sh1, sc1, gt1, sh2, sc2, gt2 = mod
    xf = x.reshape(m, D_MODEL)
    og, oh, oa, vt = _premix(xf, w["g_pre_mix"], sc1, sh1, w["w_cat"], tm, t)

    qa, qb, ck, sk = tables
    qcat, kcat, vcat, ckv = _mla_prep(oa, w["qnw"], w["kvnw"], w["wuq"], w["wukv"], qa, qb, ck, sk, tm, t, ctx_cache)
    o_mla = _attn(qcat.reshape(b, t, HEADS * MLA_HEAD_PAD), kcat, vcat, min(ATTN_Q_TILE, t)).reshape(m, HEADS * MLA_DV)

    gq, gk, gvk, gkt = _gdn_prep(og, w["convw"], seg512, tm, t)
    ogf, ogb, gdn_state, ohf, ohb, hg_state = _lin_scan(
        w["layer"], gq.reshape(b, t, LIN_W), gk.reshape(b, t, LIN_W), gvk.reshape(b, t, LIN_W), gkt,
        oa.reshape(b, t, A_W), w["gconst"], gdn_s0,
        oh.reshape(b, t, H_W), vt, w["lb"], seg256, hg_s0)

    x_new = _post(xf, ogf.reshape(m, LIN_W), ogb.reshape(m, LIN_W), og, w["gnw"],
                  ohf.reshape(m, LIN_W), ohb.reshape(m, LIN_W), oh, w["hnw"], o_mla, seg256,
                  w["wout"], w["g_post_mix"], gt1, w["g_pre_ffn"], sc2, sh2, gt2, w["g_post_ffn"],
                  w["wf"], w["wo"], tm)
    mkr = oa[:, A_KR:A_KR + MLA_ROPE]
    return x_new.reshape(b, t, D_MODEL), gdn_state, hg_state, ckv.reshape(b, t, MLA_KV_RANK), mkr.reshape(b, t, MLA_ROPE)


def _hg_state_to_block(s):
    st = jnp.swapaxes(s, -1, -2)
    eye = jnp.eye(HEADS, dtype=s.dtype)
    big = st[:, :, :, :, None, :] * eye[None, None, :, None, :, None]
    return big.reshape(s.shape[0], 2, LIN_W, LIN_W)


def _hg_block_to_state(big):
    b = big.shape[0]
    r = big.reshape(b, 2, HEADS, HD, HEADS, HD)
    diag = jnp.stack([r[:, :, h, :, h, :] for h in range(HEADS)], axis=2)
    return jnp.swapaxes(diag, -1, -2)


def kernel(x_prompt, x_sample, cache_mla_ckv, cache_mla_krope, state_gdn, state_hgrn, c, c_ctx, w_ada, b_ada,
           g_pre_mix, g_post_mix, g_pre_ffn, g_post_ffn, w_in, w_out, gdn_conv_w, gdn_a_log, gdn_dt_bias,
           gdn_norm_w, hgrn_lb, hgrn_norm_w, mla_q_norm_w, mla_w_uq, mla_kv_norm_w, mla_w_ukv, w_ffn_in, w_ffn_out):
    bp, tp, _ = x_prompt.shape
    bd, td, _ = x_sample.shape
    past = cache_mla_ckv.shape[2]
    assert 1 + bd <= MOD_ROWS and tp % SCAN_BLOCK == 0 and td % SCAN_BLOCK == 0

    lower_bounds = jnp.pad(hgrn_lb.astype(F32), ((0, 8 - DEPTH), (0, 0)))

    cond = jnp.concatenate([c_ctx[None, :], c, jnp.zeros((MOD_ROWS - 1 - bd, D_MODEL), F32)], axis=0)
    mod_all = _ada(cond, w_ada, b_ada)

    seg512 = _seg_ones(2 * LIN_W)
    seg256 = _seg_ones(LIN_W)
    tab_ctx = _rope_tables(tp, False)
    tab_lat = _rope_tables(td, True)

    weights = [_layer_weights(l, w_in, w_out, gdn_conv_w, gdn_a_log, gdn_dt_bias, gdn_norm_w, lower_bounds,
                              hgrn_norm_w, mla_q_norm_w, mla_w_uq, mla_kv_norm_w, mla_w_ukv, w_ffn_in, w_ffn_out,
                              g_pre_mix, g_post_mix, g_pre_ffn, g_post_ffn) for l in range(DEPTH)]

    def mods(l, lo, n):
        rows = mod_all[l, lo:lo + n].reshape(n, 1, 6 * D_MODEL)
        return tuple(rows[:, :, i * D_MODEL:(i + 1) * D_MODEL] for i in range(6))

    xp = x_prompt
    zero_g = jnp.zeros((bp, 2 * HEADS, HD, HD), F32)
    zero_h = jnp.zeros((bp, 2, LIN_W, LIN_W), F32)
    ckv_l, kr_l, gs_l, hs_l = [], [], [], []
    for l in range(DEPTH):
        xp, gs, hs, ckv, mkr = _trunk_layer(xp, mods(l, 0, 1), weights[l], zero_g, zero_h, tab_ctx, None,
                                            seg512, seg256)
        ckv_l.append(ckv)
        kr_l.append(mkr)
        gs_l.append(gs.reshape(bp, 2, HEADS, HD, HD))
        hs_l.append(_hg_block_to_state(hs))

    xs = x_sample
    for l in range(DEPTH):
        ctx_cache = (cache_mla_ckv[:, l], jnp.pad(cache_mla_krope[:, l], ((0, 0), (0, 0), (0, 128 - MLA_ROPE))))
        xs, _, _, _, _ = _trunk_layer(xs, mods(l, 1, bd), weights[l],
                                      state_gdn[:, l].reshape(bd, 2 * HEADS, HD, HD),
                                      _hg_state_to_block(state_hgrn[:, l]), tab_lat, ctx_cache, seg512, seg256)

    return (xp, xs, jnp.stack(ckv_l, axis=1), jnp.stack(kr_l, axis=1),
            jnp.stack(gs_l, axis=1), jnp.stack(hs_l, axis=1))
```
